```python
import math
import jax, jax.numpy as jnp
from jax import lax
import numpy as np

D_MODEL = 1024
BATCH = 8
SEQ = 4096
DEPTH = 1

N_META = 16
BLOCK_Q = 128
EPS = 1e-6
NEG_INF = -1e30

SSM_WIDTH = D_MODEL // 2
SSM_GROUP = 16
SSM_GROUPS = SSM_WIDTH // SSM_GROUP
SSM_STATE = 64
DT_MIN = 1e-3
DT_MAX = 1e-1

ATT_HEADS = 4
QK_DIM = 64
V_DIM = 2 * QK_DIM
ATT_WIDTH = ATT_HEADS * V_DIM
QK_COLS = ATT_HEADS * 2 * QK_DIM

IN_COLS = SSM_WIDTH + 2 * QK_COLS + ATT_WIDTH + 2 * D_MODEL
SPLITS = (SSM_WIDTH,
          SSM_WIDTH + QK_COLS,
          SSM_WIDTH + 2 * QK_COLS,
          SSM_WIDTH + 2 * QK_COLS + ATT_WIDTH,
          SSM_WIDTH + 2 * QK_COLS + ATT_WIDTH + D_MODEL)

D_FF = ((8 * D_MODEL // 3 + 127) // 128) * 128
CONV_W = 3

kernel_name = 'hybrid_s5_diffattn_convffn_block'


def rmsnorm(x, g):
    xf = x.astype(jnp.float32)
    r = lax.rsqrt(jnp.mean(xf * xf, axis=-1, keepdims=True) + EPS)
    return (xf * r).astype(x.dtype) * g


def s5_mixer(u, a_re, a_im, log_dt, b_re, b_im, c_re, c_im, d, glu_w, glu_b):
    bsz, seq_len, _ = u.shape
    ug = u.astype(jnp.float32).reshape(bsz, seq_len, SSM_GROUPS, SSM_GROUP)
    a_re = a_re.astype(jnp.float32); a_im = a_im.astype(jnp.float32)
    dt = jnp.exp(log_dt.astype(jnp.float32))[:, None]
    mag = jnp.exp(a_re * dt)
    lb_re = mag * jnp.cos(a_im * dt)
    lb_im = mag * jnp.sin(a_im * dt)
    den = a_re * a_re + a_im * a_im
    n_re = lb_re - 1.0
    f_re = (n_re * a_re + lb_im * a_im) / den
    f_im = (lb_im * a_re - n_re * a_im) / den
    b_re = b_re.astype(jnp.float32); b_im = b_im.astype(jnp.float32)
    bb_re = f_re[..., None] * b_re - f_im[..., None] * b_im
    bb_im = f_re[..., None] * b_im + f_im[..., None] * b_re
    x_re = jnp.einsum('blgc,gpc->blgp', ug, bb_re)
    x_im = jnp.einsum('blgc,gpc->blgp', ug, bb_im)
    shp = (1, seq_len, SSM_GROUPS, SSM_STATE)
    a_re_t = jnp.broadcast_to(lb_re, shp)
    a_im_t = jnp.broadcast_to(lb_im, shp)

    def combine(left, right):
        ar_l, ai_l, br_l, bi_l = left
        ar_r, ai_r, br_r, bi_r = right
        return (ar_r * ar_l - ai_r * ai_l,
                ar_r * ai_l + ai_r * ar_l,
                ar_r * br_l - ai_r * bi_l + br_r,
                ar_r * bi_l + ai_r * br_l + bi_r)

    _, _, s_re, s_im = lax.associative_scan(combine, (a_re_t, a_im_t, x_re, x_im), axis=1)
    y = (jnp.einsum('blgp,gcp->blgc', s_re, c_re.astype(jnp.float32))
         - jnp.einsum('blgp,gcp->blgc', s_im, c_im.astype(jnp.float32))
         + d.astype(jnp.float32) * ug)
    y = jax.nn.gelu(y.reshape(bsz, seq_len, SSM_WIDTH)).astype(u.dtype)
    return y * jax.nn.sigmoid(y @ glu_w + glu_b)


def diff_attention(q, k, v, lam, slopes):
    bsz, seq_len = q.shape[:2]
    nb = seq_len // BLOCK_Q
    qb = jnp.moveaxis(q.reshape(bsz, nb, BLOCK_Q, ATT_HEADS, 2, QK_DIM), 1, 0)
    starts = jnp.arange(nb, dtype=jnp.int32) * BLOCK_Q
    kpos = jnp.arange(seq_len, dtype=jnp.int32)

    def one_block(args):
        q_blk, start = args
        s = jnp.einsum('bqhjd,bkhjd->bhjqk', q_blk, k,
                       preferred_element_type=jnp.float32)
        dist = (start + jnp.arange(BLOCK_Q, dtype=jnp.int32))[:, None] - kpos[None, :]
        s = s - slopes[None, :, None, None, None] * dist.astype(jnp.float32)[None, None, None]
        s = jnp.where(dist[None, None, None] >= 0, s, NEG_INF)
        p = jax.nn.softmax(s, axis=-1)
        w = p[:, :, 0] - lam * p[:, :, 1]
        return jnp.einsum('bhqk,bkhe->bqhe', w.astype(v.dtype), v)

    o = lax.map(one_block, (qb, starts))
    return jnp.moveaxis(o, 0, 1).reshape(bsz, seq_len, ATT_HEADS, V_DIM)


def conv_glu_ffn(h, w_up, conv_w, conv_b, w_down):
    seq_len = h.shape[1]
    up = h @ w_up
    a, b = up[..., :D_FF], up[..., D_FF:]
    a_pad = jnp.pad(a, ((0, 0), (CONV_W - 1, 0), (0, 0)))
    c = (a_pad[:, 0:seq_len] * conv_w[0] + a_pad[:, 1:seq_len + 1] * conv_w[1]
         + a_pad[:, 2:seq_len + 2] * conv_w[2] + conv_b)
    return (jax.nn.gelu(c) * b) @ w_down


def setup_inputs(seed: int = 0) -> dict:
    key = jax.random.key(seed)
    ks = jax.random.split(key, 32)
    f32 = jnp.float32
    nrm = lambda k, shp, s: s * jax.random.normal(k, shp, f32)
    L_ = DEPTH
    return {
        'x': jax.random.normal(ks[0], (BATCH, SEQ, D_MODEL), f32),
        'meta_tokens': nrm(ks[1], (N_META, D_MODEL), 1.0),
        'norm1_g': 1.0 + nrm(ks[2], (L_, D_MODEL), 0.02),
        'w_in': nrm(ks[3], (L_, D_MODEL, IN_COLS), D_MODEL ** -0.5),
        'ssm_a_re': -0.5 + nrm(ks[4], (L_, SSM_GROUPS, SSM_STATE), 0.01),
        'ssm_a_im': math.pi * jnp.arange(SSM_STATE, dtype=f32)[None, None, :]
                    + nrm(ks[5], (L_, SSM_GROUPS, SSM_STATE), 0.01),
        'ssm_log_dt': jax.random.uniform(ks[6], (L_, SSM_GROUPS), f32,
                                         math.log(DT_MIN), math.log(DT_MAX)),
        'ssm_b_re': nrm(ks[7], (L_, SSM_GROUPS, SSM_STATE, SSM_GROUP), (2 * SSM_GROUP) ** -0.5),
        'ssm_b_im': nrm(ks[8], (L_, SSM_GROUPS, SSM_STATE, SSM_GROUP), (2 * SSM_GROUP) ** -0.5),
        'ssm_c_re': nrm(ks[9], (L_, SSM_GROUPS, SSM_GROUP, SSM_STATE), (2 * SSM_STATE) ** -0.5),
        'ssm_c_im': nrm(ks[10], (L_, SSM_GROUPS, SSM_GROUP, SSM_STATE), (2 * SSM_STATE) ** -0.5),
        'ssm_d': nrm(ks[11], (L_, SSM_GROUPS, SSM_GROUP), 1.0),
        'ssm_glu_w': nrm(ks[12], (L_, SSM_WIDTH, SSM_WIDTH), SSM_WIDTH ** -0.5),
        'ssm_glu_b': nrm(ks[13], (L_, SSM_WIDTH), 0.01),
        'q_norm_g': 1.0 + nrm(ks[14], (L_, QK_DIM), 0.02),
        'k_norm_g': 1.0 + nrm(ks[15], (L_, QK_DIM), 0.02),
        'lam_q1': nrm(ks[16], (L_, QK_DIM), 0.1),
        'lam_k1': nrm(ks[17], (L_, QK_DIM), 0.1),
        'lam_q2': nrm(ks[18], (L_, QK_DIM), 0.1),
        'lam_k2': nrm(ks[19], (L_, QK_DIM), 0.1),
        'subln_g': 1.0 + nrm(ks[20], (L_, V_DIM), 0.02),
        'w_ssm_out': nrm(ks[21], (L_, SSM_WIDTH, D_MODEL), SSM_WIDTH ** -0.5),
        'w_att_out': nrm(ks[22], (L_, ATT_WIDTH, D_MODEL), ATT_WIDTH ** -0.5),
        'w_o': nrm(ks[23], (L_, D_MODEL, D_MODEL), D_MODEL ** -0.5),
        'norm2_g': 1.0 + nrm(ks[24], (L_, D_MODEL), 0.02),
        'w_up': nrm(ks[25], (L_, D_MODEL, 2 * D_FF), D_MODEL ** -0.5),
        'conv_w': nrm(ks[26], (L_, CONV_W, D_FF), CONV_W ** -0.5),
        'conv_b': nrm(ks[27], (L_, D_FF), 0.01),
        'w_down': nrm(ks[28], (L_, D_FF, D_MODEL), D_FF ** -0.5),
    }


def reference(x, meta_tokens, norm1_g, w_in, ssm_a_re, ssm_a_im, ssm_log_dt, ssm_b_re, ssm_b_im,
              ssm_c_re, ssm_c_im, ssm_d, ssm_glu_w, ssm_glu_b, q_norm_g, k_norm_g,
              lam_q1, lam_k1, lam_q2, lam_k2, subln_g, w_ssm_out, w_att_out, w_o,
              norm2_g, w_up, conv_w, conv_b, w_down):
    bsz, seq, _ = x.shape
    seq_len = seq + N_META
    seq_pad = -(-seq_len // BLOCK_Q) * BLOCK_Q
    meta = jnp.broadcast_to(meta_tokens[None].astype(x.dtype), (bsz, N_META, D_MODEL))
    h = jnp.concatenate([meta, x, jnp.zeros((bsz, seq_pad - seq_len, D_MODEL), x.dtype)], axis=1)
    slopes = 2.0 ** (-8.0 * jnp.arange(1, ATT_HEADS + 1, dtype=jnp.float32) / ATT_HEADS)
    q_scale = QK_DIM ** -0.5

    for l in range(DEPTH):
        lam_init = 0.8 - 0.6 * math.exp(-0.3 * l)
        hn = rmsnorm(h, norm1_g[l])
        proj = hn @ w_in[l]
        u, q, k, v, g_ssm, g_att = jnp.split(proj, SPLITS, axis=-1)
        y_ssm = s5_mixer(u, ssm_a_re[l], ssm_a_im[l], ssm_log_dt[l], ssm_b_re[l], ssm_b_im[l],
                         ssm_c_re[l], ssm_c_im[l], ssm_d[l], ssm_glu_w[l], ssm_glu_b[l])
        q = rmsnorm(q.reshape(bsz, seq_pad, ATT_HEADS, 2, QK_DIM), q_norm_g[l]) * q_scale
        k = rmsnorm(k.reshape(bsz, seq_pad, ATT_HEADS, 2, QK_DIM), k_norm_g[l])
        v = v.reshape(bsz, seq_pad, ATT_HEADS, V_DIM)
        lam = (jnp.exp(jnp.sum(lam_q1[l] * lam_k1[l]).astype(jnp.float32))
               - jnp.exp(jnp.sum(lam_q2[l] * lam_k2[l]).astype(jnp.float32)) + lam_init)
        o = diff_attention(q, k, v, lam, slopes)
        y_att = (rmsnorm(o, subln_g[l]) * (1.0 - lam_init)).reshape(bsz, seq_pad, ATT_WIDTH)
        mixed = (jax.nn.sigmoid(g_ssm) * (y_ssm @ w_ssm_out[l])
                 + jax.nn.sigmoid(g_att) * (y_att @ w_att_out[l]))
        h = h + mixed @ w_o[l]
        h = h + conv_glu_ffn(rmsnorm(h, norm2_g[l]), w_up[l], conv_w[l], conv_b[l], w_down[l])

    return h[:, N_META:N_META + seq]
```

```python
import functools
import math

import jax
import jax.numpy as jnp
from jax import lax
from jax.experimental import pallas as pl
from jax.experimental.pallas import tpu as pltpu

F32 = jnp.float32
BF16 = jnp.bfloat16

D_MODEL = 1024
N_META = 16
EPS = 1e-6
NEG_INF = -1e30

SSM_WIDTH = 512
SSM_GROUP = 16
SSM_GROUPS = 32
SSM_STATE = 64
SSM_BLOCKS = 4
SSM_BLOCK_IN = SSM_WIDTH // SSM_BLOCKS
SSM_BLOCK_STATE = SSM_GROUPS * SSM_STATE // SSM_BLOCKS

ATT_HEADS = 4
QK_DIM = 64
V_DIM = 128
ATT_WIDTH = 512
QK_COLS = 512
MIX_COLS = SSM_WIDTH + 2 * QK_COLS + ATT_WIDTH

D_FF = 2816
FF_CHUNK = 256
N_FF_CHUNKS = D_FF // FF_CHUNK

HEAD_ROWS = 128
N_JUNK = HEAD_ROWS - N_META
HALO = 16

VMEM_LIMIT = 56 * 1024 * 1024


def _rmsnorm(x, g):
    r = lax.rsqrt(jnp.mean(x * x, axis=-1, keepdims=True) + EPS)
    return (x * r) * g


def _gelu(x):
    return jax.nn.gelu(x, approximate=True)


def _ssm_prep_kernel(are_ref, aim_ref, ldt_ref, btr_ref, bti_ref,
                     lr_ref, li_ref, bbr_ref, bbi_ref):
    a_re = are_ref[...]
    a_im = aim_ref[...]
    dt = jnp.exp(ldt_ref[...])
    mag = jnp.exp(a_re * dt)
    lb_re = mag * jnp.cos(a_im * dt)
    lb_im = mag * jnp.sin(a_im * dt)
    den = a_re * a_re + a_im * a_im
    n_re = lb_re - 1.0
    f_re = (n_re * a_re + lb_im * a_im) / den
    f_im = (lb_im * a_re - n_re * a_im) / den
    lr_ref[...] = lb_re
    li_ref[...] = lb_im
    bt_re = btr_ref[...]
    bt_im = bti_ref[...]
    bbr_ref[...] = f_re[:, None, :] * bt_re - f_im[:, None, :] * bt_im
    bbi_ref[...] = f_re[:, None, :] * bt_im + f_im[:, None, :] * bt_re


def _ssm_prep(a_re, a_im, log_dt, b_re, b_im):
    g, p, c = b_re.shape
    return pl.pallas_call(
        _ssm_prep_kernel,
        out_shape=(jax.ShapeDtypeStruct((g, p), F32), jax.ShapeDtypeStruct((g, p), F32),
                   jax.ShapeDtypeStruct((g, c, p), F32), jax.ShapeDtypeStruct((g, c, p), F32)),
        name="ssm_prep",
    )(a_re, a_im, log_dt.reshape(g, 1), jnp.swapaxes(b_re, 1, 2), jnp.swapaxes(b_im, 1, 2))


def _block_diag(m):
    gpb = SSM_GROUPS // SSM_BLOCKS
    _, r, c = m.shape
    m4 = m.reshape(SSM_BLOCKS, gpb, r, c)
    eye = jnp.eye(gpb, dtype=m.dtype)
    return jnp.einsum('jgrc,gh->jgrhc', m4, eye).reshape(SSM_BLOCKS, gpb * r, gpb * c)


def _qk_norm(q, g):
    lo = lax.broadcasted_iota(jnp.int32, (1, 128), 1) < QK_DIM
    outs = []
    for c in range(QK_COLS // 128):
        blk = q[:, c * 128:(c + 1) * 128]
        sq = blk * blk
        s_lo = jnp.sum(jnp.where(lo, sq, 0.0), axis=-1, keepdims=True)
        s_hi = jnp.sum(jnp.where(lo, 0.0, sq), axis=-1, keepdims=True)
        r_lo = lax.rsqrt(s_lo / QK_DIM + EPS)
        r_hi = lax.rsqrt(s_hi / QK_DIM + EPS)
        outs.append(blk * jnp.where(lo, r_lo, r_hi))
    return jnp.concatenate(outs, axis=-1) * g


def _inproj_kernel(x_ref, g1_ref, w_ref, qg_ref, kg_ref,
                   u_ref, qt_ref, k_ref, vt_ref, *, ta):
    x = x_ref[...]
    hn = _rmsnorm(x, g1_ref[...]).astype(BF16)
    proj = jnp.dot(hn, w_ref[...], preferred_element_type=F32)
    u_ref[...] = proj[:, :SSM_WIDTH]
    q = proj[:, SSM_WIDTH:SSM_WIDTH + QK_COLS]
    k = proj[:, SSM_WIDTH + QK_COLS:SSM_WIDTH + 2 * QK_COLS]
    v = proj[:, SSM_WIDTH + 2 * QK_COLS:]
    qn = _qk_norm(q, qg_ref[...]) * (QK_DIM ** -0.5)
    kn = _qk_norm(k, kg_ref[...])
    k_ref[...] = kn.astype(BF16)
    for c in range(x.shape[0] // ta):
        qt_ref[c] = qn[c * ta:(c + 1) * ta, :].T.astype(BF16)
        vt_ref[c] = v[c * ta:(c + 1) * ta, :].T.astype(BF16)


def _inproj(x, g1, w_mix, qg, kg, *, t, ta):
    b, l, _ = x.shape
    nt = l // t
    cpt = t // ta
    kern = functools.partial(_inproj_kernel, ta=ta)
    return pl.pallas_call(
        kern,
        grid=(b, nt),
        in_specs=[
            pl.BlockSpec((None, t, D_MODEL), lambda bi, i: (bi, i, 0)),
            pl.BlockSpec((1, D_MODEL), lambda bi, i: (0, 0)),
            pl.BlockSpec((D_MODEL, MIX_COLS), lambda bi, i: (0, 0)),
            pl.BlockSpec((1, QK_COLS), lambda bi, i: (0, 0)),
            pl.BlockSpec((1, QK_COLS), lambda bi, i: (0, 0)),
        ],
        out_specs=[
            pl.BlockSpec((None, t, SSM_WIDTH), lambda bi, i: (bi, i, 0)),
            pl.BlockSpec((None, cpt, QK_COLS, ta), lambda bi, i: (bi, i, 0, 0)),
            pl.BlockSpec((None, t, QK_COLS), lambda bi, i: (bi, i, 0)),
            pl.BlockSpec((None, cpt, ATT_WIDTH, ta), lambda bi, i: (bi, i, 0, 0)),
        ],
        out_shape=(
            jax.ShapeDtypeStruct((b, l, SSM_WIDTH), F32),
            jax.ShapeDtypeStruct((b, l // ta, QK_COLS, ta), BF16),
            jax.ShapeDtypeStruct((b, l, QK_COLS), BF16),
            jax.ShapeDtypeStruct((b, l // ta, ATT_WIDTH, ta), BF16),
        ),
        compiler_params=pltpu.CompilerParams(
            dimension_semantics=("parallel", "parallel"), vmem_limit_bytes=VMEM_LIMIT),
        name="inproj",
    )(x, g1, w_mix, qg, kg)


def _ssm_kernel(u_ref, s0r_ref, s0i_ref, wb_ref, lamr_ref, lami_ref, wcr_ref, wci_ref,
                d_ref, gluw_ref, glub_ref,
                y_ref, sfr_ref, sfi_ref,
                xr_ref, xi_ref, sr_ref, si_ref, *, tc, nb):
    t = pl.program_id(0)

    @pl.when(t == 0)
    def _():
        sr_ref[...] = s0r_ref[...]
        si_ref[...] = s0i_ref[...]

    u = u_ref[...]
    ub = u.astype(BF16)
    for j in range(SSM_BLOCKS):
        x = jnp.dot(ub[:, j * SSM_BLOCK_IN:(j + 1) * SSM_BLOCK_IN], wb_ref[j],
                    preferred_element_type=F32)
        xr_ref[j] = x[:, :SSM_BLOCK_STATE]
        xi_ref[j] = x[:, SSM_BLOCK_STATE:]

    for j in range(SSM_BLOCKS):
        cols = slice(j * SSM_BLOCK_STATE, (j + 1) * SSM_BLOCK_STATE)
        lr = lamr_ref[:, cols]
        li = lami_ref[:, cols]

        def step(tt, carry, j=j, lr=lr, li=li):
            s_re, s_im = carry
            rows = pl.ds(pl.multiple_of(tt * nb, nb), nb)
            n_re = lr * s_re - li * s_im + xr_ref[j, rows, :]
            n_im = lr * s_im + li * s_re + xi_ref[j, rows, :]
            xr_ref[j, rows, :] = n_re
            xi_ref[j, rows, :] = n_im
            return n_re, n_im

        s_re, s_im = lax.fori_loop(0, tc, step, (sr_ref[:, cols], si_ref[:, cols]), unroll=8)
        sr_ref[:, cols] = s_re
        si_ref[:, cols] = s_im

    ys = []
    for j in range(SSM_BLOCKS):
        y_j = (jnp.dot(xr_ref[j].astype(BF16), wcr_ref[j], preferred_element_type=F32)
               - jnp.dot(xi_ref[j].astype(BF16), wci_ref[j], preferred_element_type=F32))
        ys.append(y_j)
    y = jnp.concatenate(ys, axis=-1) + d_ref[...] * u
    y = _gelu(y)
    z = jnp.dot(y.astype(BF16), gluw_ref[...], preferred_element_type=F32) + glub_ref[...]
    y_ref[...] = (y * jax.nn.sigmoid(z)).astype(BF16)

    @pl.when(t == pl.num_programs(0) - 1)
    def _():
        sfr_ref[...] = sr_ref[...]
        sfi_ref[...] = si_ref[...]


def _ssm(u2d, s0r, s0i, wb, lamr, lami, wcr, wci, d, gluw, glub, *, tc, nb):
    rows_total = u2d.shape[0]
    r = tc * nb
    nsteps = rows_total // r
    ns = SSM_GROUPS * SSM_STATE
    kern = functools.partial(_ssm_kernel, tc=tc, nb=nb)
    const2 = lambda t: (0, 0)
    const3 = lambda t: (0, 0, 0)
    return pl.pallas_call(
        kern,
        grid=(nsteps,),
        in_specs=[
            pl.BlockSpec((r, SSM_WIDTH), lambda t: (t, 0)),
            pl.BlockSpec((nb, ns), const2),
            pl.BlockSpec((nb, ns), const2),
            pl.BlockSpec((SSM_BLOCKS, SSM_BLOCK_IN, 2 * SSM_BLOCK_STATE), const3),
            pl.BlockSpec((nb, ns), const2),
            pl.BlockSpec((nb, ns), const2),
            pl.BlockSpec((SSM_BLOCKS, SSM_BLOCK_STATE, SSM_BLOCK_IN), const3),
            pl.BlockSpec((SSM_BLOCKS, SSM_BLOCK_STATE, SSM_BLOCK_IN), const3),
            pl.BlockSpec((1, SSM_WIDTH), const2),
            pl.BlockSpec((SSM_WIDTH, SSM_WIDTH), const2),
            pl.BlockSpec((1, SSM_WIDTH), const2),
        ],
        out_specs=[
            pl.BlockSpec((r, SSM_WIDTH), lambda t: (t, 0)),
            pl.BlockSpec((nb, ns), const2),
            pl.BlockSpec((nb, ns), const2),
        ],
        out_shape=(
            jax.ShapeDtypeStruct((rows_total, SSM_WIDTH), BF16),
            jax.ShapeDtypeStruct((nb, ns), F32),
            jax.ShapeDtypeStruct((nb, ns), F32),
        ),
        scratch_shapes=[
            pltpu.VMEM((SSM_BLOCKS, r, SSM_BLOCK_STATE), F32),
            pltpu.VMEM((SSM_BLOCKS, r, SSM_BLOCK_STATE), F32),
            pltpu.VMEM((nb, ns), F32),
            pltpu.VMEM((nb, ns), F32),
        ],
        compiler_params=pltpu.CompilerParams(
            dimension_semantics=("arbitrary",), vmem_limit_bytes=VMEM_LIMIT),
        name="ssm",
    )(u2d, s0r, s0i, wb, lamr, lami, wcr, wci, d, gluw, glub)


def _attn_kernel(slopes_ref, qt_ref, k_ref, vt_ref, *rest, ta, has_prefix, lam_init):
    if has_prefix:
        kh_ref, vht_ref = rest[:2]
        rest = rest[2:]
    (rel_ref, lq1_ref, lk1_ref, lq2_ref, lk2_ref, subg_ref,
     o_ref, q2_ref, srel_ref, m_ref, l_ref, acc_ref) = rest

    h = pl.program_id(1)
    i = pl.program_id(2)
    slope = slopes_ref[h]
    rel = rel_ref[...]

    qt = qt_ref[...]
    first = lax.broadcasted_iota(jnp.int32, qt.shape, 0) < QK_DIM
    zero = jnp.zeros_like(qt)
    q2_ref[:, :ta] = jnp.where(first, qt, zero)
    q2_ref[:, ta:] = jnp.where(first, zero, qt)
    srel_ref[...] = slope * rel
    m_ref[...] = jnp.full(m_ref.shape, NEG_INF, F32)
    l_ref[...] = jnp.zeros(l_ref.shape, F32)
    acc_ref[...] = jnp.zeros(acc_ref.shape, F32)

    def update(s, shift, vt_blk):
        m_old = m_ref[...]
        m_new = jnp.maximum(m_old, jnp.max(s, axis=0, keepdims=True) - shift)
        alpha = jnp.exp(m_old - m_new)
        p = jnp.exp(s - (m_new + shift))
        l_ref[...] = alpha * l_ref[...] + jnp.sum(p, axis=0, keepdims=True)
        acc_ref[...] = acc_ref[...] * alpha + jnp.dot(
            vt_blk, p.astype(BF16), preferred_element_type=F32)
        m_ref[...] = m_new

    if has_prefix:
        nk = kh_ref.shape[0]
        s = jnp.dot(kh_ref[...], q2_ref[...], preferred_element_type=F32)
        s = s - srel_ref[:nk, :]
        valid = lax.broadcasted_iota(jnp.int32, s.shape, 0) >= N_JUNK
        s = jnp.where(valid, s, NEG_INF)
        shift = slope * (nk + i * ta).astype(F32)
        update(s, shift, vht_ref[...])

    def full_block(j, carry):
        k_blk = k_ref[pl.ds(pl.multiple_of(j * ta, ta), ta), :]
        s = jnp.dot(k_blk, q2_ref[...], preferred_element_type=F32) - srel_ref[...]
        shift = slope * ((i - j) * ta).astype(F32)
        update(s, shift, vt_ref[j])
        return carry

    lax.fori_loop(0, i, full_block, 0)

    k_blk = k_ref[pl.ds(pl.multiple_of(i * ta, ta), ta), :]
    s = jnp.dot(k_blk, q2_ref[...], preferred_element_type=F32) - srel_ref[...]
    keep = rel >= 0.0
    if not has_prefix:
        keep = jnp.logical_and(keep, lax.broadcasted_iota(jnp.int32, s.shape, 0) >= N_JUNK)
    s = jnp.where(keep, s, NEG_INF)
    update(s, jnp.float32(0.0), vt_ref[i])

    lam = (jnp.exp(jnp.sum(lq1_ref[...] * lk1_ref[...], axis=-1, keepdims=True))
           - jnp.exp(jnp.sum(lq2_ref[...] * lk2_ref[...], axis=-1, keepdims=True))
           + lam_init)
    acc = acc_ref[...]
    l = l_ref[...]
    o = acc[:, :ta] / l[:, :ta] - lam * (acc[:, ta:] / l[:, ta:])
    r = lax.rsqrt(jnp.mean(o * o, axis=0, keepdims=True) + EPS)
    y = ((o * r) * subg_ref[...]) * (1.0 - lam_init)
    o_ref[...] = y.T.astype(BF16)


def _attention(slopes, qt, k, vt, prefix, rel, lq1, lk1, lq2, lk2, subg, *, ta, lam_init):
    b, l, _ = k.shape
    nq = l // ta
    has_prefix = prefix is not None
    kern = functools.partial(_attn_kernel, ta=ta, has_prefix=has_prefix, lam_init=lam_init)
    in_specs = [
        pl.BlockSpec(memory_space=pltpu.SMEM),
        pl.BlockSpec((None, None, V_DIM, ta), lambda bi, h, i: (bi, i, h, 0)),
        pl.BlockSpec((None, l, V_DIM), lambda bi, h, i: (bi, 0, h)),
        pl.BlockSpec((None, nq, V_DIM, ta), lambda bi, h, i: (bi, 0, h, 0)),
    ]
    args = [slopes, qt, k, vt]
    if has_prefix:
        kh, vht = prefix
        in_specs += [
            pl.BlockSpec((None, HEAD_ROWS, V_DIM), lambda bi, h, i: (0, 0, h)),
            pl.BlockSpec((None, None, V_DIM, HEAD_ROWS), lambda bi, h, i: (0, 0, h, 0)),
        ]
        args += [kh, vht]
    const2 = lambda bi, h, i: (0, 0)
    in_specs += [
        pl.BlockSpec((ta, 2 * ta), const2),
        pl.BlockSpec((1, QK_DIM), const2), pl.BlockSpec((1, QK_DIM), const2),
        pl.BlockSpec((1, QK_DIM), const2), pl.BlockSpec((1, QK_DIM), const2),
        pl.BlockSpec((V_DIM, 1), const2),
    ]
    args += [rel, lq1, lk1, lq2, lk2, subg]
    return pl.pallas_call(
        kern,
        grid=(b, ATT_HEADS, nq),
        in_specs=in_specs,
        out_specs=pl.BlockSpec((None, ta, V_DIM), lambda bi, h, i: (bi, i, h)),
        out_shape=jax.ShapeDtypeStruct((b, l, ATT_WIDTH), BF16),
        scratch_shapes=[
            pltpu.VMEM((V_DIM, 2 * ta), BF16),
            pltpu.VMEM((ta, 2 * ta), F32),
            pltpu.VMEM((1, 2 * ta), F32),
            pltpu.VMEM((1, 2 * ta), F32),
            pltpu.VMEM((V_DIM, 2 * ta), F32),
        ],
        compiler_params=pltpu.CompilerParams(
            dimension_semantics=("parallel", "parallel", "arbitrary"),
            vmem_limit_bytes=VMEM_LIMIT),
        name="diff_attention",
    )(*args)


def _merge_kernel(x_ref, ys_ref, ya_ref, g1_ref, wg_ref, wso_ref, wao_ref, wo_ref, h1_ref):
    x = x_ref[...]
    hn = _rmsnorm(x, g1_ref[...]).astype(BF16)
    gates = jnp.dot(hn, wg_ref[...], preferred_element_type=F32)
    a = jnp.dot(ys_ref[...], wso_ref[...], preferred_element_type=F32)
    c = jnp.dot(ya_ref[...], wao_ref[...], preferred_element_type=F32)
    mixed = (jax.nn.sigmoid(gates[:, :D_MODEL]) * a
             + jax.nn.sigmoid(gates[:, D_MODEL:]) * c)
    h1_ref[...] = x + jnp.dot(mixed.astype(BF16), wo_ref[...], preferred_element_type=F32)


def _merge(x, ys, ya, g1, wg, wso, wao, wo, *, t):
    b, l, _ = x.shape
    const2 = lambda bi, i: (0, 0)
    row = lambda bi, i: (bi, i, 0)
    return pl.pallas_call(
        _merge_kernel,
        grid=(b, l // t),
        in_specs=[
            pl.BlockSpec((None, t, D_MODEL), row),
            pl.BlockSpec((None, t, SSM_WIDTH), row),
            pl.BlockSpec((None, t, ATT_WIDTH), row),
            pl.BlockSpec((1, D_MODEL), const2),
            pl.BlockSpec((D_MODEL, 2 * D_MODEL), const2),
            pl.BlockSpec((SSM_WIDTH, D_MODEL), const2),
            pl.BlockSpec((ATT_WIDTH, D_MODEL), const2),
            pl.BlockSpec((D_MODEL, D_MODEL), const2),
        ],
        out_specs=pl.BlockSpec((None, t, D_MODEL), row),
        out_shape=jax.ShapeDtypeStruct((b, l, D_MODEL), F32),
        compiler_params=pltpu.CompilerParams(
            dimension_semantics=("parallel", "parallel"), vmem_limit_bytes=VMEM_LIMIT),
        name="merge",
    )(x, ys, ya, g1, wg, wso, wao, wo)


def _ffn_kernel(h_ref, hprev_ref, hhead_ref, g2_ref, wup_ref, cw_ref, cb_ref, wdn_ref,
                out_ref, hn_ref, gate_ref):
    i = pl.program_id(1)
    t = h_ref.shape[0]
    h = h_ref[...]
    g2 = g2_ref[...]
    halo = jnp.where(i == 0, hhead_ref[...], hprev_ref[...])
    hn_ref[:HALO, :] = _rmsnorm(halo, g2).astype(BF16)
    hn_ref[HALO:, :] = _rmsnorm(h, g2).astype(BF16)
    hn = hn_ref[...]
    for f in range(N_FF_CHUNKS):
        cols = slice(f * FF_CHUNK, (f + 1) * FF_CHUNK)
        up = jnp.dot(hn, wup_ref[f], preferred_element_type=F32)
        a = up[:, :FF_CHUNK]
        gate = up[HALO:, FF_CHUNK:]
        c = (a[HALO - 2:HALO - 2 + t] * cw_ref[0:1, cols]
             + a[HALO - 1:HALO - 1 + t] * cw_ref[1:2, cols]
             + a[HALO:] * cw_ref[2:3, cols] + cb_ref[:, cols])
        gate_ref[:, cols] = (_gelu(c) * gate).astype(BF16)
    out_ref[...] = h + jnp.dot(gate_ref[...], wdn_ref[...], preferred_element_type=F32)


def _ffn(h1, h1_head, g2, wup, cw, cb, wdn, *, t):
    b, l, _ = h1.shape
    const2 = lambda bi, i: (0, 0)
    const3 = lambda bi, i: (0, 0, 0)
    row = lambda bi, i: (bi, i, 0)
    per = t // HALO
    return pl.pallas_call(
        _ffn_kernel,
        grid=(b, l // t),
        in_specs=[
            pl.BlockSpec((None, t, D_MODEL), row),
            pl.BlockSpec((None, HALO, D_MODEL),
                         lambda bi, i: (bi, jnp.maximum(i * per - 1, 0), 0)),
            pl.BlockSpec((None, HALO, D_MODEL),
                         lambda bi, i: (0, HEAD_ROWS // HALO - 1, 0)),
            pl.BlockSpec((1, D_MODEL), const2),
            pl.BlockSpec((N_FF_CHUNKS, D_MODEL, 2 * FF_CHUNK), const3),
            pl.BlockSpec((3, D_FF), const2),
            pl.BlockSpec((1, D_FF), const2),
            pl.BlockSpec((D_FF, D_MODEL), const2),
        ],
        out_specs=pl.BlockSpec((None, t, D_MODEL), row),
        out_shape=jax.ShapeDtypeStruct((b, l, D_MODEL), F32),
        scratch_shapes=[
            pltpu.VMEM((HALO + t, D_MODEL), BF16),
            pltpu.VMEM((t, D_FF), BF16),
        ],
        compiler_params=pltpu.CompilerParams(
            dimension_semantics=("parallel", "arbitrary"), vmem_limit_bytes=VMEM_LIMIT),
        name="conv_ffn",
    )(h1, h1, h1_head, g2, wup, cw, cb, wdn)


def _pick_tile(l, pref):
    t = min(pref, l)
    while l % t:
        t //= 2
    return t


def _rel_matrix(ta):
    jj = jnp.arange(ta, dtype=jnp.int32)[:, None]
    ii = jnp.arange(2 * ta, dtype=jnp.int32)[None, :] % ta
    return (ii - jj).astype(F32)


def _layer(x, head, p, l_idx):
    bsz, seq, _ = x.shape
    lam_init = 0.8 - 0.6 * math.exp(-0.3 * l_idx)
    slopes = 2.0 ** (-8.0 * jnp.arange(1, ATT_HEADS + 1, dtype=F32) / ATT_HEADS)

    g1 = p['norm1_g'][l_idx][None]
    w_in = p['w_in'][l_idx]
    w_mix = w_in[:, :MIX_COLS].astype(BF16)
    w_gate = w_in[:, MIX_COLS:].astype(BF16)
    qg = jnp.tile(p['q_norm_g'][l_idx], QK_COLS // QK_DIM)[None]
    kg = jnp.tile(p['k_norm_g'][l_idx], QK_COLS // QK_DIM)[None]

    lam_re, lam_im, bbt_re, bbt_im = _ssm_prep(
        p['ssm_a_re'][l_idx], p['ssm_a_im'][l_idx], p['ssm_log_dt'][l_idx],
        p['ssm_b_re'][l_idx], p['ssm_b_im'][l_idx])
    wb = jnp.concatenate([_block_diag(bbt_re), _block_diag(bbt_im)], axis=-1).astype(BF16)
    wcr = _block_diag(jnp.swapaxes(p['ssm_c_re'][l_idx], 1, 2)).astype(BF16)
    wci = _block_diag(jnp.swapaxes(p['ssm_c_im'][l_idx], 1, 2)).astype(BF16)
    ns = SSM_GROUPS * SSM_STATE
    lamr = jnp.broadcast_to(lam_re.reshape(1, ns), (bsz, ns))
    lami = jnp.broadcast_to(lam_im.reshape(1, ns), (bsz, ns))
    d = p['ssm_d'][l_idx].reshape(1, SSM_WIDTH)
    gluw = p['ssm_glu_w'][l_idx].astype(BF16)
    glub = p['ssm_glu_b'][l_idx][None]

    lq1, lk1 = p['lam_q1'][l_idx][None], p['lam_k1'][l_idx][None]
    lq2, lk2 = p['lam_q2'][l_idx][None], p['lam_k2'][l_idx][None]
    subg = p['subln_g'][l_idx][:, None]
    wso = p['w_ssm_out'][l_idx].astype(BF16)
    wao = p['w_att_out'][l_idx].astype(BF16)
    wo = p['w_o'][l_idx].astype(BF16)

    t_row = _pick_tile(seq, 512)
    ta = _pick_tile(seq, 256)
    tc = _pick_tile(seq, 64)

    u_h, qt_h, k_h, vt_h = _inproj(head, g1, w_mix, qg, kg, t=HEAD_ROWS, ta=HEAD_ROWS)
    zeros_state = jnp.zeros((bsz, ns), F32)
    u_h2d = jnp.repeat(u_h[0], bsz, axis=0)
    ys_h2d, s0r, s0i = _ssm(u_h2d, zeros_state, zeros_state, wb, lamr, lami, wcr, wci,
                            d, gluw, glub, tc=_pick_tile(HEAD_ROWS, 64), nb=bsz)
    ys_h = ys_h2d[::bsz][None]
    ya_h = _attention(slopes, qt_h, k_h, vt_h, None, _rel_matrix(HEAD_ROWS),
                      lq1, lk1, lq2, lk2, subg, ta=HEAD_ROWS, lam_init=lam_init)
    h1_head = _merge(head, ys_h, ya_h, g1, w_gate, wso, wao, wo, t=HEAD_ROWS)

    u, qt, k, vt = _inproj(x, g1, w_mix, qg, kg, t=t_row, ta=ta)
    u2d = jnp.swapaxes(u, 0, 1).reshape(seq * bsz, SSM_WIDTH)
    ys2d, _, _ = _ssm(u2d, s0r, s0i, wb, lamr, lami, wcr, wci, d, gluw, glub, tc=tc, nb=bsz)
    ys = jnp.swapaxes(ys2d.reshape(seq, bsz, SSM_WIDTH), 0, 1)
    ya = _attention(slopes, qt, k, vt, (k_h, vt_h), _rel_matrix(ta),
                    lq1, lk1, lq2, lk2, subg, ta=ta, lam_init=lam_init)
    h1 = _merge(x, ys, ya, g1, w_gate, wso, wao, wo, t=t_row)

    w_up = p['w_up'][l_idx]
    wup = jnp.concatenate(
        [w_up[:, :D_FF].reshape(D_MODEL, N_FF_CHUNKS, FF_CHUNK),
         w_up[:, D_FF:].reshape(D_MODEL, N_FF_CHUNKS, FF_CHUNK)], axis=-1)
    wup = jnp.swapaxes(wup, 0, 1).astype(BF16)
    out = _ffn(h1, h1_head, p['norm2_g'][l_idx][None], wup, p['conv_w'][l_idx],
               p['conv_b'][l_idx][None], p['w_down'][l_idx].astype(BF16), t=t_row)
    return out, h1_head


def kernel(x, meta_tokens, norm1_g, w_in, ssm_a_re, ssm_a_im, ssm_log_dt, ssm_b_re, ssm_b_im,
           ssm_c_re, ssm_c_im, ssm_d, ssm_glu_w, ssm_glu_b, q_norm_g, k_norm_g,
           lam_q1, lam_k1, lam_q2, lam_k2, subln_g, w_ssm_out, w_att_out, w_o,
           norm2_g, w_up, conv_w, conv_b, w_down):
    params = dict(norm1_g=norm1_g, w_in=w_in, ssm_a_re=ssm_a_re, ssm_a_im=ssm_a_im,
                  ssm_log_dt=ssm_log_dt, ssm_b_re=ssm_b_re, ssm_b_im=ssm_b_im,
                  ssm_c_re=ssm_c_re, ssm_c_im=ssm_c_im, ssm_d=ssm_d, ssm_glu_w=ssm_glu_w,
                  ssm_glu_b=ssm_glu_b, q_norm_g=q_norm_g, k_norm_g=k_norm_g,
                  lam_q1=lam_q1, lam_k1=lam_k1, lam_q2=lam_q2, lam_k2=lam_k2,
                  subln_g=subln_g, w_ssm_out=w_ssm_out, w_att_out=w_att_out, w_o=w_o,
                  norm2_g=norm2_g, w_up=w_up, conv_w=conv_w, conv_b=conv_b, w_down=w_down)
    depth = norm1_g.shape[0]
    assert depth == 1, "the head tile is only carried through one layer"
    head = jnp.concatenate(
        [jnp.zeros((N_JUNK, D_MODEL), x.dtype), meta_tokens.astype(x.dtype)], axis=0)[None]
    out, _ = _layer(x, head, params, 0)
    return out
```

```python
import functools
import math

import jax
import jax.numpy as jnp
from jax import lax
from jax.experimental import pallas as pl
from jax.experimental.pallas import tpu as pltpu

F32 = jnp.float32
BF16 = jnp.bfloat16

D_MODEL = 1024
N_META = 16
EPS = 1e-6
NEG_INF = -1e30

SSM_WIDTH = 512
SSM_GROUP = 16
SSM_GROUPS = 32
SSM_STATE = 64
SSM_BLOCKS = 4
SSM_BLOCK_IN = SSM_WIDTH // SSM_BLOCKS
SSM_BLOCK_STATE = SSM_GROUPS * SSM_STATE // SSM_BLOCKS

ATT_HEADS = 4
QK_DIM = 64
V_DIM = 128
ATT_WIDTH = 512
QK_COLS = 512
MIX_COLS = SSM_WIDTH + 2 * QK_COLS + ATT_WIDTH

D_FF = 2816
FF_CHUNK = 256
N_FF_CHUNKS = D_FF // FF_CHUNK

HEAD_ROWS = 128
N_JUNK = HEAD_ROWS - N_META
HALO = 16

VMEM_LIMIT = 56 * 1024 * 1024


def _rmsnorm(x, g):
    r = lax.rsqrt(jnp.mean(x * x, axis=-1, keepdims=True) + EPS)
    return (x * r) * g


def _gelu(x):
    return jax.nn.gelu(x, approximate=True)


def _ssm_prep_kernel(are_ref, aim_ref, ldt_ref, btr_ref, bti_ref,
                     lr_ref, li_ref, bbr_ref, bbi_ref):
    a_re = are_ref[...]
    a_im = aim_ref[...]
    dt = jnp.exp(ldt_ref[...])
    mag = jnp.exp(a_re * dt)
    lb_re = mag * jnp.cos(a_im * dt)
    lb_im = mag * jnp.sin(a_im * dt)
    den = a_re * a_re + a_im * a_im
    n_re = lb_re - 1.0
    f_re = (n_re * a_re + lb_im * a_im) / den
    f_im = (lb_im * a_re - n_re * a_im) / den
    lr_ref[...] = lb_re
    li_ref[...] = lb_im
    bt_re = btr_ref[...]
    bt_im = bti_ref[...]
    bbr_ref[...] = f_re[:, None, :] * bt_re - f_im[:, None, :] * bt_im
    bbi_ref[...] = f_re[:, None, :] * bt_im + f_im[:, None, :] * bt_re


def _ssm_prep(a_re, a_im, log_dt, b_re, b_im):
    g, p, c = b_re.shape
    return pl.pallas_call(
        _ssm_prep_kernel,
        out_shape=(jax.ShapeDtypeStruct((g, p), F32), jax.ShapeDtypeStruct((g, p), F32),
                   jax.ShapeDtypeStruct((g, c, p), F32), jax.ShapeDtypeStruct((g, c, p), F32)),
        name="ssm_prep",
    )(a_re, a_im, log_dt.reshape(g, 1), jnp.swapaxes(b_re, 1, 2), jnp.swapaxes(b_im, 1, 2))


def _block_diag(m):
    gpb = SSM_GROUPS // SSM_BLOCKS
    _, r, c = m.shape
    m4 = m.reshape(SSM_BLOCKS, gpb, r, c)
    eye = jnp.eye(gpb, dtype=m.dtype)
    return jnp.einsum('jgrc,gh->jgrhc', m4, eye).reshape(SSM_BLOCKS, gpb * r, gpb * c)


def _qk_norm(q, g):
    lo = lax.broadcasted_iota(jnp.int32, (1, 128), 1) < QK_DIM
    outs = []
    for c in range(QK_COLS // 128):
        blk = q[:, c * 128:(c + 1) * 128]
        sq = blk * blk
        s_lo = jnp.sum(jnp.where(lo, sq, 0.0), axis=-1, keepdims=True)
        s_hi = jnp.sum(jnp.where(lo, 0.0, sq), axis=-1, keepdims=True)
        r_lo = lax.rsqrt(s_lo / QK_DIM + EPS)
        r_hi = lax.rsqrt(s_hi / QK_DIM + EPS)
        outs.append(blk * jnp.where(lo, r_lo, r_hi))
    return jnp.concatenate(outs, axis=-1) * g


def _inproj_kernel(x_ref, g1_ref, w_ref, qg_ref, kg_ref,
                   u_ref, qt_ref, k_ref, vt_ref, *, ta, tk):
    x = x_ref[...]
    hn = _rmsnorm(x, g1_ref[...]).astype(BF16)
    proj = jnp.dot(hn, w_ref[...], preferred_element_type=F32)
    u_ref[...] = proj[:, :SSM_WIDTH]
    q = proj[:, SSM_WIDTH:SSM_WIDTH + QK_COLS]
    k = proj[:, SSM_WIDTH + QK_COLS:SSM_WIDTH + 2 * QK_COLS]
    v = proj[:, SSM_WIDTH + 2 * QK_COLS:]
    qn = _qk_norm(q, qg_ref[...]) * (QK_DIM ** -0.5)
    kn = _qk_norm(k, kg_ref[...])
    k_ref[...] = kn.astype(BF16)
    for c in range(x.shape[0] // ta):
        qt_ref[c] = qn[c * ta:(c + 1) * ta, :].T.astype(BF16)
    for c in range(x.shape[0] // tk):
        vt_ref[c] = v[c * tk:(c + 1) * tk, :].T.astype(BF16)


def _inproj(x, g1, w_mix, qg, kg, *, t, ta, tk):
    b, l, _ = x.shape
    nt = l // t
    kern = functools.partial(_inproj_kernel, ta=ta, tk=tk)
    return pl.pallas_call(
        kern,
        grid=(b, nt),
        in_specs=[
            pl.BlockSpec((None, t, D_MODEL), lambda bi, i: (bi, i, 0)),
            pl.BlockSpec((1, D_MODEL), lambda bi, i: (0, 0)),
            pl.BlockSpec((D_MODEL, MIX_COLS), lambda bi, i: (0, 0)),
            pl.BlockSpec((1, QK_COLS), lambda bi, i: (0, 0)),
            pl.BlockSpec((1, QK_COLS), lambda bi, i: (0, 0)),
        ],
        out_specs=[
            pl.BlockSpec((None, t, SSM_WIDTH), lambda bi, i: (bi, i, 0)),
            pl.BlockSpec((None, t // ta, QK_COLS, ta), lambda bi, i: (bi, i, 0, 0)),
            pl.BlockSpec((None, t, QK_COLS), lambda bi, i: (bi, i, 0)),
            pl.BlockSpec((None, t // tk, ATT_WIDTH, tk), lambda bi, i: (bi, i, 0, 0)),
        ],
        out_shape=(
            jax.ShapeDtypeStruct((b, l, SSM_WIDTH), F32),
            jax.ShapeDtypeStruct((b, l // ta, QK_COLS, ta), BF16),
            jax.ShapeDtypeStruct((b, l, QK_COLS), BF16),
            jax.ShapeDtypeStruct((b, l // tk, ATT_WIDTH, tk), BF16),
        ),
        compiler_params=pltpu.CompilerParams(
            dimension_semantics=("parallel", "parallel"), vmem_limit_bytes=VMEM_LIMIT),
        name="inproj",
    )(x, g1, w_mix, qg, kg)


def _ssm_kernel(u_ref, s0r_ref, s0i_ref, wb_ref, lamr_ref, lami_ref, wcr_ref, wci_ref,
                d_ref, gluw_ref, glub_ref,
                y_ref, sfr_ref, sfi_ref,
                xr_ref, xi_ref, sr_ref, si_ref, *, tc, nb):
    t = pl.program_id(0)

    @pl.when(t == 0)
    def _():
        sr_ref[...] = s0r_ref[...]
        si_ref[...] = s0i_ref[...]

    u = u_ref[...]
    ub = u.astype(BF16)
    for j in range(SSM_BLOCKS):
        x = jnp.dot(ub[:, j * SSM_BLOCK_IN:(j + 1) * SSM_BLOCK_IN], wb_ref[j],
                    preferred_element_type=F32)
        xr_ref[j] = x[:, :SSM_BLOCK_STATE]
        xi_ref[j] = x[:, SSM_BLOCK_STATE:]

    for j in range(SSM_BLOCKS):
        cols = slice(j * SSM_BLOCK_STATE, (j + 1) * SSM_BLOCK_STATE)
        lr = lamr_ref[:, cols]
        li = lami_ref[:, cols]

        def step(tt, carry, j=j, lr=lr, li=li):
            s_re, s_im = carry
            rows = pl.ds(pl.multiple_of(tt * nb, nb), nb)
            n_re = lr * s_re - li * s_im + xr_ref[j, rows, :]
            n_im = lr * s_im + li * s_re + xi_ref[j, rows, :]
            xr_ref[j, rows, :] = n_re
            xi_ref[j, rows, :] = n_im
            return n_re, n_im

        s_re, s_im = lax.fori_loop(0, tc, step, (sr_ref[:, cols], si_ref[:, cols]), unroll=8)
        sr_ref[:, cols] = s_re
        si_ref[:, cols] = s_im

    ys = []
    for j in range(SSM_BLOCKS):
        y_j = (jnp.dot(xr_ref[j].astype(BF16), wcr_ref[j], preferred_element_type=F32)
               - jnp.dot(xi_ref[j].astype(BF16), wci_ref[j], preferred_element_type=F32))
        ys.append(y_j)
    y = jnp.concatenate(ys, axis=-1) + d_ref[...] * u
    y = _gelu(y)
    z = jnp.dot(y.astype(BF16), gluw_ref[...], preferred_element_type=F32) + glub_ref[...]
    y_ref[...] = (y * jax.nn.sigmoid(z)).astype(BF16)

    @pl.when(t == pl.num_programs(0) - 1)
    def _():
        sfr_ref[...] = sr_ref[...]
        sfi_ref[...] = si_ref[...]


def _ssm(u2d, s0r, s0i, wb, lamr, lami, wcr, wci, d, gluw, glub, *, tc, nb):
    rows_total = u2d.shape[0]
    r = tc * nb
    nsteps = rows_total // r
    ns = SSM_GROUPS * SSM_STATE
    kern = functools.partial(_ssm_kernel, tc=tc, nb=nb)
    const2 = lambda t: (0, 0)
    const3 = lambda t: (0, 0, 0)
    return pl.pallas_call(
        kern,
        grid=(nsteps,),
        in_specs=[
            pl.BlockSpec((r, SSM_WIDTH), lambda t: (t, 0)),
            pl.BlockSpec((nb, ns), const2),
            pl.BlockSpec((nb, ns), const2),
            pl.BlockSpec((SSM_BLOCKS, SSM_BLOCK_IN, 2 * SSM_BLOCK_STATE), const3),
            pl.BlockSpec((nb, ns), const2),
            pl.BlockSpec((nb, ns), const2),
            pl.BlockSpec((SSM_BLOCKS, SSM_BLOCK_STATE, SSM_BLOCK_IN), const3),
            pl.BlockSpec((SSM_BLOCKS, SSM_BLOCK_STATE, SSM_BLOCK_IN), const3),
            pl.BlockSpec((1, SSM_WIDTH), const2),
            pl.BlockSpec((SSM_WIDTH, SSM_WIDTH), const2),
            pl.BlockSpec((1, SSM_WIDTH), const2),
        ],
        out_specs=[
            pl.BlockSpec((r, SSM_WIDTH), lambda t: (t, 0)),
            pl.BlockSpec((nb, ns), const2),
            pl.BlockSpec((nb, ns), const2),
        ],
        out_shape=(
            jax.ShapeDtypeStruct((rows_total, SSM_WIDTH), BF16),
            jax.ShapeDtypeStruct((nb, ns), F32),
            jax.ShapeDtypeStruct((nb, ns), F32),
        ),
        scratch_shapes=[
            pltpu.VMEM((SSM_BLOCKS, r, SSM_BLOCK_STATE), F32),
            pltpu.VMEM((SSM_BLOCKS, r, SSM_BLOCK_STATE), F32),
            pltpu.VMEM((nb, ns), F32),
            pltpu.VMEM((nb, ns), F32),
        ],
        compiler_params=pltpu.CompilerParams(
            dimension_semantics=("arbitrary",), vmem_limit_bytes=VMEM_LIMIT),
        name="ssm",
    )(u2d, s0r, s0i, wb, lamr, lami, wcr, wci, d, gluw, glub)


def _attn_kernel(slopes_ref, qt_ref, k_ref, vt_ref, *rest, ta, tk, has_prefix, lam_init):
    if has_prefix:
        kh_ref, vht_ref = rest[:2]
        rest = rest[2:]
    (kb_ref, lq1_ref, lk1_ref, lq2_ref, lk2_ref, subg_ref,
     o_ref, q2_ref, sa_ref, sb_ref, pa_ref, pb_ref, acc_ref) = rest

    h = pl.program_id(1)
    i = pl.program_id(2)
    slope = slopes_ref[h]
    nsub = ta // tk

    qt = qt_ref[...]
    first = lax.broadcasted_iota(jnp.int32, qt.shape, 0) < QK_DIM
    zero = jnp.zeros_like(qt)
    q2_ref[:V_DIM, :ta] = jnp.where(first, qt, zero)
    q2_ref[:V_DIM, ta:] = jnp.where(first, zero, qt)
    bias_row = lax.broadcasted_iota(jnp.int32, (V_DIM, 2 * ta), 0) == 0
    q2_ref[V_DIM:, :] = jnp.where(bias_row, slope, 0.0).astype(BF16)

    def scores(k_blk):
        lhs = jnp.concatenate([k_blk, kb_ref[:k_blk.shape[0], :]], axis=1)
        return jnp.dot(lhs, q2_ref[...], preferred_element_type=F32)

    def key_block(kidx):
        return k_ref[pl.ds(pl.multiple_of(kidx * tk, tk), tk), :]

    def softmax_step(m_old, l_old, s, shift):
        m_new = jnp.maximum(m_old, jnp.max(s, axis=0, keepdims=True) - shift)
        alpha = jnp.exp(m_old - m_new)
        p = jnp.exp(s - (m_new + shift))
        l_new = alpha * l_old + jnp.sum(p, axis=0, keepdims=True)
        return m_new, l_new, alpha, p.astype(BF16)

    def pv(acc, alpha, vt_blk, p):
        return acc * alpha + jnp.dot(vt_blk, p, preferred_element_type=F32)

    m = jnp.full((1, 2 * ta), NEG_INF, F32)
    l = jnp.zeros((1, 2 * ta), F32)
    acc = jnp.zeros((V_DIM, 2 * ta), F32)
    if has_prefix:
        nk = kh_ref.shape[0]
        s = scores(kh_ref[...])
        valid = lax.broadcasted_iota(jnp.int32, s.shape, 0) >= N_JUNK
        s = jnp.where(valid, s, NEG_INF)
        m, l, alpha, p = softmax_step(m, l, s, slope * (nk + i * ta).astype(F32))
        acc = pv(acc, alpha, vht_ref[...], p)
    acc_ref[...] = acc

    sa_ref[...] = scores(key_block(0))
    pb_ref[...] = jnp.zeros(pb_ref.shape, BF16)

    def superblock(j, carry):
        m, l, alpha_b = carry
        n0 = 2 * j
        shift0 = slope * (i * ta - n0 * tk).astype(F32)
        sb_ref[...] = scores(key_block(n0 + 1))
        m, l, alpha_a, p = softmax_step(m, l, sa_ref[...], shift0)
        pa_ref[...] = p
        acc = pv(acc_ref[...], alpha_b, vt_ref[jnp.maximum(n0 - 1, 0)], pb_ref[...])
        sa_ref[...] = scores(key_block(n0 + 2))
        m, l, alpha_b, p = softmax_step(m, l, sb_ref[...], shift0 - slope * tk)
        pb_ref[...] = p
        acc_ref[...] = pv(acc, alpha_a, vt_ref[n0], pa_ref[...])
        return m, l, alpha_b

    alpha = jnp.ones((1, 2 * ta), F32)
    if nsub == 2:
        m, l, alpha = lax.fori_loop(0, i, superblock, (m, l, alpha))
    else:
        assert nsub == 1 and k_ref.shape[0] == ta

    q_idx = lax.broadcasted_iota(jnp.int32, (tk, 2 * ta), 1) & (ta - 1)
    k_idx = lax.broadcasted_iota(jnp.int32, (tk, 2 * ta), 0)
    acc = acc_ref[...]
    vt_pend = vt_ref[jnp.maximum(i * nsub - 1, 0)]
    p_pend = pb_ref[...]
    s = sa_ref[...]
    for c in range(nsub):
        if c + 1 < nsub:
            s_next = scores(key_block(i * nsub + c + 1))
        keep = q_idx - k_idx >= c * tk
        if not has_prefix:
            keep = jnp.logical_and(keep, k_idx >= N_JUNK - c * tk)
        m, l, alpha_c, p = softmax_step(m, l, jnp.where(keep, s, NEG_INF), slope * (-c * tk))
        acc = pv(acc, alpha, vt_pend, p_pend)
        alpha, vt_pend, p_pend = alpha_c, vt_ref[i * nsub + c], p
        if c + 1 < nsub:
            s = s_next
    acc = pv(acc, alpha, vt_pend, p_pend)

    lam = (jnp.exp(jnp.sum(lq1_ref[...] * lk1_ref[...], axis=-1, keepdims=True))
           - jnp.exp(jnp.sum(lq2_ref[...] * lk2_ref[...], axis=-1, keepdims=True))
           + lam_init)
    o = acc[:, :ta] / l[:, :ta] - lam * (acc[:, ta:] / l[:, ta:])
    r = lax.rsqrt(jnp.mean(o * o, axis=0, keepdims=True) + EPS)
    y = ((o * r) * subg_ref[...]) * (1.0 - lam_init)
    o_ref[...] = y.T.astype(BF16)


def _attention(slopes, qt, k, vt, prefix, lq1, lk1, lq2, lk2, subg, *, ta, tk, lam_init):
    b, l, _ = k.shape
    nq = l // ta
    nkv = l // tk
    assert ta & (ta - 1) == 0 and ta % tk == 0
    has_prefix = prefix is not None
    kern = functools.partial(_attn_kernel, ta=ta, tk=tk, has_prefix=has_prefix,
                             lam_init=lam_init)
    in_specs = [
        pl.BlockSpec(memory_space=pltpu.SMEM),
        pl.BlockSpec((None, None, V_DIM, ta), lambda bi, h, i: (bi, i, h, 0)),
        pl.BlockSpec((None, l, V_DIM), lambda bi, h, i: (bi, 0, h)),
        pl.BlockSpec((None, nkv, V_DIM, tk), lambda bi, h, i: (bi, 0, h, 0)),
    ]
    args = [slopes, qt, k, vt]
    if has_prefix:
        kh, vht = prefix
        assert tk >= HEAD_ROWS
        in_specs += [
            pl.BlockSpec((None, HEAD_ROWS, V_DIM), lambda bi, h, i: (0, 0, h)),
            pl.BlockSpec((None, None, V_DIM, HEAD_ROWS), lambda bi, h, i: (0, 0, h, 0)),
        ]
        args += [kh, vht]
    const2 = lambda bi, h, i: (0, 0)
    in_specs += [
        pl.BlockSpec((tk, V_DIM), const2),
        pl.BlockSpec((1, QK_DIM), const2), pl.BlockSpec((1, QK_DIM), const2),
        pl.BlockSpec((1, QK_DIM), const2), pl.BlockSpec((1, QK_DIM), const2),
        pl.BlockSpec((V_DIM, 1), const2),
    ]
    kb = jnp.zeros((tk, V_DIM), F32).at[:, 0].set(jnp.arange(tk, dtype=F32)).astype(BF16)
    args += [kb, lq1, lk1, lq2, lk2, subg]
    return pl.pallas_call(
        kern,
        grid=(b, ATT_HEADS, nq),
        in_specs=in_specs,
        out_specs=pl.BlockSpec((None, ta, V_DIM), lambda bi, h, i: (bi, i, h)),
        out_shape=jax.ShapeDtypeStruct((b, l, ATT_WIDTH), BF16),
        scratch_shapes=[
            pltpu.VMEM((2 * V_DIM, 2 * ta), BF16),
            pltpu.VMEM((tk, 2 * ta), F32),
            pltpu.VMEM((tk, 2 * ta), F32),
            pltpu.VMEM((tk, 2 * ta), BF16),
            pltpu.VMEM((tk, 2 * ta), BF16),
            pltpu.VMEM((V_DIM, 2 * ta), F32),
        ],
        compiler_params=pltpu.CompilerParams(
            dimension_semantics=("parallel", "parallel", "arbitrary"),
            vmem_limit_bytes=VMEM_LIMIT),
        name="diff_attention",
    )(*args)


def _merge_kernel(x_ref, ys_ref, ya_ref, g1_ref, wg_ref, wso_ref, wao_ref, wo_ref, h1_ref):
    x = x_ref[...]
    hn = _rmsnorm(x, g1_ref[...]).astype(BF16)
    gates = jnp.dot(hn, wg_ref[...], preferred_element_type=F32)
    a = jnp.dot(ys_ref[...], wso_ref[...], preferred_element_type=F32)
    c = jnp.dot(ya_ref[...], wao_ref[...], preferred_element_type=F32)
    mixed = (jax.nn.sigmoid(gates[:, :D_MODEL]) * a
             + jax.nn.sigmoid(gates[:, D_MODEL:]) * c)
    h1_ref[...] = x + jnp.dot(mixed.astype(BF16), wo_ref[...], preferred_element_type=F32)


def _merge(x, ys, ya, g1, wg, wso, wao, wo, *, t):
    b, l, _ = x.shape
    const2 = lambda bi, i: (0, 0)
    row = lambda bi, i: (bi, i, 0)
    return pl.pallas_call(
        _merge_kernel,
        grid=(b, l // t),
        in_specs=[
            pl.BlockSpec((None, t, D_MODEL), row),
            pl.BlockSpec((None, t, SSM_WIDTH), row),
            pl.BlockSpec((None, t, ATT_WIDTH), row),
            pl.BlockSpec((1, D_MODEL), const2),
            pl.BlockSpec((D_MODEL, 2 * D_MODEL), const2),
            pl.BlockSpec((SSM_WIDTH, D_MODEL), const2),
            pl.BlockSpec((ATT_WIDTH, D_MODEL), const2),
            pl.BlockSpec((D_MODEL, D_MODEL), const2),
        ],
        out_specs=pl.BlockSpec((None, t, D_MODEL), row),
        out_shape=jax.ShapeDtypeStruct((b, l, D_MODEL), F32),
        compiler_params=pltpu.CompilerParams(
            dimension_semantics=("parallel", "parallel"), vmem_limit_bytes=VMEM_LIMIT),
        name="merge",
    )(x, ys, ya, g1, wg, wso, wao, wo)


def _ffn_kernel(h_ref, hprev_ref, hhead_ref, g2_ref, wup_ref, cw_ref, cb_ref, wdn_ref,
                out_ref, hn_ref, gate_ref):
    i = pl.program_id(1)
    t = h_ref.shape[0]
    h = h_ref[...]
    g2 = g2_ref[...]
    halo = jnp.where(i == 0, hhead_ref[...], hprev_ref[...])
    hn_ref[:HALO, :] = _rmsnorm(halo, g2).astype(BF16)
    hn_ref[HALO:, :] = _rmsnorm(h, g2).astype(BF16)
    hn = hn_ref[...]
    for f in range(N_FF_CHUNKS):
        cols = slice(f * FF_CHUNK, (f + 1) * FF_CHUNK)
        up = jnp.dot(hn, wup_ref[f], preferred_element_type=F32)
        a = up[:, :FF_CHUNK]
        gate = up[HALO:, FF_CHUNK:]
        c = (a[HALO - 2:HALO - 2 + t] * cw_ref[0:1, cols]
             + a[HALO - 1:HALO - 1 + t] * cw_ref[1:2, cols]
             + a[HALO:] * cw_ref[2:3, cols] + cb_ref[:, cols])
        gate_ref[:, cols] = (_gelu(c) * gate).astype(BF16)
    out_ref[...] = h + jnp.dot(gate_ref[...], wdn_ref[...], preferred_element_type=F32)


def _ffn(h1, h1_head, g2, wup, cw, cb, wdn, *, t):
    b, l, _ = h1.shape
    const2 = lambda bi, i: (0, 0)
    const3 = lambda bi, i: (0, 0, 0)
    row = lambda bi, i: (bi, i, 0)
    per = t // HALO
    return pl.pallas_call(
        _ffn_kernel,
        grid=(b, l // t),
        in_specs=[
            pl.BlockSpec((None, t, D_MODEL), row),
            pl.BlockSpec((None, HALO, D_MODEL),
                         lambda bi, i: (bi, jnp.maximum(i * per - 1, 0), 0)),
            pl.BlockSpec((None, HALO, D_MODEL),
                         lambda bi, i: (0, HEAD_ROWS // HALO - 1, 0)),
            pl.BlockSpec((1, D_MODEL), const2),
            pl.BlockSpec((N_FF_CHUNKS, D_MODEL, 2 * FF_CHUNK), const3),
            pl.BlockSpec((3, D_FF), const2),
            pl.BlockSpec((1, D_FF), const2),
            pl.BlockSpec((D_FF, D_MODEL), const2),
        ],
        out_specs=pl.BlockSpec((None, t, D_MODEL), row),
        out_shape=jax.ShapeDtypeStruct((b, l, D_MODEL), F32),
        scratch_shapes=[
            pltpu.VMEM((HALO + t, D_MODEL), BF16),
            pltpu.VMEM((t, D_FF), BF16),
        ],
        compiler_params=pltpu.CompilerParams(
            dimension_semantics=("parallel", "arbitrary"), vmem_limit_bytes=VMEM_LIMIT),
        name="conv_ffn",
    )(h1, h1, h1_head, g2, wup, cw, cb, wdn)


def _pick_tile(l, pref):
    t = min(pref, l)
    while l % t:
        t //= 2
    return t


def _layer(x, head, p, l_idx):
    bsz, seq, _ = x.shape
    lam_init = 0.8 - 0.6 * math.exp(-0.3 * l_idx)
    slopes = 2.0 ** (-8.0 * jnp.arange(1, ATT_HEADS + 1, dtype=F32) / ATT_HEADS)

    g1 = p['norm1_g'][l_idx][None]
    w_in = p['w_in'][l_idx]
    w_mix = w_in[:, :MIX_COLS].astype(BF16)
    w_gate = w_in[:, MIX_COLS:].astype(BF16)
    qg = jnp.tile(p['q_norm_g'][l_idx], QK_COLS // QK_DIM)[None]
    kg = jnp.tile(p['k_norm_g'][l_idx], QK_COLS // QK_DIM)[None]

    lam_re, lam_im, bbt_re, bbt_im = _ssm_prep(
        p['ssm_a_re'][l_idx], p['ssm_a_im'][l_idx], p['ssm_log_dt'][l_idx],
        p['ssm_b_re'][l_idx], p['ssm_b_im'][l_idx])
    wb = jnp.concatenate([_block_diag(bbt_re), _block_diag(bbt_im)], axis=-1).astype(BF16)
    wcr = _block_diag(jnp.swapaxes(p['ssm_c_re'][l_idx], 1, 2)).astype(BF16)
    wci = _block_diag(jnp.swapaxes(p['ssm_c_im'][l_idx], 1, 2)).astype(BF16)
    ns = SSM_GROUPS * SSM_STATE
    lamr = jnp.broadcast_to(lam_re.reshape(1, ns), (bsz, ns))
    lami = jnp.broadcast_to(lam_im.reshape(1, ns), (bsz, ns))
    d = p['ssm_d'][l_idx].reshape(1, SSM_WIDTH)
    gluw = p['ssm_glu_w'][l_idx].astype(BF16)
    glub = p['ssm_glu_b'][l_idx][None]

    lq1, lk1 = p['lam_q1'][l_idx][None], p['lam_k1'][l_idx][None]
    lq2, lk2 = p['lam_q2'][l_idx][None], p['lam_k2'][l_idx][None]
    subg = p['subln_g'][l_idx][:, None]
    wso = p['w_ssm_out'][l_idx].astype(BF16)
    wao = p['w_att_out'][l_idx].astype(BF16)
    wo = p['w_o'][l_idx].astype(BF16)

    t_row = _pick_tile(seq, 512)
    ta = _pick_tile(seq, 512)
    tk = _pick_tile(seq, 256)
    tc = _pick_tile(seq, 64)

    u_h, qt_h, k_h, vt_h = _inproj(head, g1, w_mix, qg, kg, t=HEAD_ROWS, ta=HEAD_ROWS,
                                    tk=HEAD_ROWS)
    zeros_state = jnp.zeros((bsz, ns), F32)
    u_h2d = jnp.repeat(u_h[0], bsz, axis=0)
    ys_h2d, s0r, s0i = _ssm(u_h2d, zeros_state, zeros_state, wb, lamr, lami, wcr, wci,
                            d, gluw, glub, tc=_pick_tile(HEAD_ROWS, 64), nb=bsz)
    ys_h = ys_h2d[::bsz][None]
    ya_h = _attention(slopes, qt_h, k_h, vt_h, None, lq1, lk1, lq2, lk2, subg,
                      ta=HEAD_ROWS, tk=HEAD_ROWS, lam_init=lam_init)
    h1_head = _merge(head, ys_h, ya_h, g1, w_gate, wso, wao, wo, t=HEAD_ROWS)

    u, qt, k, vt = _inproj(x, g1, w_mix, qg, kg, t=t_row, ta=ta, tk=tk)
    u2d = jnp.swapaxes(u, 0, 1).reshape(seq * bsz, SSM_WIDTH)
    ys2d, _, _ = _ssm(u2d, s0r, s0i, wb, lamr, lami, wcr, wci, d, gluw, glub, tc=tc, nb=bsz)
    ys = jnp.swapaxes(ys2d.reshape(seq, bsz, SSM_WIDTH), 0, 1)
    ya = _attention(slopes, qt, k, vt, (k_h, vt_h), lq1, lk1, lq2, lk2, subg,
                    ta=ta, tk=tk, lam_init=lam_init)
    h1 = _merge(x, ys, ya, g1, w_gate, wso, wao, wo, t=t_row)

    w_up = p['w_up'][l_idx]
    wup = jnp.concatenate(
        [w_up[:, :D_FF].reshape(D_MODEL, N_FF_CHUNKS, FF_CHUNK),
         w_up[:, D_FF:].reshape(D_MODEL, N_FF_CHUNKS, FF_CHUNK)], axis=-1)
    wup = jnp.swapaxes(wup, 0, 1).astype(BF16)
    out = _ffn(h1, h1_head, p['norm2_g'][l_idx][None], wup, p['conv_w'][l_idx],
               p['conv_b'][l_idx][None], p['w_down'][l_idx].astype(BF16), t=t_row)
    return out, h1_head


def kernel(x, meta_tokens, norm1_g, w_in, ssm_a_re, ssm_a_im, ssm_log_dt, ssm_b_re, ssm_b_im,
           ssm_c_re, ssm_c_im, ssm_d, ssm_glu_w, ssm_glu_b, q_norm_g, k_norm_g,
           lam_q1, lam_k1, lam_q2, lam_k2, subln_g, w_ssm_out, w_att_out, w_o,
           norm2_g, w_up, conv_w, conv_b, w_down):
    params = dict(norm1_g=norm1_g, w_in=w_in, ssm_a_re=ssm_a_re, ssm_a_im=ssm_a_im,
                  ssm_log_dt=ssm_log_dt, ssm_b_re=ssm_b_re, ssm_b_im=ssm_b_im,
                  ssm_c_re=ssm_c_re, ssm_c_im=ssm_c_im, ssm_d=ssm_d, ssm_glu_w=ssm_glu_w,
                  ssm_glu_b=ssm_glu_b, q_norm_g=q_norm_g, k_norm_g=k_norm_g,
                  lam_q1=lam_q1, lam_k1=lam_k1, lam_q2=lam_q2, lam_k2=lam_k2,
                  subln_g=subln_g, w_ssm_out=w_ssm_out, w_att_out=w_att_out, w_o=w_o,
                  norm2_g=norm2_g, w_up=w_up, conv_w=conv_w, conv_b=conv_b, w_down=w_down)
    depth = norm1_g.shape[0]
    assert depth == 1, "the head tile is only carried through one layer"
    head = jnp.concatenate(
        [jnp.zeros((N_JUNK, D_MODEL), x.dtype), meta_tokens.astype(x.dtype)], axis=0)[None]
    out, _ = _layer(x, head, params, 0)
    return out
```

```python
import functools
import math

import jax
import jax.numpy as jnp
from jax import lax
from jax.experimental import pallas as pl
from jax.experimental.pallas import tpu as pltpu

F32 = jnp.float32
BF16 = jnp.bfloat16

D_MODEL = 1024
N_META = 16
EPS = 1e-6
NEG_INF = -1e30

SSM_WIDTH = 512
SSM_GROUP = 16
SSM_GROUPS = 32
SSM_STATE = 64
SSM_BLOCKS = 4
SSM_BLOCK_IN = SSM_WIDTH // SSM_BLOCKS
SSM_BLOCK_STATE = SSM_GROUPS * SSM_STATE // SSM_BLOCKS

ATT_HEADS = 4
QK_DIM = 64
V_DIM = 128
ATT_WIDTH = 512
QK_COLS = 512
MIX_COLS = SSM_WIDTH + 2 * QK_COLS + ATT_WIDTH

D_FF = 2816
FF_CHUNK = 256
N_FF_CHUNKS = D_FF // FF_CHUNK

HEAD_ROWS = 128
N_JUNK = HEAD_ROWS - N_META
SUM_ROWS = 16
LOG2E = math.log2(math.e)
DIAG_TILE = 256
COL_TILE = 512
HALO = 16

VMEM_LIMIT = 56 * 1024 * 1024


def _rmsnorm(x, g):
    r = lax.rsqrt(jnp.mean(x * x, axis=-1, keepdims=True) + EPS)
    return (x * r) * g


def _gelu(x):
    return jax.nn.gelu(x, approximate=True)


def _ssm_prep_kernel(are_ref, aim_ref, ldt_ref, btr_ref, bti_ref,
                     lr_ref, li_ref, bbr_ref, bbi_ref):
    a_re = are_ref[...]
    a_im = aim_ref[...]
    dt = jnp.exp(ldt_ref[...])
    mag = jnp.exp(a_re * dt)
    lb_re = mag * jnp.cos(a_im * dt)
    lb_im = mag * jnp.sin(a_im * dt)
    den = a_re * a_re + a_im * a_im
    n_re = lb_re - 1.0
    f_re = (n_re * a_re + lb_im * a_im) / den
    f_im = (lb_im * a_re - n_re * a_im) / den
    lr_ref[...] = lb_re
    li_ref[...] = lb_im
    bt_re = btr_ref[...]
    bt_im = bti_ref[...]
    bbr_ref[...] = f_re[:, None, :] * bt_re - f_im[:, None, :] * bt_im
    bbi_ref[...] = f_re[:, None, :] * bt_im + f_im[:, None, :] * bt_re


def _ssm_prep(a_re, a_im, log_dt, b_re, b_im):
    g, p, c = b_re.shape
    return pl.pallas_call(
        _ssm_prep_kernel,
        out_shape=(jax.ShapeDtypeStruct((g, p), F32), jax.ShapeDtypeStruct((g, p), F32),
                   jax.ShapeDtypeStruct((g, c, p), F32), jax.ShapeDtypeStruct((g, c, p), F32)),
        name="ssm_prep",
    )(a_re, a_im, log_dt.reshape(g, 1), jnp.swapaxes(b_re, 1, 2), jnp.swapaxes(b_im, 1, 2))


def _block_diag(m):
    gpb = SSM_GROUPS // SSM_BLOCKS
    _, r, c = m.shape
    m4 = m.reshape(SSM_BLOCKS, gpb, r, c)
    eye = jnp.eye(gpb, dtype=m.dtype)
    return jnp.einsum('jgrc,gh->jgrhc', m4, eye).reshape(SSM_BLOCKS, gpb * r, gpb * c)


def _qk_norm(q, g):
    lo = lax.broadcasted_iota(jnp.int32, (1, 128), 1) < QK_DIM
    outs = []
    for c in range(QK_COLS // 128):
        blk = q[:, c * 128:(c + 1) * 128]
        sq = blk * blk
        s_lo = jnp.sum(jnp.where(lo, sq, 0.0), axis=-1, keepdims=True)
        s_hi = jnp.sum(jnp.where(lo, 0.0, sq), axis=-1, keepdims=True)
        r_lo = lax.rsqrt(s_lo / QK_DIM + EPS)
        r_hi = lax.rsqrt(s_hi / QK_DIM + EPS)
        outs.append(blk * jnp.where(lo, r_lo, r_hi))
    return jnp.concatenate(outs, axis=-1) * g


def _inproj_kernel(x_ref, g1_ref, w_ref, qg_ref, kg_ref,
                   u_ref, qt_ref, k_ref, vt_ref, *, ta, tk):
    x = x_ref[...]
    hn = _rmsnorm(x, g1_ref[...]).astype(BF16)
    proj = jnp.dot(hn, w_ref[...], preferred_element_type=F32)
    u_ref[...] = proj[:, :SSM_WIDTH]
    q = proj[:, SSM_WIDTH:SSM_WIDTH + QK_COLS]
    k = proj[:, SSM_WIDTH + QK_COLS:SSM_WIDTH + 2 * QK_COLS]
    v = proj[:, SSM_WIDTH + 2 * QK_COLS:]
    qn = _qk_norm(q, qg_ref[...]) * (QK_DIM ** -0.5 * LOG2E)
    kn = _qk_norm(k, kg_ref[...])
    k_ref[...] = kn.astype(BF16)
    for c in range(x.shape[0] // ta):
        qt_ref[c] = qn[c * ta:(c + 1) * ta, :].T.astype(BF16)
    for c in range(x.shape[0] // tk):
        vt_ref[c] = v[c * tk:(c + 1) * tk, :].T.astype(BF16)


def _inproj(x, g1, w_mix, qg, kg, *, t, ta, tk):
    b, l, _ = x.shape
    nt = l // t
    kern = functools.partial(_inproj_kernel, ta=ta, tk=tk)
    return pl.pallas_call(
        kern,
        grid=(b, nt),
        in_specs=[
            pl.BlockSpec((None, t, D_MODEL), lambda bi, i: (bi, i, 0)),
            pl.BlockSpec((1, D_MODEL), lambda bi, i: (0, 0)),
            pl.BlockSpec((D_MODEL, MIX_COLS), lambda bi, i: (0, 0)),
            pl.BlockSpec((1, QK_COLS), lambda bi, i: (0, 0)),
            pl.BlockSpec((1, QK_COLS), lambda bi, i: (0, 0)),
        ],
        out_specs=[
            pl.BlockSpec((None, t, SSM_WIDTH), lambda bi, i: (bi, i, 0)),
            pl.BlockSpec((None, t // ta, QK_COLS, ta), lambda bi, i: (bi, i, 0, 0)),
            pl.BlockSpec((None, t, QK_COLS), lambda bi, i: (bi, i, 0)),
            pl.BlockSpec((None, t // tk, ATT_WIDTH, tk), lambda bi, i: (bi, i, 0, 0)),
        ],
        out_shape=(
            jax.ShapeDtypeStruct((b, l, SSM_WIDTH), F32),
            jax.ShapeDtypeStruct((b, l // ta, QK_COLS, ta), BF16),
            jax.ShapeDtypeStruct((b, l, QK_COLS), BF16),
            jax.ShapeDtypeStruct((b, l // tk, ATT_WIDTH, tk), BF16),
        ),
        compiler_params=pltpu.CompilerParams(
            dimension_semantics=("parallel", "parallel"), vmem_limit_bytes=VMEM_LIMIT),
        name="inproj",
    )(x, g1, w_mix, qg, kg)


def _ssm_kernel(u_ref, s0r_ref, s0i_ref, wb_ref, lamr_ref, lami_ref, wcr_ref, wci_ref,
                d_ref, gluw_ref, glub_ref,
                y_ref, sfr_ref, sfi_ref,
                xr_ref, xi_ref, sr_ref, si_ref, *, tc, nb):
    t = pl.program_id(0)

    @pl.when(t == 0)
    def _():
        sr_ref[...] = s0r_ref[...]
        si_ref[...] = s0i_ref[...]

    u = u_ref[...]
    ub = u.astype(BF16)
    for j in range(SSM_BLOCKS):
        x = jnp.dot(ub[:, j * SSM_BLOCK_IN:(j + 1) * SSM_BLOCK_IN], wb_ref[j],
                    preferred_element_type=F32)
        xr_ref[j] = x[:, :SSM_BLOCK_STATE]
        xi_ref[j] = x[:, SSM_BLOCK_STATE:]

    for j in range(SSM_BLOCKS):
        cols = slice(j * SSM_BLOCK_STATE, (j + 1) * SSM_BLOCK_STATE)
        lr = lamr_ref[:, cols]
        li = lami_ref[:, cols]

        def step(tt, carry, j=j, lr=lr, li=li):
            s_re, s_im = carry
            rows = pl.ds(pl.multiple_of(tt * nb, nb), nb)
            n_re = lr * s_re - li * s_im + xr_ref[j, rows, :]
            n_im = lr * s_im + li * s_re + xi_ref[j, rows, :]
            xr_ref[j, rows, :] = n_re
            xi_ref[j, rows, :] = n_im
            return n_re, n_im

        s_re, s_im = lax.fori_loop(0, tc, step, (sr_ref[:, cols], si_ref[:, cols]), unroll=8)
        sr_ref[:, cols] = s_re
        si_ref[:, cols] = s_im

    ys = []
    for j in range(SSM_BLOCKS):
        y_j = (jnp.dot(xr_ref[j].astype(BF16), wcr_ref[j], preferred_element_type=F32)
               - jnp.dot(xi_ref[j].astype(BF16), wci_ref[j], preferred_element_type=F32))
        ys.append(y_j)
    y = jnp.concatenate(ys, axis=-1) + d_ref[...] * u
    y = _gelu(y)
    z = jnp.dot(y.astype(BF16), gluw_ref[...], preferred_element_type=F32) + glub_ref[...]
    y_ref[...] = (y * jax.nn.sigmoid(z)).astype(BF16)

    @pl.when(t == pl.num_programs(0) - 1)
    def _():
        sfr_ref[...] = sr_ref[...]
        sfi_ref[...] = si_ref[...]


def _ssm(u2d, s0r, s0i, wb, lamr, lami, wcr, wci, d, gluw, glub, *, tc, nb):
    rows_total = u2d.shape[0]
    r = tc * nb
    nsteps = rows_total // r
    ns = SSM_GROUPS * SSM_STATE
    kern = functools.partial(_ssm_kernel, tc=tc, nb=nb)
    const2 = lambda t: (0, 0)
    const3 = lambda t: (0, 0, 0)
    return pl.pallas_call(
        kern,
        grid=(nsteps,),
        in_specs=[
            pl.BlockSpec((r, SSM_WIDTH), lambda t: (t, 0)),
            pl.BlockSpec((nb, ns), const2),
            pl.BlockSpec((nb, ns), const2),
            pl.BlockSpec((SSM_BLOCKS, SSM_BLOCK_IN, 2 * SSM_BLOCK_STATE), const3),
            pl.BlockSpec((nb, ns), const2),
            pl.BlockSpec((nb, ns), const2),
            pl.BlockSpec((SSM_BLOCKS, SSM_BLOCK_STATE, SSM_BLOCK_IN), const3),
            pl.BlockSpec((SSM_BLOCKS, SSM_BLOCK_STATE, SSM_BLOCK_IN), const3),
            pl.BlockSpec((1, SSM_WIDTH), const2),
            pl.BlockSpec((SSM_WIDTH, SSM_WIDTH), const2),
            pl.BlockSpec((1, SSM_WIDTH), const2),
        ],
        out_specs=[
            pl.BlockSpec((r, SSM_WIDTH), lambda t: (t, 0)),
            pl.BlockSpec((nb, ns), const2),
            pl.BlockSpec((nb, ns), const2),
        ],
        out_shape=(
            jax.ShapeDtypeStruct((rows_total, SSM_WIDTH), BF16),
            jax.ShapeDtypeStruct((nb, ns), F32),
            jax.ShapeDtypeStruct((nb, ns), F32),
        ),
        scratch_shapes=[
            pltpu.VMEM((SSM_BLOCKS, r, SSM_BLOCK_STATE), F32),
            pltpu.VMEM((SSM_BLOCKS, r, SSM_BLOCK_STATE), F32),
            pltpu.VMEM((nb, ns), F32),
            pltpu.VMEM((nb, ns), F32),
        ],
        compiler_params=pltpu.CompilerParams(
            dimension_semantics=("arbitrary",), vmem_limit_bytes=VMEM_LIMIT),
        name="ssm",
    )(u2d, s0r, s0i, wb, lamr, lami, wcr, wci, d, gluw, glub)


def _attn_kernel(slopes_ref, qt_ref, k_ref, vt_ref, *rest, ta, tk, has_prefix, lam_init):
    if has_prefix:
        kh_ref, vht_ref = rest[:2]
        rest = rest[2:]
    (kb_ref, ones_ref, lq1_ref, lk1_ref, lq2_ref, lk2_ref, subg_ref,
     o_ref, q2_ref, sa_ref, sb_ref, pa_ref, pb_ref, acc_ref) = rest

    h = pl.program_id(1)
    i = pl.program_id(2)
    slope = slopes_ref[h]
    nsub = ta // tk

    qt = qt_ref[...]
    first = lax.broadcasted_iota(jnp.int32, qt.shape, 0) < QK_DIM
    zero = jnp.zeros_like(qt)
    q2_ref[:V_DIM, :ta] = jnp.where(first, qt, zero)
    q2_ref[:V_DIM, ta:] = jnp.where(first, zero, qt)
    slope2 = jnp.full((1, 2 * ta), slope * LOG2E, F32)
    c_hi = slope2.astype(BF16).astype(F32)
    c_mid = (slope2 - c_hi).astype(BF16).astype(F32)
    c_lo = slope2 - c_hi - c_mid
    bias_row = lax.broadcasted_iota(jnp.int32, (V_DIM, 2 * ta), 0)
    q2_ref[V_DIM:, :] = jnp.where(
        bias_row == 0, c_hi,
        jnp.where(bias_row == 1, c_mid, jnp.where(bias_row == 2, c_lo, 0.0))).astype(BF16)
    slope = slope * LOG2E

    def key_lhs(k_blk):
        return jnp.concatenate([k_blk, kb_ref[:k_blk.shape[0], :]], axis=1)

    def value_lhs(vt_blk):
        return jnp.concatenate([vt_blk, ones_ref[:, :vt_blk.shape[1]]], axis=0)

    def scores(k_blk):
        return jnp.dot(key_lhs(k_blk), q2_ref[...], preferred_element_type=F32)

    def key_block(kidx):
        return k_ref[pl.ds(pl.multiple_of(kidx * tk, tk), tk), :]

    def softmax_step(m_old, s, shift):
        m_new = jnp.maximum(m_old, jnp.max(s, axis=0, keepdims=True) - shift)
        alpha = jnp.exp2(m_old - m_new)
        p = jnp.exp2(s - (m_new + shift))
        return m_new, alpha, p.astype(BF16)

    def pv(acc, alpha, vt_blk, p):
        return acc * alpha + jnp.dot(value_lhs(vt_blk), p, preferred_element_type=F32)

    m = jnp.full((1, 2 * ta), NEG_INF, F32)
    acc = jnp.zeros(acc_ref.shape, F32)
    if has_prefix:
        nk = kh_ref.shape[0]
        s = scores(kh_ref[...])
    sa_ref[...] = scores(key_block(0))
    pb_ref[...] = jnp.zeros(pb_ref.shape, BF16)
    if has_prefix:
        valid = lax.broadcasted_iota(jnp.int32, s.shape, 0) >= N_JUNK
        s = jnp.where(valid, s, NEG_INF)
        m, alpha, p = softmax_step(m, s, slope * (nk + i * ta).astype(F32))
        acc = pv(acc, alpha, vht_ref[...], p)
    acc_ref[...] = acc

    def superblock(j, carry):
        m, alpha_b = carry
        n0 = 2 * j
        shift0 = slope * (i * ta - n0 * tk).astype(F32)
        shift1 = shift0 - slope * tk
        k_a = key_lhs(key_block(n0 + 1))
        k_b = key_lhs(key_block(n0 + 2))
        v_p = value_lhs(vt_ref[jnp.maximum(n0 - 1, 0)])
        v_a = value_lhs(vt_ref[n0])
        ms, alphas = [], []
        for ct in range(2 * ta // COL_TILE):
            cs = slice(ct * COL_TILE, (ct + 1) * COL_TILE)
            sb_ref[:, cs] = jnp.dot(k_a, q2_ref[:, cs], preferred_element_type=F32)
            m_a, alpha_a, p = softmax_step(m[:, cs], sa_ref[:, cs], shift0)
            pa_ref[:, cs] = p
            acc = (acc_ref[:, cs] * alpha_b[:, cs]
                   + jnp.dot(v_p, pb_ref[:, cs], preferred_element_type=F32))
            sa_ref[:, cs] = jnp.dot(k_b, q2_ref[:, cs], preferred_element_type=F32)
            m_b, alpha_c, p = softmax_step(m_a, sb_ref[:, cs], shift1)
            pb_ref[:, cs] = p
            acc_ref[:, cs] = acc * alpha_a + jnp.dot(v_a, pa_ref[:, cs],
                                                     preferred_element_type=F32)
            ms.append(m_b)
            alphas.append(alpha_c)
        return jnp.concatenate(ms, axis=1), jnp.concatenate(alphas, axis=1)

    alpha = jnp.ones((1, 2 * ta), F32)
    if nsub == 2:
        m, alpha = lax.fori_loop(0, i, superblock, (m, alpha))
    else:
        assert nsub == 1 and k_ref.shape[0] == ta

    dw = min(DIAG_TILE, 2 * ta)
    tiles = [slice(ct * dw, (ct + 1) * dw) for ct in range(2 * ta // dw)]
    q_his = [min(ct * dw % ta + dw, ta) - 1 for ct in range(len(tiles))]
    k_idx = lax.broadcasted_iota(jnp.int32, (tk, dw), 0)
    col = lax.broadcasted_iota(jnp.int32, (tk, dw), 1)
    v_pend = value_lhs(vt_ref[jnp.maximum(i * nsub - 1, 0)])
    m_t = [m[:, cs] for cs in tiles]
    acc_t = [acc_ref[:, cs] * alpha[:, cs]
             + jnp.dot(v_pend, pb_ref[:, cs], preferred_element_type=F32) for cs in tiles]
    s_t = [sa_ref[:, cs] for cs in tiles]
    for c in range(nsub):
        live = [t for t in range(len(tiles)) if q_his[t] >= c * tk]
        if c + 1 < nsub:
            k_next = key_lhs(key_block(i * nsub + c + 1))
            s_next = {t: jnp.dot(k_next, q2_ref[:, tiles[t]], preferred_element_type=F32)
                      for t in live if q_his[t] >= (c + 1) * tk}
        p_t = {}
        for t in live:
            q_idx = (col + t * dw) & (ta - 1)
            keep = q_idx - k_idx >= c * tk
            if not has_prefix:
                keep = jnp.logical_and(keep, k_idx >= N_JUNK - c * tk)
            m_t[t], alpha_c, p_t[t] = softmax_step(
                m_t[t], jnp.where(keep, s_t[t], NEG_INF), slope * (-c * tk))
            acc_t[t] = acc_t[t] * alpha_c
        v_c = value_lhs(vt_ref[i * nsub + c])
        for t in live:
            acc_t[t] = acc_t[t] + jnp.dot(v_c, p_t[t], preferred_element_type=F32)
        if c + 1 < nsub:
            s_t = s_next
    acc = jnp.concatenate(acc_t, axis=1)

    lam = (jnp.exp(jnp.sum(lq1_ref[...] * lk1_ref[...], axis=-1, keepdims=True))
           - jnp.exp(jnp.sum(lq2_ref[...] * lk2_ref[...], axis=-1, keepdims=True))
           + lam_init)
    l = acc[V_DIM:V_DIM + 1, :]
    acc = acc[:V_DIM, :]
    o = acc[:, :ta] / l[:, :ta] - lam * (acc[:, ta:] / l[:, ta:])
    r = lax.rsqrt(jnp.mean(o * o, axis=0, keepdims=True) + EPS)
    y = ((o * r) * subg_ref[...]) * (1.0 - lam_init)
    o_ref[...] = y.T.astype(BF16)


def _attention(slopes, qt, k, vt, prefix, lq1, lk1, lq2, lk2, subg, *, ta, tk, lam_init):
    b, l, _ = k.shape
    nq = l // ta
    nkv = l // tk
    assert ta & (ta - 1) == 0 and ta % tk == 0
    has_prefix = prefix is not None
    kern = functools.partial(_attn_kernel, ta=ta, tk=tk, has_prefix=has_prefix,
                             lam_init=lam_init)
    in_specs = [
        pl.BlockSpec(memory_space=pltpu.SMEM),
        pl.BlockSpec((None, None, V_DIM, ta), lambda bi, h, i: (bi, i, h, 0)),
        pl.BlockSpec((None, l, V_DIM), lambda bi, h, i: (bi, 0, h)),
        pl.BlockSpec((None, nkv, V_DIM, tk), lambda bi, h, i: (bi, 0, h, 0)),
    ]
    args = [slopes, qt, k, vt]
    if has_prefix:
        kh, vht = prefix
        assert tk >= HEAD_ROWS
        in_specs += [
            pl.BlockSpec((None, HEAD_ROWS, V_DIM), lambda bi, h, i: (0, 0, h)),
            pl.BlockSpec((None, None, V_DIM, HEAD_ROWS), lambda bi, h, i: (0, 0, h, 0)),
        ]
        args += [kh, vht]
    const2 = lambda bi, h, i: (0, 0)
    in_specs += [
        pl.BlockSpec((tk, V_DIM), const2),
        pl.BlockSpec((SUM_ROWS, tk), const2),
        pl.BlockSpec((1, QK_DIM), const2), pl.BlockSpec((1, QK_DIM), const2),
        pl.BlockSpec((1, QK_DIM), const2), pl.BlockSpec((1, QK_DIM), const2),
        pl.BlockSpec((V_DIM, 1), const2),
    ]
    assert tk <= 256
    kb = jnp.zeros((tk, V_DIM), F32).at[:, :3].set(jnp.arange(tk, dtype=F32)[:, None])
    ones = jnp.zeros((SUM_ROWS, tk), F32).at[0].set(1.0)
    args += [kb.astype(BF16), ones.astype(BF16), lq1, lk1, lq2, lk2, subg]
    return pl.pallas_call(
        kern,
        grid=(b, ATT_HEADS, nq),
        in_specs=in_specs,
        out_specs=pl.BlockSpec((None, ta, V_DIM), lambda bi, h, i: (bi, i, h)),
        out_shape=jax.ShapeDtypeStruct((b, l, ATT_WIDTH), BF16),
        scratch_shapes=[
            pltpu.VMEM((2 * V_DIM, 2 * ta), BF16),
            pltpu.VMEM((tk, 2 * ta), F32),
            pltpu.VMEM((tk, 2 * ta), F32),
            pltpu.VMEM((tk, 2 * ta), BF16),
            pltpu.VMEM((tk, 2 * ta), BF16),
            pltpu.VMEM((V_DIM + SUM_ROWS, 2 * ta), F32),
        ],
        compiler_params=pltpu.CompilerParams(
            dimension_semantics=("parallel", "parallel", "arbitrary"),
            vmem_limit_bytes=VMEM_LIMIT),
        name="diff_attention",
    )(*args)


def _merge_kernel(x_ref, ys_ref, ya_ref, g1_ref, wg_ref, wso_ref, wao_ref, wo_ref, h1_ref):
    x = x_ref[...]
    hn = _rmsnorm(x, g1_ref[...]).astype(BF16)
    gates = jnp.dot(hn, wg_ref[...], preferred_element_type=F32)
    a = jnp.dot(ys_ref[...], wso_ref[...], preferred_element_type=F32)
    c = jnp.dot(ya_ref[...], wao_ref[...], preferred_element_type=F32)
    mixed = (jax.nn.sigmoid(gates[:, :D_MODEL]) * a
             + jax.nn.sigmoid(gates[:, D_MODEL:]) * c)
    h1_ref[...] = x + jnp.dot(mixed.astype(BF16), wo_ref[...], preferred_element_type=F32)


def _merge(x, ys, ya, g1, wg, wso, wao, wo, *, t):
    b, l, _ = x.shape
    const2 = lambda bi, i: (0, 0)
    row = lambda bi, i: (bi, i, 0)
    return pl.pallas_call(
        _merge_kernel,
        grid=(b, l // t),
        in_specs=[
            pl.BlockSpec((None, t, D_MODEL), row),
            pl.BlockSpec((None, t, SSM_WIDTH), row),
            pl.BlockSpec((None, t, ATT_WIDTH), row),
            pl.BlockSpec((1, D_MODEL), const2),
            pl.BlockSpec((D_MODEL, 2 * D_MODEL), const2),
            pl.BlockSpec((SSM_WIDTH, D_MODEL), const2),
            pl.BlockSpec((ATT_WIDTH, D_MODEL), const2),
            pl.BlockSpec((D_MODEL, D_MODEL), const2),
        ],
        out_specs=pl.BlockSpec((None, t, D_MODEL), row),
        out_shape=jax.ShapeDtypeStruct((b, l, D_MODEL), F32),
        compiler_params=pltpu.CompilerParams(
            dimension_semantics=("parallel", "parallel"), vmem_limit_bytes=VMEM_LIMIT),
        name="merge",
    )(x, ys, ya, g1, wg, wso, wao, wo)


def _ffn_kernel(h_ref, hprev_ref, hhead_ref, g2_ref, wup_ref, cw_ref, cb_ref, wdn_ref,
                out_ref, hn_ref, gate_ref):
    i = pl.program_id(1)
    t = h_ref.shape[0]
    h = h_ref[...]
    g2 = g2_ref[...]
    halo = jnp.where(i == 0, hhead_ref[...], hprev_ref[...])
    hn_ref[:HALO, :] = _rmsnorm(halo, g2).astype(BF16)
    hn_ref[HALO:, :] = _rmsnorm(h, g2).astype(BF16)
    hn = hn_ref[...]
    for f in range(N_FF_CHUNKS):
        cols = slice(f * FF_CHUNK, (f + 1) * FF_CHUNK)
        up = jnp.dot(hn, wup_ref[f], preferred_element_type=F32)
        a = up[:, :FF_CHUNK]
        gate = up[HALO:, FF_CHUNK:]
        c = (a[HALO - 2:HALO - 2 + t] * cw_ref[0:1, cols]
             + a[HALO - 1:HALO - 1 + t] * cw_ref[1:2, cols]
             + a[HALO:] * cw_ref[2:3, cols] + cb_ref[:, cols])
        gate_ref[:, cols] = (_gelu(c) * gate).astype(BF16)
    out_ref[...] = h + jnp.dot(gate_ref[...], wdn_ref[...], preferred_element_type=F32)


def _ffn(h1, h1_head, g2, wup, cw, cb, wdn, *, t):
    b, l, _ = h1.shape
    const2 = lambda bi, i: (0, 0)
    const3 = lambda bi, i: (0, 0, 0)
    row = lambda bi, i: (bi, i, 0)
    per = t // HALO
    return pl.pallas_call(
        _ffn_kernel,
        grid=(b, l // t),
        in_specs=[
            pl.BlockSpec((None, t, D_MODEL), row),
            pl.BlockSpec((None, HALO, D_MODEL),
                         lambda bi, i: (bi, jnp.maximum(i * per - 1, 0), 0)),
            pl.BlockSpec((None, HALO, D_MODEL),
                         lambda bi, i: (0, HEAD_ROWS // HALO - 1, 0)),
            pl.BlockSpec((1, D_MODEL), const2),
            pl.BlockSpec((N_FF_CHUNKS, D_MODEL, 2 * FF_CHUNK), const3),
            pl.BlockSpec((3, D_FF), const2),
            pl.BlockSpec((1, D_FF), const2),
            pl.BlockSpec((D_FF, D_MODEL), const2),
        ],
        out_specs=pl.BlockSpec((None, t, D_MODEL), row),
        out_shape=jax.ShapeDtypeStruct((b, l, D_MODEL), F32),
        scratch_shapes=[
            pltpu.VMEM((HALO + t, D_MODEL), BF16),
            pltpu.VMEM((t, D_FF), BF16),
        ],
        compiler_params=pltpu.CompilerParams(
            dimension_semantics=("parallel", "arbitrary"), vmem_limit_bytes=VMEM_LIMIT),
        name="conv_ffn",
    )(h1, h1, h1_head, g2, wup, cw, cb, wdn)


def _pick_tile(l, pref):
    t = min(pref, l)
    while l % t:
        t //= 2
    return t


def _layer(x, head, p, l_idx):
    bsz, seq, _ = x.shape
    lam_init = 0.8 - 0.6 * math.exp(-0.3 * l_idx)
    slopes = 2.0 ** (-8.0 * jnp.arange(1, ATT_HEADS + 1, dtype=F32) / ATT_HEADS)

    g1 = p['norm1_g'][l_idx][None]
    w_in = p['w_in'][l_idx]
    w_mix = w_in[:, :MIX_COLS].astype(BF16)
    w_gate = w_in[:, MIX_COLS:].astype(BF16)
    qg = jnp.tile(p['q_norm_g'][l_idx], QK_COLS // QK_DIM)[None]
    kg = jnp.tile(p['k_norm_g'][l_idx], QK_COLS // QK_DIM)[None]

    lam_re, lam_im, bbt_re, bbt_im = _ssm_prep(
        p['ssm_a_re'][l_idx], p['ssm_a_im'][l_idx], p['ssm_log_dt'][l_idx],
        p['ssm_b_re'][l_idx], p['ssm_b_im'][l_idx])
    wb = jnp.concatenate([_block_diag(bbt_re), _block_diag(bbt_im)], axis=-1).astype(BF16)
    wcr = _block_diag(jnp.swapaxes(p['ssm_c_re'][l_idx], 1, 2)).astype(BF16)
    wci = _block_diag(jnp.swapaxes(p['ssm_c_im'][l_idx], 1, 2)).astype(BF16)
    ns = SSM_GROUPS * SSM_STATE
    lamr = jnp.broadcast_to(lam_re.reshape(1, ns), (bsz, ns))
    lami = jnp.broadcast_to(lam_im.reshape(1, ns), (bsz, ns))
    d = p['ssm_d'][l_idx].reshape(1, SSM_WIDTH)
    gluw = p['ssm_glu_w'][l_idx].astype(BF16)
    glub = p['ssm_glu_b'][l_idx][None]

    lq1, lk1 = p['lam_q1'][l_idx][None], p['lam_k1'][l_idx][None]
    lq2, lk2 = p['lam_q2'][l_idx][None], p['lam_k2'][l_idx][None]
    subg = p['subln_g'][l_idx][:, None]
    wso = p['w_ssm_out'][l_idx].astype(BF16)
    wao = p['w_att_out'][l_idx].astype(BF16)
    wo = p['w_o'][l_idx].astype(BF16)

    t_row = _pick_tile(seq, 512)
    ta = _pick_tile(seq, 512)
    tk = _pick_tile(seq, 256)
    tc = _pick_tile(seq, 64)

    u_h, qt_h, k_h, vt_h = _inproj(head, g1, w_mix, qg, kg, t=HEAD_ROWS, ta=HEAD_ROWS,
                                    tk=HEAD_ROWS)
    zeros_state = jnp.zeros((bsz, ns), F32)
    u_h2d = jnp.repeat(u_h[0], bsz, axis=0)
    ys_h2d, s0r, s0i = _ssm(u_h2d, zeros_state, zeros_state, wb, lamr, lami, wcr, wci,
                            d, gluw, glub, tc=_pick_tile(HEAD_ROWS, 64), nb=bsz)
    ys_h = ys_h2d[::bsz][None]
    ya_h = _attention(slopes, qt_h, k_h, vt_h, None, lq1, lk1, lq2, lk2, subg,
                      ta=HEAD_ROWS, tk=HEAD_ROWS, lam_init=lam_init)
    h1_head = _merge(head, ys_h, ya_h, g1, w_gate, wso, wao, wo, t=HEAD_ROWS)

    u, qt, k, vt = _inproj(x, g1, w_mix, qg, kg, t=t_row, ta=ta, tk=tk)
    u2d = jnp.swapaxes(u, 0, 1).reshape(seq * bsz, SSM_WIDTH)
    ys2d, _, _ = _ssm(u2d, s0r, s0i, wb, lamr, lami, wcr, wci, d, gluw, glub, tc=tc, nb=bsz)
    ys = jnp.swapaxes(ys2d.reshape(seq, bsz, SSM_WIDTH), 0, 1)
    ya = _attention(slopes, qt, k, vt, (k_h, vt_h), lq1, lk1, lq2, lk2, subg,
                    ta=ta, tk=tk, lam_init=lam_init)
    h1 = _merge(x, ys, ya, g1, w_gate, wso, wao, wo, t=t_row)

    w_up = p['w_up'][l_idx]
    wup = jnp.concatenate(
        [w_up[:, :D_FF].reshape(D_MODEL, N_FF_CHUNKS, FF_CHUNK),
         w_up[:, D_FF:].reshape(D_MODEL, N_FF_CHUNKS, FF_CHUNK)], axis=-1)
    wup = jnp.swapaxes(wup, 0, 1).astype(BF16)
    out = _ffn(h1, h1_head, p['norm2_g'][l_idx][None], wup, p['conv_w'][l_idx],
               p['conv_b'][l_idx][None], p['w_down'][l_idx].astype(BF16), t=t_row)
    return out, h1_head


def kernel(x, meta_tokens, norm1_g, w_in, ssm_a_re, ssm_a_im, ssm_log_dt, ssm_b_re, ssm_b_im,
           ssm_c_re, ssm_c_im, ssm_d, ssm_glu_w, ssm_glu_b, q_norm_g, k_norm_g,
           lam_q1, lam_k1, lam_q2, lam_k2, subln_g, w_ssm_out, w_att_out, w_o,
           norm2_g, w_up, conv_w, conv_b, w_down):
    params = dict(norm1_g=norm1_g, w_in=w_in, ssm_a_re=ssm_a_re, ssm_a_im=ssm_a_im,
                  ssm_log_dt=ssm_log_dt, ssm_b_re=ssm_b_re, ssm_b_im=ssm_b_im,
                  ssm_c_re=ssm_c_re, ssm_c_im=ssm_c_im, ssm_d=ssm_d, ssm_glu_w=ssm_glu_w,
                  ssm_glu_b=ssm_glu_b, q_norm_g=q_norm_g, k_norm_g=k_norm_g,
                  lam_q1=lam_q1, lam_k1=lam_k1, lam_q2=lam_q2, lam_k2=lam_k2,
                  subln_g=subln_g, w_ssm_out=w_ssm_out, w_att_out=w_att_out, w_o=w_o,
                  norm2_g=norm2_g, w_up=w_up, conv_w=conv_w, conv_b=conv_b, w_down=w_down)
    depth = norm1_g.shape[0]
    assert depth == 1, "the head tile is only carried through one layer"
    head = jnp.concatenate(
        [jnp.zeros((N_JUNK, D_MODEL), x.dtype), meta_tokens.astype(x.dtype)], axis=0)[None]
    out, _ = _layer(x, head, params, 0)
    return out
```

```python
import functools
import math

import jax
import jax.numpy as jnp
from jax import lax
from jax.experimental import pallas as pl
from jax.experimental.pallas import tpu as pltpu

F32 = jnp.float32
BF16 = jnp.bfloat16

D_MODEL = 1024
N_META = 16
EPS = 1e-6
NEG_INF = -1e30

SSM_WIDTH = 512
SSM_GROUP = 16
SSM_GROUPS = 32
SSM_STATE = 64
SSM_BLOCKS = 4
SSM_BLOCK_IN = SSM_WIDTH // SSM_BLOCKS
SSM_BLOCK_STATE = SSM_GROUPS * SSM_STATE // SSM_BLOCKS

ATT_HEADS = 4
QK_DIM = 64
V_DIM = 128
ATT_WIDTH = 512
QK_COLS = 512
MIX_COLS = SSM_WIDTH + 2 * QK_COLS + ATT_WIDTH

D_FF = 2816
FF_CHUNK = 256
N_FF_CHUNKS = D_FF // FF_CHUNK

HEAD_ROWS = 128
N_JUNK = HEAD_ROWS - N_META
SUM_ROWS = 16
LOG2E = math.log2(math.e)
DIAG_TILE = 256
COL_TILE = 512
HALO = 16

VMEM_LIMIT = 56 * 1024 * 1024


def _rmsnorm(x, g):
    r = lax.rsqrt(jnp.mean(x * x, axis=-1, keepdims=True) + EPS)
    return (x * r) * g


def _gelu(x):
    return jax.nn.gelu(x, approximate=True)


def _ssm_prep_kernel(are_ref, aim_ref, ldt_ref, btr_ref, bti_ref,
                     lr_ref, li_ref, bbr_ref, bbi_ref):
    a_re = are_ref[...]
    a_im = aim_ref[...]
    dt = jnp.exp(ldt_ref[...])
    mag = jnp.exp(a_re * dt)
    lb_re = mag * jnp.cos(a_im * dt)
    lb_im = mag * jnp.sin(a_im * dt)
    den = a_re * a_re + a_im * a_im
    n_re = lb_re - 1.0
    f_re = (n_re * a_re + lb_im * a_im) / den
    f_im = (lb_im * a_re - n_re * a_im) / den
    lr_ref[...] = lb_re
    li_ref[...] = lb_im
    bt_re = btr_ref[...]
    bt_im = bti_ref[...]
    bbr_ref[...] = f_re[:, None, :] * bt_re - f_im[:, None, :] * bt_im
    bbi_ref[...] = f_re[:, None, :] * bt_im + f_im[:, None, :] * bt_re


def _ssm_prep(a_re, a_im, log_dt, b_re, b_im):
    g, p, c = b_re.shape
    return pl.pallas_call(
        _ssm_prep_kernel,
        out_shape=(jax.ShapeDtypeStruct((g, p), F32), jax.ShapeDtypeStruct((g, p), F32),
                   jax.ShapeDtypeStruct((g, c, p), F32), jax.ShapeDtypeStruct((g, c, p), F32)),
        name="ssm_prep",
    )(a_re, a_im, log_dt.reshape(g, 1), jnp.swapaxes(b_re, 1, 2), jnp.swapaxes(b_im, 1, 2))


def _block_diag(m):
    gpb = SSM_GROUPS // SSM_BLOCKS
    _, r, c = m.shape
    m4 = m.reshape(SSM_BLOCKS, gpb, r, c)
    eye = jnp.eye(gpb, dtype=m.dtype)
    return jnp.einsum('jgrc,gh->jgrhc', m4, eye).reshape(SSM_BLOCKS, gpb * r, gpb * c)


def _qk_norm(q, g):
    lo = lax.broadcasted_iota(jnp.int32, (1, 128), 1) < QK_DIM
    outs = []
    for c in range(QK_COLS // 128):
        blk = q[:, c * 128:(c + 1) * 128]
        sq = blk * blk
        s_lo = jnp.sum(jnp.where(lo, sq, 0.0), axis=-1, keepdims=True)
        s_hi = jnp.sum(jnp.where(lo, 0.0, sq), axis=-1, keepdims=True)
        r_lo = lax.rsqrt(s_lo / QK_DIM + EPS)
        r_hi = lax.rsqrt(s_hi / QK_DIM + EPS)
        outs.append(blk * jnp.where(lo, r_lo, r_hi))
    return jnp.concatenate(outs, axis=-1) * g


def _inproj_kernel(x_ref, g1_ref, w_ref, qg_ref, kg_ref,
                   u_ref, qt_ref, k_ref, vt_ref, *, ta, tk):
    x = x_ref[...]
    hn = _rmsnorm(x, g1_ref[...]).astype(BF16)

    def proj(c0, width):
        return jnp.dot(hn, w_ref[:, c0:c0 + width], preferred_element_type=F32)

    q = proj(SSM_WIDTH, QK_COLS)
    k = proj(SSM_WIDTH + QK_COLS, QK_COLS)
    qn = _qk_norm(q, qg_ref[...]) * (QK_DIM ** -0.5 * LOG2E)
    for c in range(x.shape[0] // ta):
        qt_ref[c] = qn[c * ta:(c + 1) * ta, :].T.astype(BF16)
    v = proj(SSM_WIDTH + 2 * QK_COLS, ATT_WIDTH)
    k_ref[...] = _qk_norm(k, kg_ref[...]).astype(BF16)
    u_ref[...] = proj(0, SSM_WIDTH)
    for c in range(x.shape[0] // tk):
        vt_ref[c] = v[c * tk:(c + 1) * tk, :].T.astype(BF16)


def _inproj(x, g1, w_mix, qg, kg, *, t, ta, tk):
    b, l, _ = x.shape
    nt = l // t
    kern = functools.partial(_inproj_kernel, ta=ta, tk=tk)
    return pl.pallas_call(
        kern,
        grid=(b, nt),
        in_specs=[
            pl.BlockSpec((None, t, D_MODEL), lambda bi, i: (bi, i, 0)),
            pl.BlockSpec((1, D_MODEL), lambda bi, i: (0, 0)),
            pl.BlockSpec((D_MODEL, MIX_COLS), lambda bi, i: (0, 0)),
            pl.BlockSpec((1, QK_COLS), lambda bi, i: (0, 0)),
            pl.BlockSpec((1, QK_COLS), lambda bi, i: (0, 0)),
        ],
        out_specs=[
            pl.BlockSpec((None, t, SSM_WIDTH), lambda bi, i: (bi, i, 0)),
            pl.BlockSpec((None, t // ta, QK_COLS, ta), lambda bi, i: (bi, i, 0, 0)),
            pl.BlockSpec((None, t, QK_COLS), lambda bi, i: (bi, i, 0)),
            pl.BlockSpec((None, t // tk, ATT_WIDTH, tk), lambda bi, i: (bi, i, 0, 0)),
        ],
        out_shape=(
            jax.ShapeDtypeStruct((b, l, SSM_WIDTH), F32),
            jax.ShapeDtypeStruct((b, l // ta, QK_COLS, ta), BF16),
            jax.ShapeDtypeStruct((b, l, QK_COLS), BF16),
            jax.ShapeDtypeStruct((b, l // tk, ATT_WIDTH, tk), BF16),
        ),
        compiler_params=pltpu.CompilerParams(
            dimension_semantics=("parallel", "parallel"), vmem_limit_bytes=VMEM_LIMIT),
        name="inproj",
    )(x, g1, w_mix, qg, kg)


def _ssm_kernel(u_ref, s0r_ref, s0i_ref, wb_ref, lamr_ref, lami_ref, wcr_ref, wci_ref,
                d_ref, gluw_ref, glub_ref,
                y_ref, sfr_ref, sfi_ref,
                xr_ref, xi_ref, sr_ref, si_ref, *, tc, nb):
    t = pl.program_id(0)

    @pl.when(t == 0)
    def _():
        sr_ref[...] = s0r_ref[...]
        si_ref[...] = s0i_ref[...]

    u = jnp.swapaxes(u_ref[...], 0, 1).reshape(tc * nb, SSM_WIDTH)
    ub = u.astype(BF16)

    def project_in(j):
        x = jnp.dot(ub[:, j * SSM_BLOCK_IN:(j + 1) * SSM_BLOCK_IN], wb_ref[j],
                    preferred_element_type=F32)
        xr_ref[j] = x[:, :SSM_BLOCK_STATE]
        xi_ref[j] = x[:, SSM_BLOCK_STATE:]

    def scan(j):
        cols = slice(j * SSM_BLOCK_STATE, (j + 1) * SSM_BLOCK_STATE)
        lr = lamr_ref[:, cols]
        li = lami_ref[:, cols]
        s_re = sr_ref[:, cols]
        s_im = si_ref[:, cols]
        for tt in range(tc):
            rows = slice(tt * nb, (tt + 1) * nb)
            s_re, s_im = (lr * s_re - li * s_im + xr_ref[j, rows, :],
                          lr * s_im + li * s_re + xi_ref[j, rows, :])
            xr_ref[j, rows, :] = s_re
            xi_ref[j, rows, :] = s_im
        sr_ref[:, cols] = s_re
        si_ref[:, cols] = s_im

    def project_out(j):
        return (jnp.dot(xr_ref[j].astype(BF16), wcr_ref[j], preferred_element_type=F32)
                - jnp.dot(xi_ref[j].astype(BF16), wci_ref[j], preferred_element_type=F32))

    ys = []
    project_in(0)
    for j in range(SSM_BLOCKS):
        if j + 1 < SSM_BLOCKS:
            project_in(j + 1)
        scan(j)
        ys.append(project_out(j))
    y = jnp.concatenate(ys, axis=-1) + d_ref[...] * u
    y = _gelu(y)
    z = jnp.dot(y.astype(BF16), gluw_ref[...], preferred_element_type=F32) + glub_ref[...]
    out = (y * jax.nn.sigmoid(z)).reshape(tc, nb, SSM_WIDTH)
    y_ref[...] = jnp.swapaxes(out, 0, 1).astype(BF16)

    @pl.when(t == pl.num_programs(0) - 1)
    def _():
        sfr_ref[...] = sr_ref[...]
        sfi_ref[...] = si_ref[...]


def _ssm(u, s0r, s0i, wb, lamr, lami, wcr, wci, d, gluw, glub, *, tc):
    nb, seq, _ = u.shape
    r = tc * nb
    nsteps = seq // tc
    ns = SSM_GROUPS * SSM_STATE
    kern = functools.partial(_ssm_kernel, tc=tc, nb=nb)
    const2 = lambda t: (0, 0)
    const3 = lambda t: (0, 0, 0)
    return pl.pallas_call(
        kern,
        grid=(nsteps,),
        in_specs=[
            pl.BlockSpec((nb, tc, SSM_WIDTH), lambda t: (0, t, 0)),
            pl.BlockSpec((nb, ns), const2),
            pl.BlockSpec((nb, ns), const2),
            pl.BlockSpec((SSM_BLOCKS, SSM_BLOCK_IN, 2 * SSM_BLOCK_STATE), const3),
            pl.BlockSpec((nb, ns), const2),
            pl.BlockSpec((nb, ns), const2),
            pl.BlockSpec((SSM_BLOCKS, SSM_BLOCK_STATE, SSM_BLOCK_IN), const3),
            pl.BlockSpec((SSM_BLOCKS, SSM_BLOCK_STATE, SSM_BLOCK_IN), const3),
            pl.BlockSpec((1, SSM_WIDTH), const2),
            pl.BlockSpec((SSM_WIDTH, SSM_WIDTH), const2),
            pl.BlockSpec((1, SSM_WIDTH), const2),
        ],
        out_specs=[
            pl.BlockSpec((nb, tc, SSM_WIDTH), lambda t: (0, t, 0)),
            pl.BlockSpec((nb, ns), const2),
            pl.BlockSpec((nb, ns), const2),
        ],
        out_shape=(
            jax.ShapeDtypeStruct((nb, seq, SSM_WIDTH), BF16),
            jax.ShapeDtypeStruct((nb, ns), F32),
            jax.ShapeDtypeStruct((nb, ns), F32),
        ),
        scratch_shapes=[
            pltpu.VMEM((SSM_BLOCKS, r, SSM_BLOCK_STATE), F32),
            pltpu.VMEM((SSM_BLOCKS, r, SSM_BLOCK_STATE), F32),
            pltpu.VMEM((nb, ns), F32),
            pltpu.VMEM((nb, ns), F32),
        ],
        compiler_params=pltpu.CompilerParams(
            dimension_semantics=("arbitrary",), vmem_limit_bytes=VMEM_LIMIT),
        name="ssm",
    )(u, s0r, s0i, wb, lamr, lami, wcr, wci, d, gluw, glub)


def _attn_kernel(slopes_ref, qt_ref, k_ref, vt_ref, *rest, ta, tk, has_prefix, lam_init):
    if has_prefix:
        kh_ref, vht_ref = rest[:2]
        rest = rest[2:]
    (kb_ref, ones_ref, lq1_ref, lk1_ref, lq2_ref, lk2_ref, subg_ref,
     o_ref, q2_ref, sa_ref, sb_ref, pa_ref, pb_ref, acc_ref) = rest

    h = pl.program_id(1)
    i = pl.program_id(2)
    slope = slopes_ref[h]
    nsub = ta // tk

    qt = qt_ref[...]
    first = lax.broadcasted_iota(jnp.int32, qt.shape, 0) < QK_DIM
    zero = jnp.zeros_like(qt)
    q2_ref[:V_DIM, :ta] = jnp.where(first, qt, zero)
    q2_ref[:V_DIM, ta:] = jnp.where(first, zero, qt)
    slope2 = jnp.full((1, 2 * ta), slope * LOG2E, F32)
    c_hi = slope2.astype(BF16).astype(F32)
    c_mid = (slope2 - c_hi).astype(BF16).astype(F32)
    c_lo = slope2 - c_hi - c_mid
    bias_row = lax.broadcasted_iota(jnp.int32, (V_DIM, 2 * ta), 0)
    q2_ref[V_DIM:, :] = jnp.where(
        bias_row == 0, c_hi,
        jnp.where(bias_row == 1, c_mid, jnp.where(bias_row == 2, c_lo, 0.0))).astype(BF16)
    slope = slope * LOG2E

    def key_lhs(k_blk):
        return jnp.concatenate([k_blk, kb_ref[:k_blk.shape[0], :]], axis=1)

    def value_lhs(vt_blk):
        return jnp.concatenate([vt_blk, ones_ref[:, :vt_blk.shape[1]]], axis=0)

    def scores(k_blk):
        return jnp.dot(key_lhs(k_blk), q2_ref[...], preferred_element_type=F32)

    def key_block(kidx):
        return k_ref[pl.ds(pl.multiple_of(kidx * tk, tk), tk), :]

    def softmax_step(m_old, s, shift):
        m_new = jnp.maximum(m_old, jnp.max(s, axis=0, keepdims=True) - shift)
        alpha = jnp.exp2(m_old - m_new)
        p = jnp.exp2(s - (m_new + shift))
        return m_new, alpha, p.astype(BF16)

    def pv(acc, alpha, vt_blk, p):
        return acc * alpha + jnp.dot(value_lhs(vt_blk), p, preferred_element_type=F32)

    m = jnp.full((1, 2 * ta), NEG_INF, F32)
    acc = jnp.zeros(acc_ref.shape, F32)
    if has_prefix:
        nk = kh_ref.shape[0]
        s = scores(kh_ref[...])
    sa_ref[...] = scores(key_block(0))
    pb_ref[...] = jnp.zeros(pb_ref.shape, BF16)
    if has_prefix:
        valid = lax.broadcasted_iota(jnp.int32, s.shape, 0) >= N_JUNK
        s = jnp.where(valid, s, NEG_INF)
        m, alpha, p = softmax_step(m, s, slope * (nk + i * ta).astype(F32))
        acc = pv(acc, alpha, vht_ref[...], p)
    acc_ref[...] = acc

    def superblock(j, carry):
        m, alpha_b = carry
        n0 = 2 * j
        shift0 = slope * (i * ta - n0 * tk).astype(F32)
        shift1 = shift0 - slope * tk
        k_a = key_lhs(key_block(n0 + 1))
        k_b = key_lhs(key_block(n0 + 2))
        v_p = value_lhs(vt_ref[jnp.maximum(n0 - 1, 0)])
        v_a = value_lhs(vt_ref[n0])
        ms, alphas = [], []
        for ct in range(2 * ta // COL_TILE):
            cs = slice(ct * COL_TILE, (ct + 1) * COL_TILE)
            sb_ref[:, cs] = jnp.dot(k_a, q2_ref[:, cs], preferred_element_type=F32)
            m_a, alpha_a, p = softmax_step(m[:, cs], sa_ref[:, cs], shift0)
            pa_ref[:, cs] = p
            acc = (acc_ref[:, cs] * alpha_b[:, cs]
                   + jnp.dot(v_p, pb_ref[:, cs], preferred_element_type=F32))
            sa_ref[:, cs] = jnp.dot(k_b, q2_ref[:, cs], preferred_element_type=F32)
            m_b, alpha_c, p = softmax_step(m_a, sb_ref[:, cs], shift1)
            pb_ref[:, cs] = p
            acc_ref[:, cs] = acc * alpha_a + jnp.dot(v_a, pa_ref[:, cs],
                                                     preferred_element_type=F32)
            ms.append(m_b)
            alphas.append(alpha_c)
        return jnp.concatenate(ms, axis=1), jnp.concatenate(alphas, axis=1)

    alpha = jnp.ones((1, 2 * ta), F32)
    if nsub == 2:
        m, alpha = lax.fori_loop(0, i, superblock, (m, alpha))
    else:
        assert nsub == 1 and k_ref.shape[0] == ta

    dw = min(DIAG_TILE, 2 * ta)
    tiles = [slice(ct * dw, (ct + 1) * dw) for ct in range(2 * ta // dw)]
    q_his = [min(ct * dw % ta + dw, ta) - 1 for ct in range(len(tiles))]
    k_idx = lax.broadcasted_iota(jnp.int32, (tk, dw), 0)
    col = lax.broadcasted_iota(jnp.int32, (tk, dw), 1)
    v_pend = value_lhs(vt_ref[jnp.maximum(i * nsub - 1, 0)])
    m_t = [m[:, cs] for cs in tiles]
    acc_t = [acc_ref[:, cs] * alpha[:, cs]
             + jnp.dot(v_pend, pb_ref[:, cs], preferred_element_type=F32) for cs in tiles]
    s_t = [sa_ref[:, cs] for cs in tiles]
    for c in range(nsub):
        live = [t for t in range(len(tiles)) if q_his[t] >= c * tk]
        if c + 1 < nsub:
            k_next = key_lhs(key_block(i * nsub + c + 1))
            s_next = {t: jnp.dot(k_next, q2_ref[:, tiles[t]], preferred_element_type=F32)
                      for t in live if q_his[t] >= (c + 1) * tk}
        p_t = {}
        for t in live:
            q_idx = (col + t * dw) & (ta - 1)
            keep = q_idx - k_idx >= c * tk
            if not has_prefix:
                keep = jnp.logical_and(keep, k_idx >= N_JUNK - c * tk)
            m_t[t], alpha_c, p_t[t] = softmax_step(
                m_t[t], jnp.where(keep, s_t[t], NEG_INF), slope * (-c * tk))
            acc_t[t] = acc_t[t] * alpha_c
        v_c = value_lhs(vt_ref[i * nsub + c])
        for t in live:
            acc_t[t] = acc_t[t] + jnp.dot(v_c, p_t[t], preferred_element_type=F32)
        if c + 1 < nsub:
            s_t = s_next
    acc = jnp.concatenate(acc_t, axis=1)

    lam = (jnp.exp(jnp.sum(lq1_ref[...] * lk1_ref[...], axis=-1, keepdims=True))
           - jnp.exp(jnp.sum(lq2_ref[...] * lk2_ref[...], axis=-1, keepdims=True))
           + lam_init)
    l = acc[V_DIM:V_DIM + 1, :]
    acc = acc[:V_DIM, :]
    o = acc[:, :ta] / l[:, :ta] - lam * (acc[:, ta:] / l[:, ta:])
    r = lax.rsqrt(jnp.mean(o * o, axis=0, keepdims=True) + EPS)
    y = ((o * r) * subg_ref[...]) * (1.0 - lam_init)
    o_ref[...] = y.T.astype(BF16)


def _attention(slopes, qt, k, vt, prefix, lq1, lk1, lq2, lk2, subg, *, ta, tk, lam_init):
    b, l, _ = k.shape
    nq = l // ta
    nkv = l // tk
    assert ta & (ta - 1) == 0 and ta % tk == 0
    has_prefix = prefix is not None
    kern = functools.partial(_attn_kernel, ta=ta, tk=tk, has_prefix=has_prefix,
                             lam_init=lam_init)
    in_specs = [
        pl.BlockSpec(memory_space=pltpu.SMEM),
        pl.BlockSpec((None, None, V_DIM, ta), lambda bi, h, i: (bi, i, h, 0)),
        pl.BlockSpec((None, l, V_DIM), lambda bi, h, i: (bi, 0, h)),
        pl.BlockSpec((None, nkv, V_DIM, tk), lambda bi, h, i: (bi, 0, h, 0)),
    ]
    args = [slopes, qt, k, vt]
    if has_prefix:
        kh, vht = prefix
        assert tk >= HEAD_ROWS
        in_specs += [
            pl.BlockSpec((None, HEAD_ROWS, V_DIM), lambda bi, h, i: (0, 0, h)),
            pl.BlockSpec((None, None, V_DIM, HEAD_ROWS), lambda bi, h, i: (0, 0, h, 0)),
        ]
        args += [kh, vht]
    const2 = lambda bi, h, i: (0, 0)
    in_specs += [
        pl.BlockSpec((tk, V_DIM), const2),
        pl.BlockSpec((SUM_ROWS, tk), const2),
        pl.BlockSpec((1, QK_DIM), const2), pl.BlockSpec((1, QK_DIM), const2),
        pl.BlockSpec((1, QK_DIM), const2), pl.BlockSpec((1, QK_DIM), const2),
        pl.BlockSpec((V_DIM, 1), const2),
    ]
    assert tk <= 256
    kb = jnp.zeros((tk, V_DIM), F32).at[:, :3].set(jnp.arange(tk, dtype=F32)[:, None])
    ones = jnp.zeros((SUM_ROWS, tk), F32).at[0].set(1.0)
    args += [kb.astype(BF16), ones.astype(BF16), lq1, lk1, lq2, lk2, subg]
    return pl.pallas_call(
        kern,
        grid=(b, ATT_HEADS, nq),
        in_specs=in_specs,
        out_specs=pl.BlockSpec((None, ta, V_DIM), lambda bi, h, i: (bi, i, h)),
        out_shape=jax.ShapeDtypeStruct((b, l, ATT_WIDTH), BF16),
        scratch_shapes=[
            pltpu.VMEM((2 * V_DIM, 2 * ta), BF16),
            pltpu.VMEM((tk, 2 * ta), F32),
            pltpu.VMEM((tk, 2 * ta), F32),
            pltpu.VMEM((tk, 2 * ta), BF16),
            pltpu.VMEM((tk, 2 * ta), BF16),
            pltpu.VMEM((V_DIM + SUM_ROWS, 2 * ta), F32),
        ],
        compiler_params=pltpu.CompilerParams(
            dimension_semantics=("parallel", "parallel", "arbitrary"),
            vmem_limit_bytes=VMEM_LIMIT),
        name="diff_attention",
    )(*args)


def _merge_kernel(x_ref, ys_ref, ya_ref, g1_ref, wg_ref, wso_ref, wao_ref, wo_ref, h1_ref):
    x = x_ref[...]
    hn = _rmsnorm(x, g1_ref[...]).astype(BF16)
    gates = jnp.dot(hn, wg_ref[...], preferred_element_type=F32)
    a = jnp.dot(ys_ref[...], wso_ref[...], preferred_element_type=F32)
    c = jnp.dot(ya_ref[...], wao_ref[...], preferred_element_type=F32)
    mixed = (jax.nn.sigmoid(gates[:, :D_MODEL]) * a
             + jax.nn.sigmoid(gates[:, D_MODEL:]) * c)
    h1_ref[...] = x + jnp.dot(mixed.astype(BF16), wo_ref[...], preferred_element_type=F32)


def _merge(x, ys, ya, g1, wg, wso, wao, wo, *, t):
    b, l, _ = x.shape
    const2 = lambda bi, i: (0, 0)
    row = lambda bi, i: (bi, i, 0)
    return pl.pallas_call(
        _merge_kernel,
        grid=(b, l // t),
        in_specs=[
            pl.BlockSpec((None, t, D_MODEL), row),
            pl.BlockSpec((None, t, SSM_WIDTH), row),
            pl.BlockSpec((None, t, ATT_WIDTH), row),
            pl.BlockSpec((1, D_MODEL), const2),
            pl.BlockSpec((D_MODEL, 2 * D_MODEL), const2),
            pl.BlockSpec((SSM_WIDTH, D_MODEL), const2),
            pl.BlockSpec((ATT_WIDTH, D_MODEL), const2),
            pl.BlockSpec((D_MODEL, D_MODEL), const2),
        ],
        out_specs=pl.BlockSpec((None, t, D_MODEL), row),
        out_shape=jax.ShapeDtypeStruct((b, l, D_MODEL), F32),
        compiler_params=pltpu.CompilerParams(
            dimension_semantics=("parallel", "parallel"), vmem_limit_bytes=VMEM_LIMIT),
        name="merge",
    )(x, ys, ya, g1, wg, wso, wao, wo)


def _ffn_kernel(h_ref, hprev_ref, hhead_ref, g2_ref, wa_ref, wg_ref, cw_ref, cb_ref, wdn_ref,
                out_ref, hn_ref, gate_ref):
    i = pl.program_id(1)
    t = h_ref.shape[0]
    h = h_ref[...]
    g2 = g2_ref[...]
    halo = jnp.where(i == 0, hhead_ref[...], hprev_ref[...])
    hn_ref[:HALO, :] = _rmsnorm(halo, g2).astype(BF16)
    hn_ref[HALO:, :] = _rmsnorm(h, g2).astype(BF16)
    hn = hn_ref[...]
    for f in range(N_FF_CHUNKS):
        cols = slice(f * FF_CHUNK, (f + 1) * FF_CHUNK)
        a = jnp.dot(hn, wa_ref[:, cols], preferred_element_type=F32)
        gate = jnp.dot(hn[HALO:], wg_ref[:, cols], preferred_element_type=F32)
        c = (a[HALO - 2:HALO - 2 + t] * cw_ref[0:1, cols]
             + a[HALO - 1:HALO - 1 + t] * cw_ref[1:2, cols]
             + a[HALO:] * cw_ref[2:3, cols] + cb_ref[:, cols])
        gate_ref[:, cols] = (_gelu(c) * gate).astype(BF16)
    out_ref[...] = h + jnp.dot(gate_ref[...], wdn_ref[...], preferred_element_type=F32)


def _ffn(h1, h1_head, g2, wa, wg, cw, cb, wdn, *, t):
    b, l, _ = h1.shape
    const2 = lambda bi, i: (0, 0)
    row = lambda bi, i: (bi, i, 0)
    per = t // HALO
    return pl.pallas_call(
        _ffn_kernel,
        grid=(b, l // t),
        in_specs=[
            pl.BlockSpec((None, t, D_MODEL), row),
            pl.BlockSpec((None, HALO, D_MODEL),
                         lambda bi, i: (bi, jnp.maximum(i * per - 1, 0), 0)),
            pl.BlockSpec((None, HALO, D_MODEL),
                         lambda bi, i: (0, HEAD_ROWS // HALO - 1, 0)),
            pl.BlockSpec((1, D_MODEL), const2),
            pl.BlockSpec((D_MODEL, D_FF), const2),
            pl.BlockSpec((D_MODEL, D_FF), const2),
            pl.BlockSpec((3, D_FF), const2),
            pl.BlockSpec((1, D_FF), const2),
            pl.BlockSpec((D_FF, D_MODEL), const2),
        ],
        out_specs=pl.BlockSpec((None, t, D_MODEL), row),
        out_shape=jax.ShapeDtypeStruct((b, l, D_MODEL), F32),
        scratch_shapes=[
            pltpu.VMEM((HALO + t, D_MODEL), BF16),
            pltpu.VMEM((t, D_FF), BF16),
        ],
        compiler_params=pltpu.CompilerParams(
            dimension_semantics=("parallel", "arbitrary"), vmem_limit_bytes=VMEM_LIMIT),
        name="conv_ffn",
    )(h1, h1, h1_head, g2, wa, wg, cw, cb, wdn)


def _pick_tile(l, pref):
    t = min(pref, l)
    while l % t:
        t //= 2
    return t


def _layer(x, head, p, l_idx):
    bsz, seq, _ = x.shape
    lam_init = 0.8 - 0.6 * math.exp(-0.3 * l_idx)
    slopes = 2.0 ** (-8.0 * jnp.arange(1, ATT_HEADS + 1, dtype=F32) / ATT_HEADS)

    g1 = p['norm1_g'][l_idx][None]
    w_in = p['w_in'][l_idx]
    w_mix = w_in[:, :MIX_COLS].astype(BF16)
    w_gate = w_in[:, MIX_COLS:].astype(BF16)
    qg = jnp.tile(p['q_norm_g'][l_idx], QK_COLS // QK_DIM)[None]
    kg = jnp.tile(p['k_norm_g'][l_idx], QK_COLS // QK_DIM)[None]

    lam_re, lam_im, bbt_re, bbt_im = _ssm_prep(
        p['ssm_a_re'][l_idx], p['ssm_a_im'][l_idx], p['ssm_log_dt'][l_idx],
        p['ssm_b_re'][l_idx], p['ssm_b_im'][l_idx])
    wb = jnp.concatenate([_block_diag(bbt_re), _block_diag(bbt_im)], axis=-1).astype(BF16)
    wcr = _block_diag(jnp.swapaxes(p['ssm_c_re'][l_idx], 1, 2)).astype(BF16)
    wci = _block_diag(jnp.swapaxes(p['ssm_c_im'][l_idx], 1, 2)).astype(BF16)
    ns = SSM_GROUPS * SSM_STATE
    lamr = jnp.broadcast_to(lam_re.reshape(1, ns), (bsz, ns))
    lami = jnp.broadcast_to(lam_im.reshape(1, ns), (bsz, ns))
    d = p['ssm_d'][l_idx].reshape(1, SSM_WIDTH)
    gluw = p['ssm_glu_w'][l_idx].astype(BF16)
    glub = p['ssm_glu_b'][l_idx][None]

    lq1, lk1 = p['lam_q1'][l_idx][None], p['lam_k1'][l_idx][None]
    lq2, lk2 = p['lam_q2'][l_idx][None], p['lam_k2'][l_idx][None]
    subg = p['subln_g'][l_idx][:, None]
    wso = p['w_ssm_out'][l_idx].astype(BF16)
    wao = p['w_att_out'][l_idx].astype(BF16)
    wo = p['w_o'][l_idx].astype(BF16)

    t_row = _pick_tile(seq, 512)
    ta = _pick_tile(seq, 512)
    tk = _pick_tile(seq, 256)
    tc = _pick_tile(seq, 64)

    u_h, qt_h, k_h, vt_h = _inproj(head, g1, w_mix, qg, kg, t=HEAD_ROWS, ta=HEAD_ROWS,
                                    tk=HEAD_ROWS)
    zeros_state = jnp.zeros((bsz, ns), F32)
    u_hb = jnp.broadcast_to(u_h, (bsz,) + u_h.shape[1:])
    ys_hb, s0r, s0i = _ssm(u_hb, zeros_state, zeros_state, wb, lamr, lami, wcr, wci,
                           d, gluw, glub, tc=_pick_tile(HEAD_ROWS, 64))
    ys_h = ys_hb[:1]
    ya_h = _attention(slopes, qt_h, k_h, vt_h, None, lq1, lk1, lq2, lk2, subg,
                      ta=HEAD_ROWS, tk=HEAD_ROWS, lam_init=lam_init)
    h1_head = _merge(head, ys_h, ya_h, g1, w_gate, wso, wao, wo, t=HEAD_ROWS)

    u, qt, k, vt = _inproj(x, g1, w_mix, qg, kg, t=t_row, ta=ta, tk=tk)
    ys, _, _ = _ssm(u, s0r, s0i, wb, lamr, lami, wcr, wci, d, gluw, glub, tc=tc)
    ya = _attention(slopes, qt, k, vt, (k_h, vt_h), lq1, lk1, lq2, lk2, subg,
                    ta=ta, tk=tk, lam_init=lam_init)
    h1 = _merge(x, ys, ya, g1, w_gate, wso, wao, wo, t=t_row)

    w_up = p['w_up'][l_idx]
    out = _ffn(h1, h1_head, p['norm2_g'][l_idx][None], w_up[:, :D_FF].astype(BF16),
               w_up[:, D_FF:].astype(BF16), p['conv_w'][l_idx],
               p['conv_b'][l_idx][None], p['w_down'][l_idx].astype(BF16), t=t_row)
    return out, h1_head


def kernel(x, meta_tokens, norm1_g, w_in, ssm_a_re, ssm_a_im, ssm_log_dt, ssm_b_re, ssm_b_im,
           ssm_c_re, ssm_c_im, ssm_d, ssm_glu_w, ssm_glu_b, q_norm_g, k_norm_g,
           lam_q1, lam_k1, lam_q2, lam_k2, subln_g, w_ssm_out, w_att_out, w_o,
           norm2_g, w_up, conv_w, conv_b, w_down):
    params = dict(norm1_g=norm1_g, w_in=w_in, ssm_a_re=ssm_a_re, ssm_a_im=ssm_a_im,
                  ssm_log_dt=ssm_log_dt, ssm_b_re=ssm_b_re, ssm_b_im=ssm_b_im,
                  ssm_c_re=ssm_c_re, ssm_c_im=ssm_c_im, ssm_d=ssm_d, ssm_glu_w=ssm_glu_w,
                  ssm_glu_b=ssm_glu_b, q_norm_g=q_norm_g, k_norm_g=k_norm_g,
                  lam_q1=lam_q1, lam_k1=lam_k1, lam_q2=lam_q2, lam_k2=lam_k2,
                  subln_g=subln_g, w_ssm_out=w_ssm_out, w_att_out=w_att_out, w_o=w_o,
                  norm2_g=norm2_g, w_up=w_up, conv_w=conv_w, conv_b=conv_b, w_down=w_down)
    depth = norm1_g.shape[0]
    assert depth == 1, "the head tile is only carried through one layer"
    head = jnp.concatenate(
        [jnp.zeros((N_JUNK, D_MODEL), x.dtype), meta_tokens.astype(x.dtype)], axis=0)[None]
    out, _ = _layer(x, head, params, 0)
    return out
```

```python
import functools
import math

import jax
import jax.numpy as jnp
from jax import lax
from jax.experimental import pallas as pl
from jax.experimental.pallas import tpu as pltpu

F32 = jnp.float32
BF16 = jnp.bfloat16

D_MODEL = 1024
N_META = 16
EPS = 1e-6
NEG_INF = -1e30

SSM_WIDTH = 512
SSM_GROUP = 16
SSM_GROUPS = 32
SSM_STATE = 64
SSM_BLOCKS = 4
SSM_BLOCK_IN = SSM_WIDTH // SSM_BLOCKS
SSM_BLOCK_STATE = SSM_GROUPS * SSM_STATE // SSM_BLOCKS

ATT_HEADS = 4
QK_DIM = 64
V_DIM = 128
ATT_WIDTH = 512
QK_COLS = 512
MIX_COLS = SSM_WIDTH + 2 * QK_COLS + ATT_WIDTH

D_FF = 2816
FF_CHUNK = 256
N_FF_CHUNKS = D_FF // FF_CHUNK

HEAD_ROWS = 128
N_JUNK = HEAD_ROWS - N_META
SUM_ROWS = 16
LOG2E = math.log2(math.e)
DIAG_TILE = 256
COL_TILE = 512
HALO = 16

VMEM_LIMIT = 56 * 1024 * 1024


def _rmsnorm(x, g):
    r = lax.rsqrt(jnp.mean(x * x, axis=-1, keepdims=True) + EPS)
    return (x * r) * g


def _gelu(x):
    return jax.nn.gelu(x, approximate=True)


def _ssm_prep_kernel(are_ref, aim_ref, ldt_ref, btr_ref, bti_ref,
                     lr_ref, li_ref, bbr_ref, bbi_ref):
    a_re = are_ref[...]
    a_im = aim_ref[...]
    dt = jnp.exp(ldt_ref[...])
    mag = jnp.exp(a_re * dt)
    lb_re = mag * jnp.cos(a_im * dt)
    lb_im = mag * jnp.sin(a_im * dt)
    den = a_re * a_re + a_im * a_im
    n_re = lb_re - 1.0
    f_re = (n_re * a_re + lb_im * a_im) / den
    f_im = (lb_im * a_re - n_re * a_im) / den
    lr_ref[...] = lb_re
    li_ref[...] = lb_im
    bt_re = btr_ref[...]
    bt_im = bti_ref[...]
    bbr_ref[...] = f_re[:, None, :] * bt_re - f_im[:, None, :] * bt_im
    bbi_ref[...] = f_re[:, None, :] * bt_im + f_im[:, None, :] * bt_re


def _ssm_prep(a_re, a_im, log_dt, b_re, b_im):
    g, p, c = b_re.shape
    return pl.pallas_call(
        _ssm_prep_kernel,
        out_shape=(jax.ShapeDtypeStruct((g, p), F32), jax.ShapeDtypeStruct((g, p), F32),
                   jax.ShapeDtypeStruct((g, c, p), F32), jax.ShapeDtypeStruct((g, c, p), F32)),
        name="ssm_prep",
    )(a_re, a_im, log_dt.reshape(g, 1), jnp.swapaxes(b_re, 1, 2), jnp.swapaxes(b_im, 1, 2))


def _block_diag(m):
    gpb = SSM_GROUPS // SSM_BLOCKS
    _, r, c = m.shape
    m4 = m.reshape(SSM_BLOCKS, gpb, r, c)
    eye = jnp.eye(gpb, dtype=m.dtype)
    return jnp.einsum('jgrc,gh->jgrhc', m4, eye).reshape(SSM_BLOCKS, gpb * r, gpb * c)


def _qk_norm(q, g):
    lo = lax.broadcasted_iota(jnp.int32, (1, 128), 1) < QK_DIM
    outs = []
    for c in range(QK_COLS // 128):
        blk = q[:, c * 128:(c + 1) * 128]
        sq = blk * blk
        s_lo = jnp.sum(jnp.where(lo, sq, 0.0), axis=-1, keepdims=True)
        s_hi = jnp.sum(jnp.where(lo, 0.0, sq), axis=-1, keepdims=True)
        r_lo = lax.rsqrt(s_lo / QK_DIM + EPS)
        r_hi = lax.rsqrt(s_hi / QK_DIM + EPS)
        outs.append(blk * jnp.where(lo, r_lo, r_hi))
    return jnp.concatenate(outs, axis=-1) * g


def _inproj_kernel(x_ref, g1_ref, w_ref, qg_ref, kg_ref,
                   u_ref, qt_ref, k_ref, vt_ref, *, ta, tk):
    x = x_ref[...]
    hn = _rmsnorm(x, g1_ref[...]).astype(BF16)

    def proj(c0, width):
        return jnp.dot(hn, w_ref[:, c0:c0 + width], preferred_element_type=F32)

    q = proj(SSM_WIDTH, QK_COLS)
    k = proj(SSM_WIDTH + QK_COLS, QK_COLS)
    qn = _qk_norm(q, qg_ref[...]) * (QK_DIM ** -0.5 * LOG2E)
    for c in range(x.shape[0] // ta):
        qt_ref[c] = qn[c * ta:(c + 1) * ta, :].T.astype(BF16)
    v = proj(SSM_WIDTH + 2 * QK_COLS, ATT_WIDTH)
    k_ref[...] = _qk_norm(k, kg_ref[...]).astype(BF16)
    u_ref[...] = proj(0, SSM_WIDTH)
    for c in range(x.shape[0] // tk):
        vt_ref[c] = v[c * tk:(c + 1) * tk, :].T.astype(BF16)


def _inproj(x, g1, w_mix, qg, kg, *, t, ta, tk):
    b, l, _ = x.shape
    nt = l // t
    kern = functools.partial(_inproj_kernel, ta=ta, tk=tk)
    return pl.pallas_call(
        kern,
        grid=(b, nt),
        in_specs=[
            pl.BlockSpec((None, t, D_MODEL), lambda bi, i: (bi, i, 0)),
            pl.BlockSpec((1, D_MODEL), lambda bi, i: (0, 0)),
            pl.BlockSpec((D_MODEL, MIX_COLS), lambda bi, i: (0, 0)),
            pl.BlockSpec((1, QK_COLS), lambda bi, i: (0, 0)),
            pl.BlockSpec((1, QK_COLS), lambda bi, i: (0, 0)),
        ],
        out_specs=[
            pl.BlockSpec((None, t, SSM_WIDTH), lambda bi, i: (bi, i, 0)),
            pl.BlockSpec((None, t // ta, QK_COLS, ta), lambda bi, i: (bi, i, 0, 0)),
            pl.BlockSpec((None, t, QK_COLS), lambda bi, i: (bi, i, 0)),
            pl.BlockSpec((None, t // tk, ATT_WIDTH, tk), lambda bi, i: (bi, i, 0, 0)),
        ],
        out_shape=(
            jax.ShapeDtypeStruct((b, l, SSM_WIDTH), F32),
            jax.ShapeDtypeStruct((b, l // ta, QK_COLS, ta), BF16),
            jax.ShapeDtypeStruct((b, l, QK_COLS), BF16),
            jax.ShapeDtypeStruct((b, l // tk, ATT_WIDTH, tk), BF16),
        ),
        compiler_params=pltpu.CompilerParams(
            dimension_semantics=("parallel", "parallel"), vmem_limit_bytes=VMEM_LIMIT),
        name="inproj",
    )(x, g1, w_mix, qg, kg)


def _ssm_kernel(u_ref, s0r_ref, s0i_ref, wb_ref, lamr_ref, lami_ref, wcr_ref, wci_ref,
                d_ref, gluw_ref, glub_ref,
                y_ref, sfr_ref, sfi_ref,
                xr_ref, xi_ref, sr_ref, si_ref, *, tc, nb):
    t = pl.program_id(0)

    @pl.when(t == 0)
    def _():
        sr_ref[...] = s0r_ref[...]
        si_ref[...] = s0i_ref[...]

    u = jnp.swapaxes(u_ref[...], 0, 1).reshape(tc * nb, SSM_WIDTH)
    ub = u.astype(BF16)

    def project_in(j):
        x = jnp.dot(ub[:, j * SSM_BLOCK_IN:(j + 1) * SSM_BLOCK_IN], wb_ref[j],
                    preferred_element_type=F32)
        xr_ref[j] = x[:, :SSM_BLOCK_STATE]
        xi_ref[j] = x[:, SSM_BLOCK_STATE:]

    def scan(j):
        cols = slice(j * SSM_BLOCK_STATE, (j + 1) * SSM_BLOCK_STATE)
        lr = lamr_ref[:, cols]
        li = lami_ref[:, cols]
        s_re = sr_ref[:, cols]
        s_im = si_ref[:, cols]
        for tt in range(tc):
            rows = slice(tt * nb, (tt + 1) * nb)
            s_re, s_im = (lr * s_re - li * s_im + xr_ref[j, rows, :],
                          lr * s_im + li * s_re + xi_ref[j, rows, :])
            xr_ref[j, rows, :] = s_re
            xi_ref[j, rows, :] = s_im
        sr_ref[:, cols] = s_re
        si_ref[:, cols] = s_im

    def project_out(j):
        return (jnp.dot(xr_ref[j].astype(BF16), wcr_ref[j], preferred_element_type=F32)
                - jnp.dot(xi_ref[j].astype(BF16), wci_ref[j], preferred_element_type=F32))

    ys = []
    project_in(0)
    for j in range(SSM_BLOCKS):
        if j + 1 < SSM_BLOCKS:
            project_in(j + 1)
        scan(j)
        ys.append(project_out(j))
    y = jnp.concatenate(ys, axis=-1) + d_ref[...] * u
    y = _gelu(y)
    z = jnp.dot(y.astype(BF16), gluw_ref[...], preferred_element_type=F32) + glub_ref[...]
    out = (y * jax.nn.sigmoid(z)).reshape(tc, nb, SSM_WIDTH)
    y_ref[...] = jnp.swapaxes(out, 0, 1).astype(BF16)

    @pl.when(t == pl.num_programs(0) - 1)
    def _():
        sfr_ref[...] = sr_ref[...]
        sfi_ref[...] = si_ref[...]


def _ssm(u, s0r, s0i, wb, lamr, lami, wcr, wci, d, gluw, glub, *, tc):
    nb, seq, _ = u.shape
    r = tc * nb
    nsteps = seq // tc
    ns = SSM_GROUPS * SSM_STATE
    kern = functools.partial(_ssm_kernel, tc=tc, nb=nb)
    const2 = lambda t: (0, 0)
    const3 = lambda t: (0, 0, 0)
    return pl.pallas_call(
        kern,
        grid=(nsteps,),
        in_specs=[
            pl.BlockSpec((nb, tc, SSM_WIDTH), lambda t: (0, t, 0)),
            pl.BlockSpec((nb, ns), const2),
            pl.BlockSpec((nb, ns), const2),
            pl.BlockSpec((SSM_BLOCKS, SSM_BLOCK_IN, 2 * SSM_BLOCK_STATE), const3),
            pl.BlockSpec((nb, ns), const2),
            pl.BlockSpec((nb, ns), const2),
            pl.BlockSpec((SSM_BLOCKS, SSM_BLOCK_STATE, SSM_BLOCK_IN), const3),
            pl.BlockSpec((SSM_BLOCKS, SSM_BLOCK_STATE, SSM_BLOCK_IN), const3),
            pl.BlockSpec((1, SSM_WIDTH), const2),
            pl.BlockSpec((SSM_WIDTH, SSM_WIDTH), const2),
            pl.BlockSpec((1, SSM_WIDTH), const2),
        ],
        out_specs=[
            pl.BlockSpec((nb, tc, SSM_WIDTH), lambda t: (0, t, 0)),
            pl.BlockSpec((nb, ns), const2),
            pl.BlockSpec((nb, ns), const2),
        ],
        out_shape=(
            jax.ShapeDtypeStruct((nb, seq, SSM_WIDTH), BF16),
            jax.ShapeDtypeStruct((nb, ns), F32),
            jax.ShapeDtypeStruct((nb, ns), F32),
        ),
        scratch_shapes=[
            pltpu.VMEM((SSM_BLOCKS, r, SSM_BLOCK_STATE), F32),
            pltpu.VMEM((SSM_BLOCKS, r, SSM_BLOCK_STATE), F32),
            pltpu.VMEM((nb, ns), F32),
            pltpu.VMEM((nb, ns), F32),
        ],
        compiler_params=pltpu.CompilerParams(
            dimension_semantics=("arbitrary",), vmem_limit_bytes=VMEM_LIMIT),
        name="ssm",
    )(u, s0r, s0i, wb, lamr, lami, wcr, wci, d, gluw, glub)


def _attn_kernel(slopes_ref, *rest, ta, tk, n_tiles, has_prefix, lam_init):
    qt_refs, rest = rest[:n_tiles], rest[n_tiles:]
    k_ref, vt_ref = rest[:2]
    rest = rest[2:]
    if has_prefix:
        kh_ref, vht_ref = rest[:2]
        rest = rest[2:]
    kb_ref, ones_ref, lq1_ref, lk1_ref, lq2_ref, lk2_ref, subg_ref = rest[:7]
    o_refs, rest = rest[7:7 + n_tiles], rest[7 + n_tiles:]
    q2_ref, sa_ref, sb_ref, pa_ref, pb_ref, acc_ref = rest

    h = pl.program_id(1)
    g = pl.program_id(2)
    slope = slopes_ref[h]
    nsub = ta // tk
    nq = k_ref.shape[0] // ta
    tile_idx = [g] if n_tiles == 1 else [g, nq - 1 - g]
    width = 2 * ta
    total = n_tiles * width

    for t, qt_ref in enumerate(qt_refs):
        qt = qt_ref[...]
        first = lax.broadcasted_iota(jnp.int32, qt.shape, 0) < QK_DIM
        zero = jnp.zeros_like(qt)
        q2_ref[:V_DIM, t * width:t * width + ta] = jnp.where(first, qt, zero)
        q2_ref[:V_DIM, t * width + ta:(t + 1) * width] = jnp.where(first, zero, qt)
    slope2 = jnp.full((1, total), slope * LOG2E, F32)
    c_hi = slope2.astype(BF16).astype(F32)
    c_mid = (slope2 - c_hi).astype(BF16).astype(F32)
    c_lo = slope2 - c_hi - c_mid
    bias_row = lax.broadcasted_iota(jnp.int32, (V_DIM, total), 0)
    q2_ref[V_DIM:, :] = jnp.where(
        bias_row == 0, c_hi,
        jnp.where(bias_row == 1, c_mid, jnp.where(bias_row == 2, c_lo, 0.0))).astype(BF16)
    slope = slope * LOG2E

    def key_lhs(k_blk):
        return jnp.concatenate([k_blk, kb_ref[:k_blk.shape[0], :]], axis=1)

    def value_lhs(vt_blk):
        return jnp.concatenate([vt_blk, ones_ref[:, :vt_blk.shape[1]]], axis=0)

    def key_block(kidx):
        return k_ref[pl.ds(pl.multiple_of(kidx * tk, tk), tk), :]

    def softmax_step(m_old, s, shift):
        m_new = jnp.maximum(m_old, jnp.max(s, axis=0, keepdims=True) - shift)
        alpha = jnp.exp2(m_old - m_new)
        p = jnp.exp2(s - (m_new + shift))
        return m_new, alpha, p.astype(BF16)

    sa_ref[...] = jnp.dot(key_lhs(key_block(0)), q2_ref[...], preferred_element_type=F32)
    pb_ref[...] = jnp.zeros(pb_ref.shape, BF16)
    acc_ref[...] = jnp.zeros(acc_ref.shape, F32)
    m = jnp.full((1, total), NEG_INF, F32)
    alpha = jnp.ones((1, total), F32)

    def make_body(col_tiles):
        def body(j, carry):
            m, alpha_p = carry
            n0 = 2 * j
            k_lhs = [key_lhs(key_block(n0 + st + 1)) for st in range(2)]
            v_lhs = [value_lhs(vt_ref[jnp.maximum(n0 - 1, 0)]), value_lhs(vt_ref[n0])]
            s_bufs = (sa_ref, sb_ref)
            p_bufs = (pa_ref, pb_ref)
            new_m, new_alpha = {}, {}
            for ct in col_tiles:
                cs = slice(ct * COL_TILE, (ct + 1) * COL_TILE)
                q0 = tile_idx[ct * COL_TILE // width] * ta
                shift0 = slope * (q0 - n0 * tk).astype(F32)
                m_c, alpha_c, acc = m[:, cs], alpha_p[:, cs], acc_ref[:, cs]
                for st in range(2):
                    s_bufs[(st + 1) % 2][:, cs] = jnp.dot(
                        k_lhs[st], q2_ref[:, cs], preferred_element_type=F32)
                    m_c, alpha_n, p = softmax_step(
                        m_c, s_bufs[st % 2][:, cs], shift0 - slope * (st * tk))
                    p_bufs[st % 2][:, cs] = p
                    acc = acc * alpha_c + jnp.dot(
                        v_lhs[st], p_bufs[(st + 1) % 2][:, cs], preferred_element_type=F32)
                    alpha_c = alpha_n
                acc_ref[:, cs] = acc
                new_m[ct], new_alpha[ct] = m_c, alpha_c
            n_ct = total // COL_TILE
            pick = lambda new, old: jnp.concatenate(
                [new[ct] if ct in new else old[:, ct * COL_TILE:(ct + 1) * COL_TILE]
                 for ct in range(n_ct)], axis=1)
            return pick(new_m, m), pick(new_alpha, alpha_p)
        return body

    if nsub == 2:
        per_tile = width // COL_TILE
        all_ct = list(range(total // COL_TILE))
        m, alpha = lax.fori_loop(0, tile_idx[0], make_body(all_ct), (m, alpha))
        if n_tiles == 2:
            m, alpha = lax.fori_loop(tile_idx[0], tile_idx[1], make_body(all_ct[per_tile:]),
                                     (m, alpha))
    else:
        assert nsub == 1 and n_tiles == 1 and nq == 1

    dw = min(DIAG_TILE, width)
    n_dt = total // dw
    tiles = [slice(dt * dw, (dt + 1) * dw) for dt in range(n_dt)]
    owner = [dt * dw // width for dt in range(n_dt)]
    q_base = [dt * dw % width for dt in range(n_dt)]
    q_his = [min(q_base[dt] % ta + dw, ta) - 1 for dt in range(n_dt)]
    k_idx = lax.broadcasted_iota(jnp.int32, (tk, dw), 0)
    col = lax.broadcasted_iota(jnp.int32, (tk, dw), 1)
    if has_prefix:
        k_pre = key_lhs(kh_ref[...])[N_JUNK:, :]
        s_pre = [jnp.dot(k_pre, q2_ref[:, cs], preferred_element_type=F32) for cs in tiles]
    v_pend = [value_lhs(vt_ref[jnp.maximum(ti * nsub - 1, 0)]) for ti in tile_idx]
    m_t = [m[:, cs] for cs in tiles]
    acc_t = [acc_ref[:, cs] * alpha[:, cs]
             + jnp.dot(v_pend[owner[dt]], pb_ref[:, cs], preferred_element_type=F32)
             for dt, cs in enumerate(tiles)]
    s_t = {dt: sa_ref[:, cs] for dt, cs in enumerate(tiles)}
    masks = {}
    for c in range(nsub):
        live = [dt for dt in range(n_dt) if q_his[dt] >= c * tk]
        if c + 1 < nsub:
            k_next = [key_lhs(key_block(ti * nsub + c + 1)) for ti in tile_idx]
            s_next = {dt: jnp.dot(k_next[owner[dt]], q2_ref[:, tiles[dt]],
                                  preferred_element_type=F32)
                      for dt in live if q_his[dt] >= (c + 1) * tk}
        p_t = {}
        for dt in live:
            s = s_t[dt]
            lead = q_base[dt] % ta - c * tk
            causal = lead < tk - 1 or dw > ta
            if causal or not has_prefix:
                key = (lead if causal else None, dw > ta)
                if key not in masks:
                    q_idx = (col & (ta - 1)) if dw > ta else col
                    keep = q_idx - k_idx >= -lead if causal else None
                    if not has_prefix:
                        junk = k_idx >= N_JUNK - c * tk
                        keep = junk if keep is None else jnp.logical_and(keep, junk)
                    masks[key] = keep
                s = jnp.where(masks[key], s, NEG_INF)
            m_t[dt], alpha_c, p_t[dt] = softmax_step(m_t[dt], s, slope * (-c * tk))
            acc_t[dt] = acc_t[dt] * alpha_c
        v_c = [value_lhs(vt_ref[ti * nsub + c]) for ti in tile_idx]
        for dt in live:
            acc_t[dt] = acc_t[dt] + jnp.dot(v_c[owner[dt]], p_t[dt],
                                            preferred_element_type=F32)
        if c + 1 < nsub:
            s_t = s_next
    if has_prefix:
        v_pre = value_lhs(vht_ref[...])
        no_weight = jnp.zeros((N_JUNK, dw), BF16)
        for dt in range(n_dt):
            shift = slope * (HEAD_ROWS + tile_idx[owner[dt]] * ta).astype(F32)
            m_t[dt], alpha_c, p = softmax_step(m_t[dt], s_pre[dt], shift)
            p = jnp.concatenate([no_weight, p], axis=0)
            acc_t[dt] = acc_t[dt] * alpha_c + jnp.dot(v_pre, p, preferred_element_type=F32)

    lam = (jnp.exp(jnp.sum(lq1_ref[...] * lk1_ref[...], axis=-1, keepdims=True))
           - jnp.exp(jnp.sum(lq2_ref[...] * lk2_ref[...], axis=-1, keepdims=True))
           + lam_init)
    per = width // dw
    for t, o_ref in enumerate(o_refs):
        acc = jnp.concatenate(acc_t[t * per:(t + 1) * per], axis=1)
        l = acc[V_DIM:V_DIM + 1, :]
        acc = acc[:V_DIM, :]
        o = acc[:, :ta] / l[:, :ta] - lam * (acc[:, ta:] / l[:, ta:])
        r = lax.rsqrt(jnp.mean(o * o, axis=0, keepdims=True) + EPS)
        y = ((o * r) * subg_ref[...]) * (1.0 - lam_init)
        o_ref[...] = y.T.astype(BF16)


def _attention(slopes, qt, k, vt, prefix, lq1, lk1, lq2, lk2, subg, *, ta, tk, lam_init):
    b, l, _ = k.shape
    nq = l // ta
    nkv = l // tk
    assert ta & (ta - 1) == 0 and ta % tk == 0
    has_prefix = prefix is not None
    n_tiles = 2 if nq % 2 == 0 else 1
    steps = nq // n_tiles
    kern = functools.partial(_attn_kernel, ta=ta, tk=tk, n_tiles=n_tiles,
                             has_prefix=has_prefix, lam_init=lam_init)
    in_specs = [pl.BlockSpec(memory_space=pltpu.SMEM),
                pl.BlockSpec((None, None, V_DIM, ta), lambda bi, h, g: (bi, g, h, 0))]
    args = [slopes, qt]
    if n_tiles == 2:
        in_specs.append(
            pl.BlockSpec((None, None, V_DIM, ta), lambda bi, h, g: (bi, nq - 1 - g, h, 0)))
        args.append(qt)
    in_specs += [
        pl.BlockSpec((None, l, V_DIM), lambda bi, h, g: (bi, 0, h)),
        pl.BlockSpec((None, nkv, V_DIM, tk), lambda bi, h, g: (bi, 0, h, 0)),
    ]
    args += [k, vt]
    if has_prefix:
        kh, vht = prefix
        assert tk >= HEAD_ROWS
        in_specs += [
            pl.BlockSpec((None, HEAD_ROWS, V_DIM), lambda bi, h, g: (0, 0, h)),
            pl.BlockSpec((None, None, V_DIM, HEAD_ROWS), lambda bi, h, g: (0, 0, h, 0)),
        ]
        args += [kh, vht]
    const2 = lambda bi, h, g: (0, 0)
    in_specs += [
        pl.BlockSpec((tk, V_DIM), const2),
        pl.BlockSpec((SUM_ROWS, tk), const2),
        pl.BlockSpec((1, QK_DIM), const2), pl.BlockSpec((1, QK_DIM), const2),
        pl.BlockSpec((1, QK_DIM), const2), pl.BlockSpec((1, QK_DIM), const2),
        pl.BlockSpec((V_DIM, 1), const2),
    ]
    assert tk <= 256
    kb = jnp.zeros((tk, V_DIM), F32).at[:, :3].set(jnp.arange(tk, dtype=F32)[:, None])
    ones = jnp.zeros((SUM_ROWS, tk), F32).at[0].set(1.0)
    args += [kb.astype(BF16), ones.astype(BF16), lq1, lk1, lq2, lk2, subg]
    if n_tiles == 2:
        out_specs = [pl.BlockSpec((None, ta, V_DIM), lambda bi, h, g: (bi, g, h)),
                     pl.BlockSpec((None, ta, V_DIM), lambda bi, h, g: (bi, steps - 1 - g, h))]
        out_shape = [jax.ShapeDtypeStruct((b, l // 2, ATT_WIDTH), BF16)] * 2
    else:
        out_specs = [pl.BlockSpec((None, ta, V_DIM), lambda bi, h, g: (bi, g, h))]
        out_shape = [jax.ShapeDtypeStruct((b, l, ATT_WIDTH), BF16)]
    total = n_tiles * 2 * ta
    outs = pl.pallas_call(
        kern,
        grid=(b, ATT_HEADS, steps),
        in_specs=in_specs,
        out_specs=out_specs,
        out_shape=out_shape,
        scratch_shapes=[
            pltpu.VMEM((2 * V_DIM, total), BF16),
            pltpu.VMEM((tk, total), F32),
            pltpu.VMEM((tk, total), F32),
            pltpu.VMEM((tk, total), BF16),
            pltpu.VMEM((tk, total), BF16),
            pltpu.VMEM((V_DIM + SUM_ROWS, total), F32),
        ],
        compiler_params=pltpu.CompilerParams(
            dimension_semantics=("parallel", "parallel", "arbitrary"),
            vmem_limit_bytes=VMEM_LIMIT),
        name="diff_attention",
    )(*args)
    return outs[0] if n_tiles == 1 else tuple(outs)


def _merge_kernel(x_ref, ys_ref, *rest, halves):
    ya_refs, rest = rest[:halves], rest[halves:]
    g1_ref, wg_ref, wso_ref, wao_ref, wo_ref, h1_ref = rest
    x = x_ref[...]
    hn = _rmsnorm(x, g1_ref[...]).astype(BF16)
    gates = jnp.dot(hn, wg_ref[...], preferred_element_type=F32)
    a = jnp.dot(ys_ref[...], wso_ref[...], preferred_element_type=F32)
    ya = ya_refs[0][...]
    if halves == 2:
        ya = jnp.where(pl.program_id(1) < pl.num_programs(1) // 2, ya, ya_refs[1][...])
    c = jnp.dot(ya, wao_ref[...], preferred_element_type=F32)
    mixed = (jax.nn.sigmoid(gates[:, :D_MODEL]) * a
             + jax.nn.sigmoid(gates[:, D_MODEL:]) * c)
    h1_ref[...] = x + jnp.dot(mixed.astype(BF16), wo_ref[...], preferred_element_type=F32)


def _merge(x, ys, ya, g1, wg, wso, wao, wo, *, t):
    b, l, _ = x.shape
    const2 = lambda bi, i: (0, 0)
    row = lambda bi, i: (bi, i, 0)
    if isinstance(ya, tuple):
        half = l // t // 2
        ya_specs = [
            pl.BlockSpec((None, t, ATT_WIDTH), lambda bi, i: (bi, jnp.minimum(i, half - 1), 0)),
            pl.BlockSpec((None, t, ATT_WIDTH), lambda bi, i: (bi, jnp.maximum(i - half, 0), 0))]
    else:
        ya, ya_specs = (ya,), [pl.BlockSpec((None, t, ATT_WIDTH), row)]
    return pl.pallas_call(
        functools.partial(_merge_kernel, halves=len(ya)),
        grid=(b, l // t),
        in_specs=[
            pl.BlockSpec((None, t, D_MODEL), row),
            pl.BlockSpec((None, t, SSM_WIDTH), row),
            *ya_specs,
            pl.BlockSpec((1, D_MODEL), const2),
            pl.BlockSpec((D_MODEL, 2 * D_MODEL), const2),
            pl.BlockSpec((SSM_WIDTH, D_MODEL), const2),
            pl.BlockSpec((ATT_WIDTH, D_MODEL), const2),
            pl.BlockSpec((D_MODEL, D_MODEL), const2),
        ],
        out_specs=pl.BlockSpec((None, t, D_MODEL), row),
        out_shape=jax.ShapeDtypeStruct((b, l, D_MODEL), F32),
        compiler_params=pltpu.CompilerParams(
            dimension_semantics=("parallel", "parallel"), vmem_limit_bytes=VMEM_LIMIT),
        name="merge",
    )(x, ys, *ya, g1, wg, wso, wao, wo)


def _ffn_kernel(h_ref, hprev_ref, hhead_ref, g2_ref, wa_ref, wg_ref, cw_ref, cb_ref, wdn_ref,
                out_ref, hn_ref, gate_ref):
    i = pl.program_id(1)
    t = h_ref.shape[0]
    h = h_ref[...]
    g2 = g2_ref[...]
    halo = jnp.where(i == 0, hhead_ref[...], hprev_ref[...])
    hn_ref[:HALO, :] = _rmsnorm(halo, g2).astype(BF16)
    hn_ref[HALO:, :] = _rmsnorm(h, g2).astype(BF16)
    hn = hn_ref[...]
    for f in range(N_FF_CHUNKS):
        cols = slice(f * FF_CHUNK, (f + 1) * FF_CHUNK)
        a = jnp.dot(hn, wa_ref[:, cols], preferred_element_type=F32)
        gate = jnp.dot(hn[HALO:], wg_ref[:, cols], preferred_element_type=F32)
        c = (a[HALO - 2:HALO - 2 + t] * cw_ref[0:1, cols]
             + a[HALO - 1:HALO - 1 + t] * cw_ref[1:2, cols]
             + a[HALO:] * cw_ref[2:3, cols] + cb_ref[:, cols])
        gate_ref[:, cols] = (_gelu(c) * gate).astype(BF16)
    out_ref[...] = h + jnp.dot(gate_ref[...], wdn_ref[...], preferred_element_type=F32)


def _ffn(h1, h1_head, g2, wa, wg, cw, cb, wdn, *, t):
    b, l, _ = h1.shape
    const2 = lambda bi, i: (0, 0)
    row = lambda bi, i: (bi, i, 0)
    per = t // HALO
    return pl.pallas_call(
        _ffn_kernel,
        grid=(b, l // t),
        in_specs=[
            pl.BlockSpec((None, t, D_MODEL), row),
            pl.BlockSpec((None, HALO, D_MODEL),
                         lambda bi, i: (bi, jnp.maximum(i * per - 1, 0), 0)),
            pl.BlockSpec((None, HALO, D_MODEL),
                         lambda bi, i: (0, HEAD_ROWS // HALO - 1, 0)),
            pl.BlockSpec((1, D_MODEL), const2),
            pl.BlockSpec((D_MODEL, D_FF), const2),
            pl.BlockSpec((D_MODEL, D_FF), const2),
            pl.BlockSpec((3, D_FF), const2),
            pl.BlockSpec((1, D_FF), const2),
            pl.BlockSpec((D_FF, D_MODEL), const2),
        ],
        out_specs=pl.BlockSpec((None, t, D_MODEL), row),
        out_shape=jax.ShapeDtypeStruct((b, l, D_MODEL), F32),
        scratch_shapes=[
            pltpu.VMEM((HALO + t, D_MODEL), BF16),
            pltpu.VMEM((t, D_FF), BF16),
        ],
        compiler_params=pltpu.CompilerParams(
            dimension_semantics=("parallel", "arbitrary"), vmem_limit_bytes=VMEM_LIMIT),
        name="conv_ffn",
    )(h1, h1, h1_head, g2, wa, wg, cw, cb, wdn)


def _pick_tile(l, pref):
    t = min(pref, l)
    while l % t:
        t //= 2
    return t


def _layer(x, head, p, l_idx):
    bsz, seq, _ = x.shape
    lam_init = 0.8 - 0.6 * math.exp(-0.3 * l_idx)
    slopes = 2.0 ** (-8.0 * jnp.arange(1, ATT_HEADS + 1, dtype=F32) / ATT_HEADS)

    g1 = p['norm1_g'][l_idx][None]
    w_in = p['w_in'][l_idx]
    w_mix = w_in[:, :MIX_COLS].astype(BF16)
    w_gate = w_in[:, MIX_COLS:].astype(BF16)
    qg = jnp.tile(p['q_norm_g'][l_idx], QK_COLS // QK_DIM)[None]
    kg = jnp.tile(p['k_norm_g'][l_idx], QK_COLS // QK_DIM)[None]

    lam_re, lam_im, bbt_re, bbt_im = _ssm_prep(
        p['ssm_a_re'][l_idx], p['ssm_a_im'][l_idx], p['ssm_log_dt'][l_idx],
        p['ssm_b_re'][l_idx], p['ssm_b_im'][l_idx])
    wb = jnp.concatenate([_block_diag(bbt_re), _block_diag(bbt_im)], axis=-1).astype(BF16)
    wcr = _block_diag(jnp.swapaxes(p['ssm_c_re'][l_idx], 1, 2)).astype(BF16)
    wci = _block_diag(jnp.swapaxes(p['ssm_c_im'][l_idx], 1, 2)).astype(BF16)
    ns = SSM_GROUPS * SSM_STATE
    lamr = jnp.broadcast_to(lam_re.reshape(1, ns), (bsz, ns))
    lami = jnp.broadcast_to(lam_im.reshape(1, ns), (bsz, ns))
    d = p['ssm_d'][l_idx].reshape(1, SSM_WIDTH)
    gluw = p['ssm_glu_w'][l_idx].astype(BF16)
    glub = p['ssm_glu_b'][l_idx][None]

    lq1, lk1 = p['lam_q1'][l_idx][None], p['lam_k1'][l_idx][None]
    lq2, lk2 = p['lam_q2'][l_idx][None], p['lam_k2'][l_idx][None]
    subg = p['subln_g'][l_idx][:, None]
    wso = p['w_ssm_out'][l_idx].astype(BF16)
    wao = p['w_att_out'][l_idx].astype(BF16)
    wo = p['w_o'][l_idx].astype(BF16)

    t_row = _pick_tile(seq, 512)
    ta = _pick_tile(seq, 512)
    tk = _pick_tile(seq, 256)
    tc = _pick_tile(seq, 64)

    u_h, qt_h, k_h, vt_h = _inproj(head, g1, w_mix, qg, kg, t=HEAD_ROWS, ta=HEAD_ROWS,
                                    tk=HEAD_ROWS)
    zeros_state = jnp.zeros((bsz, ns), F32)
    u_hb = jnp.broadcast_to(u_h, (bsz,) + u_h.shape[1:])
    ys_hb, s0r, s0i = _ssm(u_hb, zeros_state, zeros_state, wb, lamr, lami, wcr, wci,
                           d, gluw, glub, tc=_pick_tile(HEAD_ROWS, 64))
    ys_h = ys_hb[:1]
    ya_h = _attention(slopes, qt_h, k_h, vt_h, None, lq1, lk1, lq2, lk2, subg,
                      ta=HEAD_ROWS, tk=HEAD_ROWS, lam_init=lam_init)
    h1_head = _merge(head, ys_h, ya_h, g1, w_gate, wso, wao, wo, t=HEAD_ROWS)

    u, qt, k, vt = _inproj(x, g1, w_mix, qg, kg, t=t_row, ta=ta, tk=tk)
    ys, _, _ = _ssm(u, s0r, s0i, wb, lamr, lami, wcr, wci, d, gluw, glub, tc=tc)
    ya = _attention(slopes, qt, k, vt, (k_h, vt_h), lq1, lk1, lq2, lk2, subg,
                    ta=ta, tk=tk, lam_init=lam_init)
    h1 = _merge(x, ys, ya, g1, w_gate, wso, wao, wo, t=t_row)

    w_up = p['w_up'][l_idx]
    out = _ffn(h1, h1_head, p['norm2_g'][l_idx][None], w_up[:, :D_FF].astype(BF16),
               w_up[:, D_FF:].astype(BF16), p['conv_w'][l_idx],
               p['conv_b'][l_idx][None], p['w_down'][l_idx].astype(BF16), t=t_row)
    return out, h1_head


def kernel(x, meta_tokens, norm1_g, w_in, ssm_a_re, ssm_a_im, ssm_log_dt, ssm_b_re, ssm_b_im,
           ssm_c_re, ssm_c_im, ssm_d, ssm_glu_w, ssm_glu_b, q_norm_g, k_norm_g,
           lam_q1, lam_k1, lam_q2, lam_k2, subln_g, w_ssm_out, w_att_out, w_o,
           norm2_g, w_up, conv_w, conv_b, w_down):
    params = dict(norm1_g=norm1_g, w_in=w_in, ssm_a_re=ssm_a_re, ssm_a_im=ssm_a_im,
                  ssm_log_dt=ssm_log_dt, ssm_b_re=ssm_b_re, ssm_b_im=ssm_b_im,
                  ssm_c_re=ssm_c_re, ssm_c_im=ssm_c_im, ssm_d=ssm_d, ssm_glu_w=ssm_glu_w,
                  ssm_glu_b=ssm_glu_b, q_norm_g=q_norm_g, k_norm_g=k_norm_g,
                  lam_q1=lam_q1, lam_k1=lam_k1, lam_q2=lam_q2, lam_k2=lam_k2,
                  subln_g=subln_g, w_ssm_out=w_ssm_out, w_att_out=w_att_out, w_o=w_o,
                  norm2_g=norm2_g, w_up=w_up, conv_w=conv_w, conv_b=conv_b, w_down=w_down)
    depth = norm1_g.shape[0]
    assert depth == 1, "the head tile is only carried through one layer"
    head = jnp.concatenate(
        [jnp.zeros((N_JUNK, D_MODEL), x.dtype), meta_tokens.astype(x.dtype)], axis=0)[None]
    out, _ = _layer(x, head, params, 0)
    return out
```

```python
import functools
import math

import jax
import jax.numpy as jnp
from jax import lax
from jax.experimental import pallas as pl
from jax.experimental.pallas import tpu as pltpu

F32 = jnp.float32
BF16 = jnp.bfloat16

D_MODEL = 1024
N_META = 16
EPS = 1e-6
NEG_INF = -1e30

SSM_WIDTH = 512
SSM_GROUP = 16
SSM_GROUPS = 32
SSM_STATE = 64
SSM_BLOCKS = 4
SSM_BLOCK_IN = SSM_WIDTH // SSM_BLOCKS
SSM_BLOCK_STATE = SSM_GROUPS * SSM_STATE // SSM_BLOCKS

ATT_HEADS = 4
QK_DIM = 64
V_DIM = 128
ATT_WIDTH = 512
QK_COLS = 512
MIX_COLS = SSM_WIDTH + 2 * QK_COLS + ATT_WIDTH

D_FF = 2816
FF_CHUNK = 256
N_FF_CHUNKS = D_FF // FF_CHUNK

HEAD_ROWS = 128
N_JUNK = HEAD_ROWS - N_META
SUM_ROWS = 16
LOG2E = math.log2(math.e)
DIAG_TILE = 256
COL_TILE = 512
HALO = 16

VMEM_LIMIT = 56 * 1024 * 1024


def _rmsnorm(x, g):
    r = lax.rsqrt(jnp.mean(x * x, axis=-1, keepdims=True) + EPS)
    return (x * r) * g


def _gelu(x):
    return jax.nn.gelu(x, approximate=True)


def _ssm_prep_kernel(are_ref, aim_ref, ldt_ref, btr_ref, bti_ref,
                     lr_ref, li_ref, bbr_ref, bbi_ref):
    a_re = are_ref[...]
    a_im = aim_ref[...]
    dt = jnp.exp(ldt_ref[...])
    mag = jnp.exp(a_re * dt)
    lb_re = mag * jnp.cos(a_im * dt)
    lb_im = mag * jnp.sin(a_im * dt)
    den = a_re * a_re + a_im * a_im
    n_re = lb_re - 1.0
    f_re = (n_re * a_re + lb_im * a_im) / den
    f_im = (lb_im * a_re - n_re * a_im) / den
    lr_ref[...] = lb_re
    li_ref[...] = lb_im
    bt_re = btr_ref[...]
    bt_im = bti_ref[...]
    bbr_ref[...] = f_re[:, None, :] * bt_re - f_im[:, None, :] * bt_im
    bbi_ref[...] = f_re[:, None, :] * bt_im + f_im[:, None, :] * bt_re


def _ssm_prep(a_re, a_im, log_dt, b_re, b_im):
    g, p, c = b_re.shape
    return pl.pallas_call(
        _ssm_prep_kernel,
        out_shape=(jax.ShapeDtypeStruct((g, p), F32), jax.ShapeDtypeStruct((g, p), F32),
                   jax.ShapeDtypeStruct((g, c, p), F32), jax.ShapeDtypeStruct((g, c, p), F32)),
        name="ssm_prep",
    )(a_re, a_im, log_dt.reshape(g, 1), jnp.swapaxes(b_re, 1, 2), jnp.swapaxes(b_im, 1, 2))


def _block_diag(m):
    gpb = SSM_GROUPS // SSM_BLOCKS
    _, r, c = m.shape
    m4 = m.reshape(SSM_BLOCKS, gpb, r, c)
    eye = jnp.eye(gpb, dtype=m.dtype)
    return jnp.einsum('jgrc,gh->jgrhc', m4, eye).reshape(SSM_BLOCKS, gpb * r, gpb * c)


def _qk_norm(q, g):
    lo = lax.broadcasted_iota(jnp.int32, (1, 128), 1) < QK_DIM
    outs = []
    for c in range(QK_COLS // 128):
        blk = q[:, c * 128:(c + 1) * 128]
        sq = blk * blk
        s_lo = jnp.sum(jnp.where(lo, sq, 0.0), axis=-1, keepdims=True)
        s_hi = jnp.sum(jnp.where(lo, 0.0, sq), axis=-1, keepdims=True)
        r_lo = lax.rsqrt(s_lo / QK_DIM + EPS)
        r_hi = lax.rsqrt(s_hi / QK_DIM + EPS)
        outs.append(blk * jnp.where(lo, r_lo, r_hi))
    return jnp.concatenate(outs, axis=-1) * g


def _inproj_kernel(x_ref, g1_ref, w_ref, qg_ref, kg_ref,
                   u_ref, qt_ref, k_ref, vt_ref, *, ta, tk):
    x = x_ref[...]
    hn = _rmsnorm(x, g1_ref[...]).astype(BF16)

    def proj(c0, width):
        return jnp.dot(hn, w_ref[:, c0:c0 + width], preferred_element_type=F32)

    q = proj(SSM_WIDTH, QK_COLS)
    k = proj(SSM_WIDTH + QK_COLS, QK_COLS)
    qn = _qk_norm(q, qg_ref[...]) * (QK_DIM ** -0.5 * LOG2E)
    for c in range(x.shape[0] // ta):
        qt_ref[c] = qn[c * ta:(c + 1) * ta, :].T.astype(BF16)
    v = proj(SSM_WIDTH + 2 * QK_COLS, ATT_WIDTH)
    k_ref[...] = _qk_norm(k, kg_ref[...]).astype(BF16)
    u_ref[...] = proj(0, SSM_WIDTH)
    for c in range(x.shape[0] // tk):
        vt_ref[c] = v[c * tk:(c + 1) * tk, :].T.astype(BF16)


def _inproj(x, g1, w_mix, qg, kg, *, t, ta, tk):
    b, l, _ = x.shape
    nt = l // t
    kern = functools.partial(_inproj_kernel, ta=ta, tk=tk)
    return pl.pallas_call(
        kern,
        grid=(b, nt),
        in_specs=[
            pl.BlockSpec((None, t, D_MODEL), lambda bi, i: (bi, i, 0)),
            pl.BlockSpec((1, D_MODEL), lambda bi, i: (0, 0)),
            pl.BlockSpec((D_MODEL, MIX_COLS), lambda bi, i: (0, 0)),
            pl.BlockSpec((1, QK_COLS), lambda bi, i: (0, 0)),
            pl.BlockSpec((1, QK_COLS), lambda bi, i: (0, 0)),
        ],
        out_specs=[
            pl.BlockSpec((None, t, SSM_WIDTH), lambda bi, i: (bi, i, 0)),
            pl.BlockSpec((None, t // ta, QK_COLS, ta), lambda bi, i: (bi, i, 0, 0)),
            pl.BlockSpec((None, t, QK_COLS), lambda bi, i: (bi, i, 0)),
            pl.BlockSpec((None, t // tk, ATT_WIDTH, tk), lambda bi, i: (bi, i, 0, 0)),
        ],
        out_shape=(
            jax.ShapeDtypeStruct((b, l, SSM_WIDTH), F32),
            jax.ShapeDtypeStruct((b, l // ta, QK_COLS, ta), BF16),
            jax.ShapeDtypeStruct((b, l, QK_COLS), BF16),
            jax.ShapeDtypeStruct((b, l // tk, ATT_WIDTH, tk), BF16),
        ),
        compiler_params=pltpu.CompilerParams(
            dimension_semantics=("parallel", "parallel"), vmem_limit_bytes=VMEM_LIMIT),
        name="inproj",
    )(x, g1, w_mix, qg, kg)


def _ssm_kernel(u_ref, s0r_ref, s0i_ref, wb_ref, lamr_ref, lami_ref, wcr_ref, wci_ref,
                d_ref, gluw_ref, glub_ref,
                y_ref, sfr_ref, sfi_ref,
                xr_ref, xi_ref, sr_ref, si_ref, *, tc, nb):
    t = pl.program_id(0)

    @pl.when(t == 0)
    def _():
        sr_ref[...] = s0r_ref[...]
        si_ref[...] = s0i_ref[...]

    u = jnp.swapaxes(u_ref[...], 0, 1).reshape(tc * nb, SSM_WIDTH)
    ub = u.astype(BF16)

    def project_in(j):
        x = jnp.dot(ub[:, j * SSM_BLOCK_IN:(j + 1) * SSM_BLOCK_IN], wb_ref[j],
                    preferred_element_type=F32)
        xr_ref[j] = x[:, :SSM_BLOCK_STATE]
        xi_ref[j] = x[:, SSM_BLOCK_STATE:]

    def scan(j):
        cols = slice(j * SSM_BLOCK_STATE, (j + 1) * SSM_BLOCK_STATE)
        lr = lamr_ref[:, cols]
        li = lami_ref[:, cols]
        s_re = sr_ref[:, cols]
        s_im = si_ref[:, cols]
        for tt in range(tc):
            rows = slice(tt * nb, (tt + 1) * nb)
            s_re, s_im = (lr * s_re - li * s_im + xr_ref[j, rows, :],
                          lr * s_im + li * s_re + xi_ref[j, rows, :])
            xr_ref[j, rows, :] = s_re
            xi_ref[j, rows, :] = s_im
        sr_ref[:, cols] = s_re
        si_ref[:, cols] = s_im

    def project_out(j):
        return (jnp.dot(xr_ref[j].astype(BF16), wcr_ref[j], preferred_element_type=F32)
                - jnp.dot(xi_ref[j].astype(BF16), wci_ref[j], preferred_element_type=F32))

    ys = []
    project_in(0)
    for j in range(SSM_BLOCKS):
        if j + 1 < SSM_BLOCKS:
            project_in(j + 1)
        scan(j)
        ys.append(project_out(j))
    y = jnp.concatenate(ys, axis=-1) + d_ref[...] * u
    y = _gelu(y)
    z = jnp.dot(y.astype(BF16), gluw_ref[...], preferred_element_type=F32) + glub_ref[...]
    out = (y * jax.nn.sigmoid(z)).reshape(tc, nb, SSM_WIDTH)
    y_ref[...] = jnp.swapaxes(out, 0, 1).astype(BF16)

    @pl.when(t == pl.num_programs(0) - 1)
    def _():
        sfr_ref[...] = sr_ref[...]
        sfi_ref[...] = si_ref[...]


def _ssm(u, s0r, s0i, wb, lamr, lami, wcr, wci, d, gluw, glub, *, tc):
    nb, seq, _ = u.shape
    r = tc * nb
    nsteps = seq // tc
    ns = SSM_GROUPS * SSM_STATE
    kern = functools.partial(_ssm_kernel, tc=tc, nb=nb)
    const2 = lambda t: (0, 0)
    const3 = lambda t: (0, 0, 0)
    return pl.pallas_call(
        kern,
        grid=(nsteps,),
        in_specs=[
            pl.BlockSpec((nb, tc, SSM_WIDTH), lambda t: (0, t, 0)),
            pl.BlockSpec((nb, ns), const2),
            pl.BlockSpec((nb, ns), const2),
            pl.BlockSpec((SSM_BLOCKS, SSM_BLOCK_IN, 2 * SSM_BLOCK_STATE), const3),
            pl.BlockSpec((nb, ns), const2),
            pl.BlockSpec((nb, ns), const2),
            pl.BlockSpec((SSM_BLOCKS, SSM_BLOCK_STATE, SSM_BLOCK_IN), const3),
            pl.BlockSpec((SSM_BLOCKS, SSM_BLOCK_STATE, SSM_BLOCK_IN), const3),
            pl.BlockSpec((1, SSM_WIDTH), const2),
            pl.BlockSpec((SSM_WIDTH, SSM_WIDTH), const2),
            pl.BlockSpec((1, SSM_WIDTH), const2),
        ],
        out_specs=[
            pl.BlockSpec((nb, tc, SSM_WIDTH), lambda t: (0, t, 0)),
            pl.BlockSpec((nb, ns), const2),
            pl.BlockSpec((nb, ns), const2),
        ],
        out_shape=(
            jax.ShapeDtypeStruct((nb, seq, SSM_WIDTH), BF16),
            jax.ShapeDtypeStruct((nb, ns), F32),
            jax.ShapeDtypeStruct((nb, ns), F32),
        ),
        scratch_shapes=[
            pltpu.VMEM((SSM_BLOCKS, r, SSM_BLOCK_STATE), F32),
            pltpu.VMEM((SSM_BLOCKS, r, SSM_BLOCK_STATE), F32),
            pltpu.VMEM((nb, ns), F32),
            pltpu.VMEM((nb, ns), F32),
        ],
        compiler_params=pltpu.CompilerParams(
            dimension_semantics=("arbitrary",), vmem_limit_bytes=VMEM_LIMIT),
        name="ssm",
    )(u, s0r, s0i, wb, lamr, lami, wcr, wci, d, gluw, glub)


def _attn_kernel(slopes_ref, *rest, ta, tk, n_tiles, has_prefix, lam_init):
    qt_refs, rest = rest[:n_tiles], rest[n_tiles:]
    k_ref, vt_ref = rest[:2]
    rest = rest[2:]
    if has_prefix:
        kh_ref, vht_ref = rest[:2]
        rest = rest[2:]
    kb_ref, ones_ref, lq1_ref, lk1_ref, lq2_ref, lk2_ref, subg_ref = rest[:7]
    o_refs, rest = rest[7:7 + n_tiles], rest[7 + n_tiles:]
    q2_ref, sa_ref, sb_ref, pa_ref, pb_ref, acc_ref = rest

    h = pl.program_id(1)
    g = pl.program_id(2)
    slope = slopes_ref[h]
    nsub = ta // tk
    nq = k_ref.shape[0] // ta
    tile_idx = [g] if n_tiles == 1 else [g, nq - 1 - g]
    width = 2 * ta
    total = n_tiles * width

    for t, qt_ref in enumerate(qt_refs):
        qt = qt_ref[...]
        first = lax.broadcasted_iota(jnp.int32, qt.shape, 0) < QK_DIM
        zero = jnp.zeros_like(qt)
        q2_ref[:V_DIM, t * width:t * width + ta] = jnp.where(first, qt, zero)
        q2_ref[:V_DIM, t * width + ta:(t + 1) * width] = jnp.where(first, zero, qt)
    slope2 = jnp.full((1, total), slope * LOG2E, F32)
    c_hi = slope2.astype(BF16).astype(F32)
    c_mid = (slope2 - c_hi).astype(BF16).astype(F32)
    c_lo = slope2 - c_hi - c_mid
    bias_row = lax.broadcasted_iota(jnp.int32, (V_DIM, total), 0)
    q2_ref[V_DIM:, :] = jnp.where(
        bias_row == 0, c_hi,
        jnp.where(bias_row == 1, c_mid, jnp.where(bias_row == 2, c_lo, 0.0))).astype(BF16)
    slope = slope * LOG2E

    def key_lhs(k_blk):
        return jnp.concatenate([k_blk, kb_ref[:k_blk.shape[0], :]], axis=1)

    def value_lhs(vt_blk):
        return jnp.concatenate([vt_blk, ones_ref[:, :vt_blk.shape[1]]], axis=0)

    def key_block(kidx):
        return k_ref[pl.ds(pl.multiple_of(kidx * tk, tk), tk), :]

    def softmax_step(m_old, s, shift):
        m_new = jnp.maximum(m_old, jnp.max(s, axis=0, keepdims=True) - shift)
        alpha = jnp.exp2(m_old - m_new)
        p = jnp.exp2(s - (m_new + shift))
        return m_new, alpha, p.astype(BF16)

    sa_ref[...] = jnp.dot(key_lhs(key_block(0)), q2_ref[...], preferred_element_type=F32)
    pb_ref[...] = jnp.zeros(pb_ref.shape, BF16)
    acc_ref[...] = jnp.zeros(acc_ref.shape, F32)
    m = jnp.full((1, total), NEG_INF, F32)
    alpha = jnp.ones((1, total), F32)

    def make_body(col_tiles):
        def body(j, carry):
            m, alpha_p = carry
            n0 = 2 * j
            k_lhs = [key_lhs(key_block(n0 + st + 1)) for st in range(2)]
            v_lhs = [value_lhs(vt_ref[jnp.maximum(n0 - 1, 0)]), value_lhs(vt_ref[n0])]
            s_bufs = (sa_ref, sb_ref)
            p_bufs = (pa_ref, pb_ref)
            new_m, new_alpha = {}, {}
            for ct in col_tiles:
                cs = slice(ct * COL_TILE, (ct + 1) * COL_TILE)
                q0 = tile_idx[ct * COL_TILE // width] * ta
                shift0 = slope * (q0 - n0 * tk).astype(F32)
                m_c, alpha_c, acc = m[:, cs], alpha_p[:, cs], acc_ref[:, cs]
                for st in range(2):
                    s_bufs[(st + 1) % 2][:, cs] = jnp.dot(
                        k_lhs[st], q2_ref[:, cs], preferred_element_type=F32)
                    m_c, alpha_n, p = softmax_step(
                        m_c, s_bufs[st % 2][:, cs], shift0 - slope * (st * tk))
                    p_bufs[st % 2][:, cs] = p
                    acc = acc * alpha_c + jnp.dot(
                        v_lhs[st], p_bufs[(st + 1) % 2][:, cs], preferred_element_type=F32)
                    alpha_c = alpha_n
                acc_ref[:, cs] = acc
                new_m[ct], new_alpha[ct] = m_c, alpha_c
            n_ct = total // COL_TILE
            pick = lambda new, old: jnp.concatenate(
                [new[ct] if ct in new else old[:, ct * COL_TILE:(ct + 1) * COL_TILE]
                 for ct in range(n_ct)], axis=1)
            return pick(new_m, m), pick(new_alpha, alpha_p)
        return body

    if nsub == 2:
        per_tile = width // COL_TILE
        all_ct = list(range(total // COL_TILE))
        m, alpha = lax.fori_loop(0, tile_idx[0], make_body(all_ct), (m, alpha))
        if n_tiles == 2:
            m, alpha = lax.fori_loop(tile_idx[0], tile_idx[1], make_body(all_ct[per_tile:]),
                                     (m, alpha))
    else:
        assert nsub == 1 and n_tiles == 1 and nq == 1

    dw = min(DIAG_TILE, width)
    n_dt = total // dw
    tiles = [slice(dt * dw, (dt + 1) * dw) for dt in range(n_dt)]
    owner = [dt * dw // width for dt in range(n_dt)]
    q_base = [dt * dw % width for dt in range(n_dt)]
    q_his = [min(q_base[dt] % ta + dw, ta) - 1 for dt in range(n_dt)]
    k_idx = lax.broadcasted_iota(jnp.int32, (tk, dw), 0)
    col = lax.broadcasted_iota(jnp.int32, (tk, dw), 1)
    if has_prefix:
        k_pre = key_lhs(kh_ref[...])[N_JUNK:, :]
        s_pre = [jnp.dot(k_pre, q2_ref[:, cs], preferred_element_type=F32) for cs in tiles]
    v_pend = [value_lhs(vt_ref[jnp.maximum(ti * nsub - 1, 0)]) for ti in tile_idx]
    m_t = [m[:, cs] for cs in tiles]
    acc_t = [acc_ref[:, cs] * alpha[:, cs]
             + jnp.dot(v_pend[owner[dt]], pb_ref[:, cs], preferred_element_type=F32)
             for dt, cs in enumerate(tiles)]
    s_t = {dt: sa_ref[:, cs] for dt, cs in enumerate(tiles)}
    masks = {}
    for c in range(nsub):
        live = [dt for dt in range(n_dt) if q_his[dt] >= c * tk]
        if c + 1 < nsub:
            k_next = [key_lhs(key_block(ti * nsub + c + 1)) for ti in tile_idx]
            s_next = {dt: jnp.dot(k_next[owner[dt]], q2_ref[:, tiles[dt]],
                                  preferred_element_type=F32)
                      for dt in live if q_his[dt] >= (c + 1) * tk}
        p_t = {}
        for dt in live:
            s = s_t[dt]
            lead = q_base[dt] % ta - c * tk
            causal = lead < tk - 1 or dw > ta
            if causal or not has_prefix:
                key = (lead if causal else None, dw > ta)
                if key not in masks:
                    q_idx = (col & (ta - 1)) if dw > ta else col
                    keep = q_idx - k_idx >= -lead if causal else None
                    if not has_prefix:
                        junk = k_idx >= N_JUNK - c * tk
                        keep = junk if keep is None else jnp.logical_and(keep, junk)
                    masks[key] = keep
                s = jnp.where(masks[key], s, NEG_INF)
            m_t[dt], alpha_c, p_t[dt] = softmax_step(m_t[dt], s, slope * (-c * tk))
            acc_t[dt] = acc_t[dt] * alpha_c
        v_c = [value_lhs(vt_ref[ti * nsub + c]) for ti in tile_idx]
        for dt in live:
            acc_t[dt] = acc_t[dt] + jnp.dot(v_c[owner[dt]], p_t[dt],
                                            preferred_element_type=F32)
        if c + 1 < nsub:
            s_t = s_next
    if has_prefix:
        v_pre = value_lhs(vht_ref[...])
        no_weight = jnp.zeros((N_JUNK, dw), BF16)
        for dt in range(n_dt):
            shift = slope * (HEAD_ROWS + tile_idx[owner[dt]] * ta).astype(F32)
            m_t[dt], alpha_c, p = softmax_step(m_t[dt], s_pre[dt], shift)
            p = jnp.concatenate([no_weight, p], axis=0)
            acc_t[dt] = acc_t[dt] * alpha_c + jnp.dot(v_pre, p, preferred_element_type=F32)

    lam = (jnp.exp(jnp.sum(lq1_ref[...] * lk1_ref[...], axis=-1, keepdims=True))
           - jnp.exp(jnp.sum(lq2_ref[...] * lk2_ref[...], axis=-1, keepdims=True))
           + lam_init)
    per = width // dw
    for t, o_ref in enumerate(o_refs):
        acc = jnp.concatenate(acc_t[t * per:(t + 1) * per], axis=1)
        l = acc[V_DIM:V_DIM + 1, :]
        acc = acc[:V_DIM, :]
        o = acc[:, :ta] / l[:, :ta] - lam * (acc[:, ta:] / l[:, ta:])
        r = lax.rsqrt(jnp.mean(o * o, axis=0, keepdims=True) + EPS)
        y = ((o * r) * subg_ref[...]) * (1.0 - lam_init)
        o_ref[...] = y.T.astype(BF16)


def _attention(slopes, qt, k, vt, prefix, lq1, lk1, lq2, lk2, subg, *, ta, tk, lam_init):
    b, l, _ = k.shape
    nq = l // ta
    nkv = l // tk
    assert ta & (ta - 1) == 0 and ta % tk == 0
    has_prefix = prefix is not None
    n_tiles = 2 if nq % 2 == 0 else 1
    steps = nq // n_tiles
    kern = functools.partial(_attn_kernel, ta=ta, tk=tk, n_tiles=n_tiles,
                             has_prefix=has_prefix, lam_init=lam_init)
    in_specs = [pl.BlockSpec(memory_space=pltpu.SMEM),
                pl.BlockSpec((None, None, V_DIM, ta), lambda bi, h, g: (bi, g, h, 0))]
    args = [slopes, qt]
    if n_tiles == 2:
        in_specs.append(
            pl.BlockSpec((None, None, V_DIM, ta), lambda bi, h, g: (bi, nq - 1 - g, h, 0)))
        args.append(qt)
    in_specs += [
        pl.BlockSpec((None, l, V_DIM), lambda bi, h, g: (bi, 0, h)),
        pl.BlockSpec((None, nkv, V_DIM, tk), lambda bi, h, g: (bi, 0, h, 0)),
    ]
    args += [k, vt]
    if has_prefix:
        kh, vht = prefix
        assert tk >= HEAD_ROWS
        in_specs += [
            pl.BlockSpec((None, HEAD_ROWS, V_DIM), lambda bi, h, g: (0, 0, h)),
            pl.BlockSpec((None, None, V_DIM, HEAD_ROWS), lambda bi, h, g: (0, 0, h, 0)),
        ]
        args += [kh, vht]
    const2 = lambda bi, h, g: (0, 0)
    in_specs += [
        pl.BlockSpec((tk, V_DIM), const2),
        pl.BlockSpec((SUM_ROWS, tk), const2),
        pl.BlockSpec((1, QK_DIM), const2), pl.BlockSpec((1, QK_DIM), const2),
        pl.BlockSpec((1, QK_DIM), const2), pl.BlockSpec((1, QK_DIM), const2),
        pl.BlockSpec((V_DIM, 1), const2),
    ]
    assert tk <= 256
    kb = jnp.zeros((tk, V_DIM), F32).at[:, :3].set(jnp.arange(tk, dtype=F32)[:, None])
    ones = jnp.zeros((SUM_ROWS, tk), F32).at[0].set(1.0)
    args += [kb.astype(BF16), ones.astype(BF16), lq1, lk1, lq2, lk2, subg]
    if n_tiles == 2:
        out_specs = [pl.BlockSpec((None, ta, V_DIM), lambda bi, h, g: (bi, g, h)),
                     pl.BlockSpec((None, ta, V_DIM), lambda bi, h, g: (bi, steps - 1 - g, h))]
        out_shape = [jax.ShapeDtypeStruct((b, l // 2, ATT_WIDTH), BF16)] * 2
    else:
        out_specs = [pl.BlockSpec((None, ta, V_DIM), lambda bi, h, g: (bi, g, h))]
        out_shape = [jax.ShapeDtypeStruct((b, l, ATT_WIDTH), BF16)]
    total = n_tiles * 2 * ta
    outs = pl.pallas_call(
        kern,
        grid=(b, ATT_HEADS, steps),
        in_specs=in_specs,
        out_specs=out_specs,
        out_shape=out_shape,
        scratch_shapes=[
            pltpu.VMEM((2 * V_DIM, total), BF16),
            pltpu.VMEM((tk, total), F32),
            pltpu.VMEM((tk, total), F32),
            pltpu.VMEM((tk, total), BF16),
            pltpu.VMEM((tk, total), BF16),
            pltpu.VMEM((V_DIM + SUM_ROWS, total), F32),
        ],
        compiler_params=pltpu.CompilerParams(
            dimension_semantics=("parallel", "parallel", "arbitrary"),
            vmem_limit_bytes=VMEM_LIMIT),
        name="diff_attention",
    )(*args)
    return outs[0] if n_tiles == 1 else tuple(outs)


def _merge_kernel(x_ref, ys_ref, *rest, halves):
    ya_refs, rest = rest[:halves], rest[halves:]
    g1_ref, wg_ref, wso_ref, wao_ref, wo_ref, h1_ref = rest
    x = x_ref[...]
    hn = _rmsnorm(x, g1_ref[...]).astype(BF16)
    gates = jnp.dot(hn, wg_ref[...], preferred_element_type=F32)
    a = jnp.dot(ys_ref[...], wso_ref[...], preferred_element_type=F32)
    ya = ya_refs[0][...]
    if halves == 2:
        ya = jnp.where(pl.program_id(1) < pl.num_programs(1) // 2, ya, ya_refs[1][...])
    c = jnp.dot(ya, wao_ref[...], preferred_element_type=F32)
    mixed = (jax.nn.sigmoid(gates[:, :D_MODEL]) * a
             + jax.nn.sigmoid(gates[:, D_MODEL:]) * c)
    h1_ref[...] = x + jnp.dot(mixed.astype(BF16), wo_ref[...], preferred_element_type=F32)


def _merge(x, ys, ya, g1, wg, wso, wao, wo, *, t):
    b, l, _ = x.shape
    const2 = lambda bi, i: (0, 0)
    row = lambda bi, i: (bi, i, 0)
    if isinstance(ya, tuple):
        half = l // t // 2
        ya_specs = [
            pl.BlockSpec((None, t, ATT_WIDTH), lambda bi, i: (bi, jnp.minimum(i, half - 1), 0)),
            pl.BlockSpec((None, t, ATT_WIDTH), lambda bi, i: (bi, jnp.maximum(i - half, 0), 0))]
    else:
        ya, ya_specs = (ya,), [pl.BlockSpec((None, t, ATT_WIDTH), row)]
    return pl.pallas_call(
        functools.partial(_merge_kernel, halves=len(ya)),
        grid=(b, l // t),
        in_specs=[
            pl.BlockSpec((None, t, D_MODEL), row),
            pl.BlockSpec((None, t, SSM_WIDTH), row),
            *ya_specs,
            pl.BlockSpec((1, D_MODEL), const2),
            pl.BlockSpec((D_MODEL, 2 * D_MODEL), const2, pipeline_mode=pl.Buffered(1)),
            pl.BlockSpec((SSM_WIDTH, D_MODEL), const2, pipeline_mode=pl.Buffered(1)),
            pl.BlockSpec((ATT_WIDTH, D_MODEL), const2, pipeline_mode=pl.Buffered(1)),
            pl.BlockSpec((D_MODEL, D_MODEL), const2, pipeline_mode=pl.Buffered(1)),
        ],
        out_specs=pl.BlockSpec((None, t, D_MODEL), row),
        out_shape=jax.ShapeDtypeStruct((b, l, D_MODEL), F32),
        compiler_params=pltpu.CompilerParams(
            dimension_semantics=("parallel", "parallel"), vmem_limit_bytes=VMEM_LIMIT),
        name="merge",
    )(x, ys, *ya, g1, wg, wso, wao, wo)


def _ffn_kernel(h_ref, hprev_ref, hhead_ref, g2_ref, wa_ref, wg_ref, cw_ref, cb_ref, wdn_ref,
                out_ref, hn_ref, gate_ref):
    i = pl.program_id(1)
    t = h_ref.shape[0]
    h = h_ref[...]
    g2 = g2_ref[...]
    halo = jnp.where(i == 0, hhead_ref[...], hprev_ref[...])
    hn_ref[:HALO, :] = _rmsnorm(halo, g2).astype(BF16)
    hn_ref[HALO:, :] = _rmsnorm(h, g2).astype(BF16)
    hn = hn_ref[...]
    for f in range(N_FF_CHUNKS):
        cols = slice(f * FF_CHUNK, (f + 1) * FF_CHUNK)
        a = jnp.dot(hn, wa_ref[:, cols], preferred_element_type=F32)
        gate = jnp.dot(hn[HALO:], wg_ref[:, cols], preferred_element_type=F32)
        c = (a[HALO - 2:HALO - 2 + t] * cw_ref[0:1, cols]
             + a[HALO - 1:HALO - 1 + t] * cw_ref[1:2, cols]
             + a[HALO:] * cw_ref[2:3, cols] + cb_ref[:, cols])
        gate_ref[:, cols] = (_gelu(c) * gate).astype(BF16)
    out_ref[...] = h + jnp.dot(gate_ref[...], wdn_ref[...], preferred_element_type=F32)


def _ffn(h1, h1_head, g2, wa, wg, cw, cb, wdn, *, t):
    b, l, _ = h1.shape
    const2 = lambda bi, i: (0, 0)
    row = lambda bi, i: (bi, i, 0)
    per = t // HALO
    return pl.pallas_call(
        _ffn_kernel,
        grid=(b, l // t),
        in_specs=[
            pl.BlockSpec((None, t, D_MODEL), row),
            pl.BlockSpec((None, HALO, D_MODEL),
                         lambda bi, i: (bi, jnp.maximum(i * per - 1, 0), 0)),
            pl.BlockSpec((None, HALO, D_MODEL),
                         lambda bi, i: (0, HEAD_ROWS // HALO - 1, 0)),
            pl.BlockSpec((1, D_MODEL), const2),
            pl.BlockSpec((D_MODEL, D_FF), const2, pipeline_mode=pl.Buffered(1)),
            pl.BlockSpec((D_MODEL, D_FF), const2, pipeline_mode=pl.Buffered(1)),
            pl.BlockSpec((3, D_FF), const2),
            pl.BlockSpec((1, D_FF), const2),
            pl.BlockSpec((D_FF, D_MODEL), const2, pipeline_mode=pl.Buffered(1)),
        ],
        out_specs=pl.BlockSpec((None, t, D_MODEL), row),
        out_shape=jax.ShapeDtypeStruct((b, l, D_MODEL), F32),
        scratch_shapes=[
            pltpu.VMEM((HALO + t, D_MODEL), BF16),
            pltpu.VMEM((t, D_FF), BF16),
        ],
        compiler_params=pltpu.CompilerParams(
            dimension_semantics=("parallel", "arbitrary"), vmem_limit_bytes=VMEM_LIMIT),
        name="conv_ffn",
    )(h1, h1, h1_head, g2, wa, wg, cw, cb, wdn)


def _pick_tile(l, pref):
    t = min(pref, l)
    while l % t:
        t //= 2
    return t


def _layer(x, head, p, l_idx):
    bsz, seq, _ = x.shape
    lam_init = 0.8 - 0.6 * math.exp(-0.3 * l_idx)
    slopes = 2.0 ** (-8.0 * jnp.arange(1, ATT_HEADS + 1, dtype=F32) / ATT_HEADS)

    g1 = p['norm1_g'][l_idx][None]
    w_in = p['w_in'][l_idx]
    w_mix = w_in[:, :MIX_COLS].astype(BF16)
    w_gate = w_in[:, MIX_COLS:].astype(BF16)
    qg = jnp.tile(p['q_norm_g'][l_idx], QK_COLS // QK_DIM)[None]
    kg = jnp.tile(p['k_norm_g'][l_idx], QK_COLS // QK_DIM)[None]

    lam_re, lam_im, bbt_re, bbt_im = _ssm_prep(
        p['ssm_a_re'][l_idx], p['ssm_a_im'][l_idx], p['ssm_log_dt'][l_idx],
        p['ssm_b_re'][l_idx], p['ssm_b_im'][l_idx])
    wb = jnp.concatenate([_block_diag(bbt_re), _block_diag(bbt_im)], axis=-1).astype(BF16)
    wcr = _block_diag(jnp.swapaxes(p['ssm_c_re'][l_idx], 1, 2)).astype(BF16)
    wci = _block_diag(jnp.swapaxes(p['ssm_c_im'][l_idx], 1, 2)).astype(BF16)
    ns = SSM_GROUPS * SSM_STATE
    lamr = jnp.broadcast_to(lam_re.reshape(1, ns), (bsz, ns))
    lami = jnp.broadcast_to(lam_im.reshape(1, ns), (bsz, ns))
    d = p['ssm_d'][l_idx].reshape(1, SSM_WIDTH)
    gluw = p['ssm_glu_w'][l_idx].astype(BF16)
    glub = p['ssm_glu_b'][l_idx][None]

    lq1, lk1 = p['lam_q1'][l_idx][None], p['lam_k1'][l_idx][None]
    lq2, lk2 = p['lam_q2'][l_idx][None], p['lam_k2'][l_idx][None]
    subg = p['subln_g'][l_idx][:, None]
    wso = p['w_ssm_out'][l_idx].astype(BF16)
    wao = p['w_att_out'][l_idx].astype(BF16)
    wo = p['w_o'][l_idx].astype(BF16)

    t_row = _pick_tile(seq, 1024)
    ta = _pick_tile(seq, 512)
    tk = _pick_tile(seq, 256)
    tc = _pick_tile(seq, 64)

    u_h, qt_h, k_h, vt_h = _inproj(head, g1, w_mix, qg, kg, t=HEAD_ROWS, ta=HEAD_ROWS,
                                    tk=HEAD_ROWS)
    zeros_state = jnp.zeros((bsz, ns), F32)
    u_hb = jnp.broadcast_to(u_h, (bsz,) + u_h.shape[1:])
    ys_hb, s0r, s0i = _ssm(u_hb, zeros_state, zeros_state, wb, lamr, lami, wcr, wci,
                           d, gluw, glub, tc=_pick_tile(HEAD_ROWS, 64))
    ys_h = ys_hb[:1]
    ya_h = _attention(slopes, qt_h, k_h, vt_h, None, lq1, lk1, lq2, lk2, subg,
                      ta=HEAD_ROWS, tk=HEAD_ROWS, lam_init=lam_init)
    h1_head = _merge(head, ys_h, ya_h, g1, w_gate, wso, wao, wo, t=HEAD_ROWS)

    u, qt, k, vt = _inproj(x, g1, w_mix, qg, kg, t=t_row, ta=ta, tk=tk)
    ys, _, _ = _ssm(u, s0r, s0i, wb, lamr, lami, wcr, wci, d, gluw, glub, tc=tc)
    ya = _attention(slopes, qt, k, vt, (k_h, vt_h), lq1, lk1, lq2, lk2, subg,
                    ta=ta, tk=tk, lam_init=lam_init)
    h1 = _merge(x, ys, ya, g1, w_gate, wso, wao, wo, t=t_row)

    w_up = p['w_up'][l_idx]
    out = _ffn(h1, h1_head, p['norm2_g'][l_idx][None], w_up[:, :D_FF].astype(BF16),
               w_up[:, D_FF:].astype(BF16), p['conv_w'][l_idx],
               p['conv_b'][l_idx][None], p['w_down'][l_idx].astype(BF16), t=t_row)
    return out, h1_head


def kernel(x, meta_tokens, norm1_g, w_in, ssm_a_re, ssm_a_im, ssm_log_dt, ssm_b_re, ssm_b_im,
           ssm_c_re, ssm_c_im, ssm_d, ssm_glu_w, ssm_glu_b, q_norm_g, k_norm_g,
           lam_q1, lam_k1, lam_q2, lam_k2, subln_g, w_ssm_out, w_att_out, w_o,
           norm2_g, w_up, conv_w, conv_b, w_down):
    params = dict(norm1_g=norm1_g, w_in=w_in, ssm_a_re=ssm_a_re, ssm_a_im=ssm_a_im,
                  ssm_log_dt=ssm_log_dt, ssm_b_re=ssm_b_re, ssm_b_im=ssm_b_im,
                  ssm_c_re=ssm_c_re, ssm_c_im=ssm_c_im, ssm_d=ssm_d, ssm_glu_w=ssm_glu_w,
                  ssm_glu_b=ssm_glu_b, q_norm_g=q_norm_g, k_norm_g=k_norm_g,
                  lam_q1=lam_q1, lam_k1=lam_k1, lam_q2=lam_q2, lam_k2=lam_k2,
                  subln_g=subln_g, w_ssm_out=w_ssm_out, w_att_out=w_att_out, w_o=w_o,
                  norm2_g=norm2_g, w_up=w_up, conv_w=conv_w, conv_b=conv_b, w_down=w_down)
    depth = norm1_g.shape[0]
    assert depth == 1, "the head tile is only carried through one layer"
    head = jnp.concatenate(
        [jnp.zeros((N_JUNK, D_MODEL), x.dtype), meta_tokens.astype(x.dtype)], axis=0)[None]
    out, _ = _layer(x, head, params, 0)
    return out
```

```python
import functools
import math

import jax
import jax.numpy as jnp
from jax import lax
from jax.experimental import pallas as pl
from jax.experimental.pallas import tpu as pltpu

F32 = jnp.float32
BF16 = jnp.bfloat16

D_MODEL = 1024
N_META = 16
EPS = 1e-6
NEG_INF = -1e30

SSM_WIDTH = 512
SSM_GROUP = 16
SSM_GROUPS = 32
SSM_STATE = 64
SSM_BLOCKS = 4
SSM_BLOCK_IN = SSM_WIDTH // SSM_BLOCKS
SSM_BLOCK_STATE = SSM_GROUPS * SSM_STATE // SSM_BLOCKS

ATT_HEADS = 4
QK_DIM = 64
V_DIM = 128
ATT_WIDTH = 512
QK_COLS = 512
MIX_COLS = SSM_WIDTH + 2 * QK_COLS + ATT_WIDTH

D_FF = 2816
FF_CHUNK = 256
N_FF_CHUNKS = D_FF // FF_CHUNK

HEAD_ROWS = 128
N_JUNK = HEAD_ROWS - N_META
SUM_ROWS = 16
LOG2E = math.log2(math.e)
DIAG_TILE = 256
TILES_PER_STEP = 4
COL_TILE = 512
HALO = 16

VMEM_LIMIT = 56 * 1024 * 1024


def _rmsnorm(x, g):
    r = lax.rsqrt(jnp.mean(x * x, axis=-1, keepdims=True) + EPS)
    return (x * r) * g


def _gelu(x):
    return jax.nn.gelu(x, approximate=True)


def _ssm_prep_kernel(are_ref, aim_ref, ldt_ref, btr_ref, bti_ref,
                     lr_ref, li_ref, bbr_ref, bbi_ref):
    a_re = are_ref[...]
    a_im = aim_ref[...]
    dt = jnp.exp(ldt_ref[...])
    mag = jnp.exp(a_re * dt)
    lb_re = mag * jnp.cos(a_im * dt)
    lb_im = mag * jnp.sin(a_im * dt)
    den = a_re * a_re + a_im * a_im
    n_re = lb_re - 1.0
    f_re = (n_re * a_re + lb_im * a_im) / den
    f_im = (lb_im * a_re - n_re * a_im) / den
    lr_ref[...] = lb_re
    li_ref[...] = lb_im
    bt_re = btr_ref[...]
    bt_im = bti_ref[...]
    bbr_ref[...] = f_re[:, None, :] * bt_re - f_im[:, None, :] * bt_im
    bbi_ref[...] = f_re[:, None, :] * bt_im + f_im[:, None, :] * bt_re


def _ssm_prep(a_re, a_im, log_dt, b_re, b_im):
    g, p, c = b_re.shape
    return pl.pallas_call(
        _ssm_prep_kernel,
        out_shape=(jax.ShapeDtypeStruct((g, p), F32), jax.ShapeDtypeStruct((g, p), F32),
                   jax.ShapeDtypeStruct((g, c, p), F32), jax.ShapeDtypeStruct((g, c, p), F32)),
        name="ssm_prep",
    )(a_re, a_im, log_dt.reshape(g, 1), jnp.swapaxes(b_re, 1, 2), jnp.swapaxes(b_im, 1, 2))


def _block_diag(m):
    gpb = SSM_GROUPS // SSM_BLOCKS
    _, r, c = m.shape
    m4 = m.reshape(SSM_BLOCKS, gpb, r, c)
    eye = jnp.eye(gpb, dtype=m.dtype)
    return jnp.einsum('jgrc,gh->jgrhc', m4, eye).reshape(SSM_BLOCKS, gpb * r, gpb * c)


def _qk_norm(q, g):
    lo = lax.broadcasted_iota(jnp.int32, (1, 128), 1) < QK_DIM
    outs = []
    for c in range(QK_COLS // 128):
        blk = q[:, c * 128:(c + 1) * 128]
        sq = blk * blk
        s_lo = jnp.sum(jnp.where(lo, sq, 0.0), axis=-1, keepdims=True)
        s_hi = jnp.sum(jnp.where(lo, 0.0, sq), axis=-1, keepdims=True)
        r_lo = lax.rsqrt(s_lo / QK_DIM + EPS)
        r_hi = lax.rsqrt(s_hi / QK_DIM + EPS)
        outs.append(blk * jnp.where(lo, r_lo, r_hi))
    return jnp.concatenate(outs, axis=-1) * g


def _inproj_kernel(x_ref, g1_ref, w_ref, qg_ref, kg_ref,
                   u_ref, qt_ref, k_ref, vt_ref, *, ta, tk):
    x = x_ref[...]
    hn = _rmsnorm(x, g1_ref[...]).astype(BF16)

    def proj(c0, width):
        return jnp.dot(hn, w_ref[:, c0:c0 + width], preferred_element_type=F32)

    q = proj(SSM_WIDTH, QK_COLS)
    k = proj(SSM_WIDTH + QK_COLS, QK_COLS)
    qn = _qk_norm(q, qg_ref[...]) * (QK_DIM ** -0.5 * LOG2E)
    for c in range(x.shape[0] // ta):
        qt_ref[c] = qn[c * ta:(c + 1) * ta, :].T.astype(BF16)
    v = proj(SSM_WIDTH + 2 * QK_COLS, ATT_WIDTH)
    k_ref[...] = _qk_norm(k, kg_ref[...]).astype(BF16)
    u_ref[...] = proj(0, SSM_WIDTH)
    for c in range(x.shape[0] // tk):
        vt_ref[c] = v[c * tk:(c + 1) * tk, :].T.astype(BF16)


def _inproj(x, g1, w_mix, qg, kg, *, t, ta, tk):
    b, l, _ = x.shape
    nt = l // t
    kern = functools.partial(_inproj_kernel, ta=ta, tk=tk)
    return pl.pallas_call(
        kern,
        grid=(b, nt),
        in_specs=[
            pl.BlockSpec((None, t, D_MODEL), lambda bi, i: (bi, i, 0)),
            pl.BlockSpec((1, D_MODEL), lambda bi, i: (0, 0)),
            pl.BlockSpec((D_MODEL, MIX_COLS), lambda bi, i: (0, 0)),
            pl.BlockSpec((1, QK_COLS), lambda bi, i: (0, 0)),
            pl.BlockSpec((1, QK_COLS), lambda bi, i: (0, 0)),
        ],
        out_specs=[
            pl.BlockSpec((None, t, SSM_WIDTH), lambda bi, i: (bi, i, 0)),
            pl.BlockSpec((None, t // ta, QK_COLS, ta), lambda bi, i: (bi, i, 0, 0)),
            pl.BlockSpec((None, t, QK_COLS), lambda bi, i: (bi, i, 0)),
            pl.BlockSpec((None, t // tk, ATT_WIDTH, tk), lambda bi, i: (bi, i, 0, 0)),
        ],
        out_shape=(
            jax.ShapeDtypeStruct((b, l, SSM_WIDTH), F32),
            jax.ShapeDtypeStruct((b, l // ta, QK_COLS, ta), BF16),
            jax.ShapeDtypeStruct((b, l, QK_COLS), BF16),
            jax.ShapeDtypeStruct((b, l // tk, ATT_WIDTH, tk), BF16),
        ),
        compiler_params=pltpu.CompilerParams(
            dimension_semantics=("parallel", "parallel"), vmem_limit_bytes=VMEM_LIMIT),
        name="inproj",
    )(x, g1, w_mix, qg, kg)


def _ssm_kernel(u_ref, s0r_ref, s0i_ref, wb_ref, lamr_ref, lami_ref, wcr_ref, wci_ref,
                d_ref, gluw_ref, glub_ref,
                y_ref, sfr_ref, sfi_ref,
                xr_ref, xi_ref, sr_ref, si_ref, *, tc, nb):
    t = pl.program_id(0)

    @pl.when(t == 0)
    def _():
        sr_ref[...] = s0r_ref[...]
        si_ref[...] = s0i_ref[...]

    u = jnp.swapaxes(u_ref[...], 0, 1).reshape(tc * nb, SSM_WIDTH)
    ub = u.astype(BF16)

    def project_in(j):
        x = jnp.dot(ub[:, j * SSM_BLOCK_IN:(j + 1) * SSM_BLOCK_IN], wb_ref[j],
                    preferred_element_type=F32)
        xr_ref[j] = x[:, :SSM_BLOCK_STATE]
        xi_ref[j] = x[:, SSM_BLOCK_STATE:]

    def scan(j):
        cols = slice(j * SSM_BLOCK_STATE, (j + 1) * SSM_BLOCK_STATE)
        lr = lamr_ref[:, cols]
        li = lami_ref[:, cols]
        s_re = sr_ref[:, cols]
        s_im = si_ref[:, cols]
        for tt in range(tc):
            rows = slice(tt * nb, (tt + 1) * nb)
            s_re, s_im = (lr * s_re - li * s_im + xr_ref[j, rows, :],
                          lr * s_im + li * s_re + xi_ref[j, rows, :])
            xr_ref[j, rows, :] = s_re
            xi_ref[j, rows, :] = s_im
        sr_ref[:, cols] = s_re
        si_ref[:, cols] = s_im

    def project_out(j):
        return (jnp.dot(xr_ref[j].astype(BF16), wcr_ref[j], preferred_element_type=F32)
                - jnp.dot(xi_ref[j].astype(BF16), wci_ref[j], preferred_element_type=F32))

    ys = []
    project_in(0)
    for j in range(SSM_BLOCKS):
        if j + 1 < SSM_BLOCKS:
            project_in(j + 1)
        scan(j)
        ys.append(project_out(j))
    y = jnp.concatenate(ys, axis=-1) + d_ref[...] * u
    y = _gelu(y)
    z = jnp.dot(y.astype(BF16), gluw_ref[...], preferred_element_type=F32) + glub_ref[...]
    out = (y * jax.nn.sigmoid(z)).reshape(tc, nb, SSM_WIDTH)
    y_ref[...] = jnp.swapaxes(out, 0, 1).astype(BF16)

    @pl.when(t == pl.num_programs(0) - 1)
    def _():
        sfr_ref[...] = sr_ref[...]
        sfi_ref[...] = si_ref[...]


def _ssm(u, s0r, s0i, wb, lamr, lami, wcr, wci, d, gluw, glub, *, tc):
    nb, seq, _ = u.shape
    r = tc * nb
    nsteps = seq // tc
    ns = SSM_GROUPS * SSM_STATE
    kern = functools.partial(_ssm_kernel, tc=tc, nb=nb)
    const2 = lambda t: (0, 0)
    const3 = lambda t: (0, 0, 0)
    return pl.pallas_call(
        kern,
        grid=(nsteps,),
        in_specs=[
            pl.BlockSpec((nb, tc, SSM_WIDTH), lambda t: (0, t, 0)),
            pl.BlockSpec((nb, ns), const2),
            pl.BlockSpec((nb, ns), const2),
            pl.BlockSpec((SSM_BLOCKS, SSM_BLOCK_IN, 2 * SSM_BLOCK_STATE), const3),
            pl.BlockSpec((nb, ns), const2),
            pl.BlockSpec((nb, ns), const2),
            pl.BlockSpec((SSM_BLOCKS, SSM_BLOCK_STATE, SSM_BLOCK_IN), const3),
            pl.BlockSpec((SSM_BLOCKS, SSM_BLOCK_STATE, SSM_BLOCK_IN), const3),
            pl.BlockSpec((1, SSM_WIDTH), const2),
            pl.BlockSpec((SSM_WIDTH, SSM_WIDTH), const2),
            pl.BlockSpec((1, SSM_WIDTH), const2),
        ],
        out_specs=[
            pl.BlockSpec((nb, tc, SSM_WIDTH), lambda t: (0, t, 0)),
            pl.BlockSpec((nb, ns), const2),
            pl.BlockSpec((nb, ns), const2),
        ],
        out_shape=(
            jax.ShapeDtypeStruct((nb, seq, SSM_WIDTH), BF16),
            jax.ShapeDtypeStruct((nb, ns), F32),
            jax.ShapeDtypeStruct((nb, ns), F32),
        ),
        scratch_shapes=[
            pltpu.VMEM((SSM_BLOCKS, r, SSM_BLOCK_STATE), F32),
            pltpu.VMEM((SSM_BLOCKS, r, SSM_BLOCK_STATE), F32),
            pltpu.VMEM((nb, ns), F32),
            pltpu.VMEM((nb, ns), F32),
        ],
        compiler_params=pltpu.CompilerParams(
            dimension_semantics=("arbitrary",), vmem_limit_bytes=VMEM_LIMIT),
        name="ssm",
    )(u, s0r, s0i, wb, lamr, lami, wcr, wci, d, gluw, glub)


def _attn_kernel(slopes_ref, *rest, ta, tk, n_tiles, has_prefix, lam_init):
    qt_ref, k_ref, vt_ref = rest[:3]
    rest = rest[3:]
    if has_prefix:
        kh_ref, vht_ref = rest[:2]
        rest = rest[2:]
    (kb_ref, ones_ref, lq1_ref, lk1_ref, lq2_ref, lk2_ref, subg_ref,
     o_ref, q2_ref, sa_ref, sb_ref, pa_ref, pb_ref, acc_ref) = rest

    h = pl.program_id(1)
    g = pl.program_id(2)
    slope = slopes_ref[h]
    nsub = ta // tk
    nq = k_ref.shape[0] // ta
    tile_idx = [g + t * (nq // n_tiles) for t in range(n_tiles)]
    width = 2 * ta
    total = n_tiles * width

    for t in range(n_tiles):
        qt = qt_ref[t]
        first = lax.broadcasted_iota(jnp.int32, qt.shape, 0) < QK_DIM
        zero = jnp.zeros_like(qt)
        q2_ref[:V_DIM, t * width:t * width + ta] = jnp.where(first, qt, zero)
        q2_ref[:V_DIM, t * width + ta:(t + 1) * width] = jnp.where(first, zero, qt)
    slope2 = jnp.full((1, total), slope * LOG2E, F32)
    c_hi = slope2.astype(BF16).astype(F32)
    c_mid = (slope2 - c_hi).astype(BF16).astype(F32)
    c_lo = slope2 - c_hi - c_mid
    bias_row = lax.broadcasted_iota(jnp.int32, (V_DIM, total), 0)
    q2_ref[V_DIM:, :] = jnp.where(
        bias_row == 0, c_hi,
        jnp.where(bias_row == 1, c_mid, jnp.where(bias_row == 2, c_lo, 0.0))).astype(BF16)
    slope = slope * LOG2E

    def key_lhs(k_blk):
        return jnp.concatenate([k_blk, kb_ref[:k_blk.shape[0], :]], axis=1)

    def value_lhs(vt_blk):
        return jnp.concatenate([vt_blk, ones_ref[:, :vt_blk.shape[1]]], axis=0)

    def key_block(kidx):
        return k_ref[pl.ds(pl.multiple_of(kidx * tk, tk), tk), :]

    def softmax_step(m_old, s, shift):
        m_new = jnp.maximum(m_old, jnp.max(s, axis=0, keepdims=True) - shift)
        alpha = jnp.exp2(m_old - m_new)
        p = jnp.exp2(s - (m_new + shift))
        return m_new, alpha, p.astype(BF16)

    sa_ref[...] = jnp.dot(key_lhs(key_block(0)), q2_ref[...], preferred_element_type=F32)
    pb_ref[...] = jnp.zeros(pb_ref.shape, BF16)
    acc_ref[...] = jnp.zeros(acc_ref.shape, F32)
    m = jnp.full((1, total), NEG_INF, F32)
    alpha = jnp.ones((1, total), F32)

    def make_body(col_tiles):
        def body(j, carry):
            m, alpha_p = carry
            n0 = 2 * j
            k_lhs = [key_lhs(key_block(n0 + st + 1)) for st in range(2)]
            v_lhs = [value_lhs(vt_ref[jnp.maximum(n0 - 1, 0)]), value_lhs(vt_ref[n0])]
            s_bufs = (sa_ref, sb_ref)
            p_bufs = (pa_ref, pb_ref)
            new_m, new_alpha = {}, {}
            for ct in col_tiles:
                cs = slice(ct * COL_TILE, (ct + 1) * COL_TILE)
                q0 = tile_idx[ct * COL_TILE // width] * ta
                shift0 = slope * (q0 - n0 * tk).astype(F32)
                m_c, alpha_c, acc = m[:, cs], alpha_p[:, cs], acc_ref[:, cs]
                for st in range(2):
                    s_bufs[(st + 1) % 2][:, cs] = jnp.dot(
                        k_lhs[st], q2_ref[:, cs], preferred_element_type=F32)
                    m_c, alpha_n, p = softmax_step(
                        m_c, s_bufs[st % 2][:, cs], shift0 - slope * (st * tk))
                    p_bufs[st % 2][:, cs] = p
                    acc = acc * alpha_c + jnp.dot(
                        v_lhs[st], p_bufs[(st + 1) % 2][:, cs], preferred_element_type=F32)
                    alpha_c = alpha_n
                acc_ref[:, cs] = acc
                new_m[ct], new_alpha[ct] = m_c, alpha_c
            n_ct = total // COL_TILE
            pick = lambda new, old: jnp.concatenate(
                [new[ct] if ct in new else old[:, ct * COL_TILE:(ct + 1) * COL_TILE]
                 for ct in range(n_ct)], axis=1)
            return pick(new_m, m), pick(new_alpha, alpha_p)
        return body

    if nsub == 2:
        per_tile = width // COL_TILE
        all_ct = list(range(total // COL_TILE))
        for t in range(n_tiles):
            lo = tile_idx[t - 1] if t else 0
            m, alpha = lax.fori_loop(lo, tile_idx[t], make_body(all_ct[t * per_tile:]),
                                     (m, alpha))
    else:
        assert nsub == 1 and n_tiles == 1 and nq == 1

    dw = min(DIAG_TILE, width)
    n_dt = total // dw
    tiles = [slice(dt * dw, (dt + 1) * dw) for dt in range(n_dt)]
    owner = [dt * dw // width for dt in range(n_dt)]
    q_base = [dt * dw % width for dt in range(n_dt)]
    q_his = [min(q_base[dt] % ta + dw, ta) - 1 for dt in range(n_dt)]
    k_idx = lax.broadcasted_iota(jnp.int32, (tk, dw), 0)
    col = lax.broadcasted_iota(jnp.int32, (tk, dw), 1)
    if has_prefix:
        k_pre = key_lhs(kh_ref[...])[N_JUNK:, :]
        s_pre = [jnp.dot(k_pre, q2_ref[:, cs], preferred_element_type=F32) for cs in tiles]
    v_pend = [value_lhs(vt_ref[jnp.maximum(ti * nsub - 1, 0)]) for ti in tile_idx]
    m_t = [m[:, cs] for cs in tiles]
    acc_t = [acc_ref[:, cs] * alpha[:, cs]
             + jnp.dot(v_pend[owner[dt]], pb_ref[:, cs], preferred_element_type=F32)
             for dt, cs in enumerate(tiles)]
    s_t = {dt: sa_ref[:, cs] for dt, cs in enumerate(tiles)}
    masks = {}
    for c in range(nsub):
        live = [dt for dt in range(n_dt) if q_his[dt] >= c * tk]
        if c + 1 < nsub:
            k_next = [key_lhs(key_block(ti * nsub + c + 1)) for ti in tile_idx]
            s_next = {dt: jnp.dot(k_next[owner[dt]], q2_ref[:, tiles[dt]],
                                  preferred_element_type=F32)
                      for dt in live if q_his[dt] >= (c + 1) * tk}
        p_t = {}
        for dt in live:
            s = s_t[dt]
            lead = q_base[dt] % ta - c * tk
            causal = lead < tk - 1 or dw > ta
            if causal or not has_prefix:
                key = (lead if causal else None, dw > ta)
                if key not in masks:
                    q_idx = (col & (ta - 1)) if dw > ta else col
                    keep = q_idx - k_idx >= -lead if causal else None
                    if not has_prefix:
                        junk = k_idx >= N_JUNK - c * tk
                        keep = junk if keep is None else jnp.logical_and(keep, junk)
                    masks[key] = keep
                s = jnp.where(masks[key], s, NEG_INF)
            m_t[dt], alpha_c, p_t[dt] = softmax_step(m_t[dt], s, slope * (-c * tk))
            acc_t[dt] = acc_t[dt] * alpha_c
        v_c = [value_lhs(vt_ref[ti * nsub + c]) for ti in tile_idx]
        for dt in live:
            acc_t[dt] = acc_t[dt] + jnp.dot(v_c[owner[dt]], p_t[dt],
                                            preferred_element_type=F32)
        if c + 1 < nsub:
            s_t = s_next
    if has_prefix:
        v_pre = value_lhs(vht_ref[...])
        no_weight = jnp.zeros((N_JUNK, dw), BF16)
        for dt in range(n_dt):
            shift = slope * (HEAD_ROWS + tile_idx[owner[dt]] * ta).astype(F32)
            m_t[dt], alpha_c, p = softmax_step(m_t[dt], s_pre[dt], shift)
            p = jnp.concatenate([no_weight, p], axis=0)
            acc_t[dt] = acc_t[dt] * alpha_c + jnp.dot(v_pre, p, preferred_element_type=F32)

    lam = (jnp.exp(jnp.sum(lq1_ref[...] * lk1_ref[...], axis=-1, keepdims=True))
           - jnp.exp(jnp.sum(lq2_ref[...] * lk2_ref[...], axis=-1, keepdims=True))
           + lam_init)
    per = width // dw
    for t in range(n_tiles):
        acc = jnp.concatenate(acc_t[t * per:(t + 1) * per], axis=1)
        l = acc[V_DIM:V_DIM + 1, :]
        acc = acc[:V_DIM, :]
        o = acc[:, :ta] / l[:, :ta] - lam * (acc[:, ta:] / l[:, ta:])
        r = lax.rsqrt(jnp.mean(o * o, axis=0, keepdims=True) + EPS)
        y = ((o * r) * subg_ref[...]) * (1.0 - lam_init)
        o_ref[t] = y.T.astype(BF16)


def _attention(slopes, qt, k, vt, prefix, lq1, lk1, lq2, lk2, subg, *, ta, tk, lam_init):
    b, l, _ = k.shape
    nq = l // ta
    nkv = l // tk
    assert ta & (ta - 1) == 0 and ta % tk == 0
    has_prefix = prefix is not None
    n_tiles = math.gcd(nq, TILES_PER_STEP)
    steps = nq // n_tiles
    kern = functools.partial(_attn_kernel, ta=ta, tk=tk, n_tiles=n_tiles,
                             has_prefix=has_prefix, lam_init=lam_init)
    qt = qt.reshape(b, n_tiles, steps, QK_COLS, ta)
    in_specs = [
        pl.BlockSpec(memory_space=pltpu.SMEM),
        pl.BlockSpec((None, n_tiles, None, V_DIM, ta), lambda bi, h, g: (bi, 0, g, h, 0)),
        pl.BlockSpec((None, l, V_DIM), lambda bi, h, g: (bi, 0, h)),
        pl.BlockSpec((None, nkv, V_DIM, tk), lambda bi, h, g: (bi, 0, h, 0)),
    ]
    args = [slopes, qt, k, vt]
    if has_prefix:
        kh, vht = prefix
        assert tk >= HEAD_ROWS
        in_specs += [
            pl.BlockSpec((None, HEAD_ROWS, V_DIM), lambda bi, h, g: (0, 0, h)),
            pl.BlockSpec((None, None, V_DIM, HEAD_ROWS), lambda bi, h, g: (0, 0, h, 0)),
        ]
        args += [kh, vht]
    const2 = lambda bi, h, g: (0, 0)
    in_specs += [
        pl.BlockSpec((tk, V_DIM), const2),
        pl.BlockSpec((SUM_ROWS, tk), const2),
        pl.BlockSpec((1, QK_DIM), const2), pl.BlockSpec((1, QK_DIM), const2),
        pl.BlockSpec((1, QK_DIM), const2), pl.BlockSpec((1, QK_DIM), const2),
        pl.BlockSpec((V_DIM, 1), const2),
    ]
    assert tk <= 256
    kb = jnp.zeros((tk, V_DIM), F32).at[:, :3].set(jnp.arange(tk, dtype=F32)[:, None])
    ones = jnp.zeros((SUM_ROWS, tk), F32).at[0].set(1.0)
    args += [kb.astype(BF16), ones.astype(BF16), lq1, lk1, lq2, lk2, subg]
    total = n_tiles * 2 * ta
    out = pl.pallas_call(
        kern,
        grid=(b, ATT_HEADS, steps),
        in_specs=in_specs,
        out_specs=pl.BlockSpec((None, n_tiles, ta, V_DIM), lambda bi, h, g: (bi, 0, g, h)),
        out_shape=jax.ShapeDtypeStruct((b, n_tiles, l // n_tiles, ATT_WIDTH), BF16),
        scratch_shapes=[
            pltpu.VMEM((2 * V_DIM, total), BF16),
            pltpu.VMEM((tk, total), F32),
            pltpu.VMEM((tk, total), F32),
            pltpu.VMEM((tk, total), BF16),
            pltpu.VMEM((tk, total), BF16),
            pltpu.VMEM((V_DIM + SUM_ROWS, total), F32),
        ],
        compiler_params=pltpu.CompilerParams(
            dimension_semantics=("parallel", "parallel", "arbitrary"),
            vmem_limit_bytes=VMEM_LIMIT),
        name="diff_attention",
    )(*args)
    return out.reshape(b, l, ATT_WIDTH)


def _merge_kernel(x_ref, ys_ref, ya_ref, g1_ref, wg_ref, wso_ref, wao_ref, wo_ref, h1_ref):
    x = x_ref[...]
    hn = _rmsnorm(x, g1_ref[...]).astype(BF16)
    gates = jnp.dot(hn, wg_ref[...], preferred_element_type=F32)
    a = jnp.dot(ys_ref[...], wso_ref[...], preferred_element_type=F32)
    c = jnp.dot(ya_ref[...], wao_ref[...], preferred_element_type=F32)
    mixed = (jax.nn.sigmoid(gates[:, :D_MODEL]) * a
             + jax.nn.sigmoid(gates[:, D_MODEL:]) * c)
    h1_ref[...] = x + jnp.dot(mixed.astype(BF16), wo_ref[...], preferred_element_type=F32)


def _merge(x, ys, ya, g1, wg, wso, wao, wo, *, t):
    b, l, _ = x.shape
    const2 = lambda bi, i: (0, 0)
    row = lambda bi, i: (bi, i, 0)
    return pl.pallas_call(
        _merge_kernel,
        grid=(b, l // t),
        in_specs=[
            pl.BlockSpec((None, t, D_MODEL), row),
            pl.BlockSpec((None, t, SSM_WIDTH), row),
            pl.BlockSpec((None, t, ATT_WIDTH), row),
            pl.BlockSpec((1, D_MODEL), const2),
            pl.BlockSpec((D_MODEL, 2 * D_MODEL), const2, pipeline_mode=pl.Buffered(1)),
            pl.BlockSpec((SSM_WIDTH, D_MODEL), const2, pipeline_mode=pl.Buffered(1)),
            pl.BlockSpec((ATT_WIDTH, D_MODEL), const2, pipeline_mode=pl.Buffered(1)),
            pl.BlockSpec((D_MODEL, D_MODEL), const2, pipeline_mode=pl.Buffered(1)),
        ],
        out_specs=pl.BlockSpec((None, t, D_MODEL), row),
        out_shape=jax.ShapeDtypeStruct((b, l, D_MODEL), F32),
        compiler_params=pltpu.CompilerParams(
            dimension_semantics=("parallel", "parallel"), vmem_limit_bytes=VMEM_LIMIT),
        name="merge",
    )(x, ys, ya, g1, wg, wso, wao, wo)


def _ffn_kernel(h_ref, hprev_ref, hhead_ref, g2_ref, wa_ref, wg_ref, cw_ref, cb_ref, wdn_ref,
                out_ref, hn_ref, gate_ref):
    i = pl.program_id(1)
    t = h_ref.shape[0]
    h = h_ref[...]
    g2 = g2_ref[...]
    halo = jnp.where(i == 0, hhead_ref[...], hprev_ref[...])
    hn_ref[:HALO, :] = _rmsnorm(halo, g2).astype(BF16)
    hn_ref[HALO:, :] = _rmsnorm(h, g2).astype(BF16)
    hn = hn_ref[...]
    for f in range(N_FF_CHUNKS):
        cols = slice(f * FF_CHUNK, (f + 1) * FF_CHUNK)
        a = jnp.dot(hn, wa_ref[:, cols], preferred_element_type=F32)
        gate = jnp.dot(hn[HALO:], wg_ref[:, cols], preferred_element_type=F32)
        c = (a[HALO - 2:HALO - 2 + t] * cw_ref[0:1, cols]
             + a[HALO - 1:HALO - 1 + t] * cw_ref[1:2, cols]
             + a[HALO:] * cw_ref[2:3, cols] + cb_ref[:, cols])
        gate_ref[:, cols] = (_gelu(c) * gate).astype(BF16)
    out_ref[...] = h + jnp.dot(gate_ref[...], wdn_ref[...], preferred_element_type=F32)


def _ffn(h1, h1_head, g2, wa, wg, cw, cb, wdn, *, t):
    b, l, _ = h1.shape
    const2 = lambda bi, i: (0, 0)
    row = lambda bi, i: (bi, i, 0)
    per = t // HALO
    return pl.pallas_call(
        _ffn_kernel,
        grid=(b, l // t),
        in_specs=[
            pl.BlockSpec((None, t, D_MODEL), row),
            pl.BlockSpec((None, HALO, D_MODEL),
                         lambda bi, i: (bi, jnp.maximum(i * per - 1, 0), 0)),
            pl.BlockSpec((None, HALO, D_MODEL),
                         lambda bi, i: (0, HEAD_ROWS // HALO - 1, 0)),
            pl.BlockSpec((1, D_MODEL), const2),
            pl.BlockSpec((D_MODEL, D_FF), const2, pipeline_mode=pl.Buffered(1)),
            pl.BlockSpec((D_MODEL, D_FF), const2, pipeline_mode=pl.Buffered(1)),
            pl.BlockSpec((3, D_FF), const2),
            pl.BlockSpec((1, D_FF), const2),
            pl.BlockSpec((D_FF, D_MODEL), const2, pipeline_mode=pl.Buffered(1)),
        ],
        out_specs=pl.BlockSpec((None, t, D_MODEL), row),
        out_shape=jax.ShapeDtypeStruct((b, l, D_MODEL), F32),
        scratch_shapes=[
            pltpu.VMEM((HALO + t, D_MODEL), BF16),
            pltpu.VMEM((t, D_FF), BF16),
        ],
        compiler_params=pltpu.CompilerParams(
            dimension_semantics=("parallel", "arbitrary"), vmem_limit_bytes=VMEM_LIMIT),
        name="conv_ffn",
    )(h1, h1, h1_head, g2, wa, wg, cw, cb, wdn)


def _pick_tile(l, pref):
    t = min(pref, l)
    while l % t:
        t //= 2
    return t


def _layer(x, head, p, l_idx):
    bsz, seq, _ = x.shape
    lam_init = 0.8 - 0.6 * math.exp(-0.3 * l_idx)
    slopes = 2.0 ** (-8.0 * jnp.arange(1, ATT_HEADS + 1, dtype=F32) / ATT_HEADS)

    g1 = p['norm1_g'][l_idx][None]
    w_in = p['w_in'][l_idx]
    w_mix = w_in[:, :MIX_COLS].astype(BF16)
    w_gate = w_in[:, MIX_COLS:].astype(BF16)
    qg = jnp.tile(p['q_norm_g'][l_idx], QK_COLS // QK_DIM)[None]
    kg = jnp.tile(p['k_norm_g'][l_idx], QK_COLS // QK_DIM)[None]

    lam_re, lam_im, bbt_re, bbt_im = _ssm_prep(
        p['ssm_a_re'][l_idx], p['ssm_a_im'][l_idx], p['ssm_log_dt'][l_idx],
        p['ssm_b_re'][l_idx], p['ssm_b_im'][l_idx])
    wb = jnp.concatenate([_block_diag(bbt_re), _block_diag(bbt_im)], axis=-1).astype(BF16)
    wcr = _block_diag(jnp.swapaxes(p['ssm_c_re'][l_idx], 1, 2)).astype(BF16)
    wci = _block_diag(jnp.swapaxes(p['ssm_c_im'][l_idx], 1, 2)).astype(BF16)
    ns = SSM_GROUPS * SSM_STATE
    lamr = jnp.broadcast_to(lam_re.reshape(1, ns), (bsz, ns))
    lami = jnp.broadcast_to(lam_im.reshape(1, ns), (bsz, ns))
    d = p['ssm_d'][l_idx].reshape(1, SSM_WIDTH)
    gluw = p['ssm_glu_w'][l_idx].astype(BF16)
    glub = p['ssm_glu_b'][l_idx][None]

    lq1, lk1 = p['lam_q1'][l_idx][None], p['lam_k1'][l_idx][None]
    lq2, lk2 = p['lam_q2'][l_idx][None], p['lam_k2'][l_idx][None]
    subg = p['subln_g'][l_idx][:, None]
    wso = p['w_ssm_out'][l_idx].astype(BF16)
    wao = p['w_att_out'][l_idx].astype(BF16)
    wo = p['w_o'][l_idx].astype(BF16)

    t_row = _pick_tile(seq, 1024)
    ta = _pick_tile(seq, 512)
    tk = _pick_tile(seq, 256)
    tc = _pick_tile(seq, 64)

    u_h, qt_h, k_h, vt_h = _inproj(head, g1, w_mix, qg, kg, t=HEAD_ROWS, ta=HEAD_ROWS,
                                    tk=HEAD_ROWS)
    zeros_state = jnp.zeros((bsz, ns), F32)
    u_hb = jnp.broadcast_to(u_h, (bsz,) + u_h.shape[1:])
    ys_hb, s0r, s0i = _ssm(u_hb, zeros_state, zeros_state, wb, lamr, lami, wcr, wci,
                           d, gluw, glub, tc=_pick_tile(HEAD_ROWS, 64))
    ys_h = ys_hb[:1]
    ya_h = _attention(slopes, qt_h, k_h, vt_h, None, lq1, lk1, lq2, lk2, subg,
                      ta=HEAD_ROWS, tk=HEAD_ROWS, lam_init=lam_init)
    h1_head = _merge(head, ys_h, ya_h, g1, w_gate, wso, wao, wo, t=HEAD_ROWS)

    u, qt, k, vt = _inproj(x, g1, w_mix, qg, kg, t=t_row, ta=ta, tk=tk)
    ys, _, _ = _ssm(u, s0r, s0i, wb, lamr, lami, wcr, wci, d, gluw, glub, tc=tc)
    ya = _attention(slopes, qt, k, vt, (k_h, vt_h), lq1, lk1, lq2, lk2, subg,
                    ta=ta, tk=tk, lam_init=lam_init)
    h1 = _merge(x, ys, ya, g1, w_gate, wso, wao, wo, t=t_row)

    w_up = p['w_up'][l_idx]
    out = _ffn(h1, h1_head, p['norm2_g'][l_idx][None], w_up[:, :D_FF].astype(BF16),
               w_up[:, D_FF:].astype(BF16), p['conv_w'][l_idx],
               p['conv_b'][l_idx][None], p['w_down'][l_idx].astype(BF16), t=t_row)
    return out, h1_head


def kernel(x, meta_tokens, norm1_g, w_in, ssm_a_re, ssm_a_im, ssm_log_dt, ssm_b_re, ssm_b_im,
           ssm_c_re, ssm_c_im, ssm_d, ssm_glu_w, ssm_glu_b, q_norm_g, k_norm_g,
           lam_q1, lam_k1, lam_q2, lam_k2, subln_g, w_ssm_out, w_att_out, w_o,
           norm2_g, w_up, conv_w, conv_b, w_down):
    params = dict(norm1_g=norm1_g, w_in=w_in, ssm_a_re=ssm_a_re, ssm_a_im=ssm_a_im,
                  ssm_log_dt=ssm_log_dt, ssm_b_re=ssm_b_re, ssm_b_im=ssm_b_im,
                  ssm_c_re=ssm_c_re, ssm_c_im=ssm_c_im, ssm_d=ssm_d, ssm_glu_w=ssm_glu_w,
                  ssm_glu_b=ssm_glu_b, q_norm_g=q_norm_g, k_norm_g=k_norm_g,
                  lam_q1=lam_q1, lam_k1=lam_k1, lam_q2=lam_q2, lam_k2=lam_k2,
                  subln_g=subln_g, w_ssm_out=w_ssm_out, w_att_out=w_att_out, w_o=w_o,
                  norm2_g=norm2_g, w_up=w_up, conv_w=conv_w, conv_b=conv_b, w_down=w_down)
    depth = norm1_g.shape[0]
    assert depth == 1, "the head tile is only carried through one layer"
    head = jnp.concatenate(
        [jnp.zeros((N_JUNK, D_MODEL), x.dtype), meta_tokens.astype(x.dtype)], axis=0)[None]
    out, _ = _layer(x, head, params, 0)
    return out
```

```python
import functools
import math

import jax
import jax.numpy as jnp
from jax import lax
from jax.experimental import pallas as pl
from jax.experimental.pallas import tpu as pltpu

F32 = jnp.float32
BF16 = jnp.bfloat16

D_MODEL = 1024
N_META = 16
EPS = 1e-6
NEG_INF = -1e30

SSM_WIDTH = 512
SSM_GROUP = 16
SSM_GROUPS = 32
SSM_STATE = 64
SSM_BLOCKS = 4
SSM_BLOCK_IN = SSM_WIDTH // SSM_BLOCKS
SSM_BLOCK_STATE = SSM_GROUPS * SSM_STATE // SSM_BLOCKS

ATT_HEADS = 4
QK_DIM = 64
V_DIM = 128
ATT_WIDTH = 512
QK_COLS = 512
MIX_COLS = SSM_WIDTH + 2 * QK_COLS + ATT_WIDTH

D_FF = 2816
FF_CHUNK = 256
N_FF_CHUNKS = D_FF // FF_CHUNK

HEAD_ROWS = 128
N_JUNK = HEAD_ROWS - N_META
SUM_ROWS = 16
LOG2E = math.log2(math.e)
DIAG_TILE = 256
TILES_PER_STEP = 8
COL_TILE = 512
HALO = 16

VMEM_LIMIT = 56 * 1024 * 1024


def _rmsnorm(x, g):
    r = lax.rsqrt(jnp.mean(x * x, axis=-1, keepdims=True) + EPS)
    return (x * r) * g


def _gelu(x):
    return jax.nn.gelu(x, approximate=True)


def _ssm_prep_kernel(are_ref, aim_ref, ldt_ref, btr_ref, bti_ref,
                     lr_ref, li_ref, bbr_ref, bbi_ref):
    a_re = are_ref[...]
    a_im = aim_ref[...]
    dt = jnp.exp(ldt_ref[...])
    mag = jnp.exp(a_re * dt)
    lb_re = mag * jnp.cos(a_im * dt)
    lb_im = mag * jnp.sin(a_im * dt)
    den = a_re * a_re + a_im * a_im
    n_re = lb_re - 1.0
    f_re = (n_re * a_re + lb_im * a_im) / den
    f_im = (lb_im * a_re - n_re * a_im) / den
    lr_ref[...] = lb_re
    li_ref[...] = lb_im
    bt_re = btr_ref[...]
    bt_im = bti_ref[...]
    bbr_ref[...] = f_re[:, None, :] * bt_re - f_im[:, None, :] * bt_im
    bbi_ref[...] = f_re[:, None, :] * bt_im + f_im[:, None, :] * bt_re


def _ssm_prep(a_re, a_im, log_dt, b_re, b_im):
    g, p, c = b_re.shape
    return pl.pallas_call(
        _ssm_prep_kernel,
        out_shape=(jax.ShapeDtypeStruct((g, p), F32), jax.ShapeDtypeStruct((g, p), F32),
                   jax.ShapeDtypeStruct((g, c, p), F32), jax.ShapeDtypeStruct((g, c, p), F32)),
        name="ssm_prep",
    )(a_re, a_im, log_dt.reshape(g, 1), jnp.swapaxes(b_re, 1, 2), jnp.swapaxes(b_im, 1, 2))


def _block_diag(m):
    gpb = SSM_GROUPS // SSM_BLOCKS
    _, r, c = m.shape
    m4 = m.reshape(SSM_BLOCKS, gpb, r, c)
    eye = jnp.eye(gpb, dtype=m.dtype)
    return jnp.einsum('jgrc,gh->jgrhc', m4, eye).reshape(SSM_BLOCKS, gpb * r, gpb * c)


def _qk_norm(q, g):
    lo = lax.broadcasted_iota(jnp.int32, (1, 128), 1) < QK_DIM
    outs = []
    for c in range(QK_COLS // 128):
        blk = q[:, c * 128:(c + 1) * 128]
        sq = blk * blk
        s_lo = jnp.sum(jnp.where(lo, sq, 0.0), axis=-1, keepdims=True)
        s_hi = jnp.sum(jnp.where(lo, 0.0, sq), axis=-1, keepdims=True)
        r_lo = lax.rsqrt(s_lo / QK_DIM + EPS)
        r_hi = lax.rsqrt(s_hi / QK_DIM + EPS)
        outs.append(blk * jnp.where(lo, r_lo, r_hi))
    return jnp.concatenate(outs, axis=-1) * g


def _inproj_kernel(x_ref, g1_ref, w_ref, qg_ref, kg_ref,
                   u_ref, qt_ref, k_ref, vt_ref, *, ta, tk):
    x = x_ref[...]
    hn = _rmsnorm(x, g1_ref[...]).astype(BF16)

    def proj(c0, width):
        return jnp.dot(hn, w_ref[:, c0:c0 + width], preferred_element_type=F32)

    q = proj(SSM_WIDTH, QK_COLS)
    k = proj(SSM_WIDTH + QK_COLS, QK_COLS)
    qn = _qk_norm(q, qg_ref[...]) * (QK_DIM ** -0.5 * LOG2E)
    for c in range(x.shape[0] // ta):
        qt_ref[c] = qn[c * ta:(c + 1) * ta, :].T.astype(BF16)
    v = proj(SSM_WIDTH + 2 * QK_COLS, ATT_WIDTH)
    k_ref[...] = _qk_norm(k, kg_ref[...]).astype(BF16)
    u_ref[...] = proj(0, SSM_WIDTH)
    for c in range(x.shape[0] // tk):
        vt_ref[c] = v[c * tk:(c + 1) * tk, :].T.astype(BF16)


def _inproj(x, g1, w_mix, qg, kg, *, t, ta, tk):
    b, l, _ = x.shape
    nt = l // t
    kern = functools.partial(_inproj_kernel, ta=ta, tk=tk)
    return pl.pallas_call(
        kern,
        grid=(b, nt),
        in_specs=[
            pl.BlockSpec((None, t, D_MODEL), lambda bi, i: (bi, i, 0)),
            pl.BlockSpec((1, D_MODEL), lambda bi, i: (0, 0)),
            pl.BlockSpec((D_MODEL, MIX_COLS), lambda bi, i: (0, 0)),
            pl.BlockSpec((1, QK_COLS), lambda bi, i: (0, 0)),
            pl.BlockSpec((1, QK_COLS), lambda bi, i: (0, 0)),
        ],
        out_specs=[
            pl.BlockSpec((None, t, SSM_WIDTH), lambda bi, i: (bi, i, 0)),
            pl.BlockSpec((None, t // ta, QK_COLS, ta), lambda bi, i: (bi, i, 0, 0)),
            pl.BlockSpec((None, t, QK_COLS), lambda bi, i: (bi, i, 0)),
            pl.BlockSpec((None, t // tk, ATT_WIDTH, tk), lambda bi, i: (bi, i, 0, 0)),
        ],
        out_shape=(
            jax.ShapeDtypeStruct((b, l, SSM_WIDTH), F32),
            jax.ShapeDtypeStruct((b, l // ta, QK_COLS, ta), BF16),
            jax.ShapeDtypeStruct((b, l, QK_COLS), BF16),
            jax.ShapeDtypeStruct((b, l // tk, ATT_WIDTH, tk), BF16),
        ),
        compiler_params=pltpu.CompilerParams(
            dimension_semantics=("parallel", "parallel"), vmem_limit_bytes=VMEM_LIMIT),
        name="inproj",
    )(x, g1, w_mix, qg, kg)


def _ssm_kernel(u_ref, s0r_ref, s0i_ref, wb_ref, lamr_ref, lami_ref, wcr_ref, wci_ref,
                d_ref, gluw_ref, glub_ref,
                y_ref, sfr_ref, sfi_ref,
                xr_ref, xi_ref, sr_ref, si_ref, *, tc, nb):
    t = pl.program_id(0)

    @pl.when(t == 0)
    def _():
        sr_ref[...] = s0r_ref[...]
        si_ref[...] = s0i_ref[...]

    u = jnp.swapaxes(u_ref[...], 0, 1).reshape(tc * nb, SSM_WIDTH)
    ub = u.astype(BF16)

    def project_in(j):
        x = jnp.dot(ub[:, j * SSM_BLOCK_IN:(j + 1) * SSM_BLOCK_IN], wb_ref[j],
                    preferred_element_type=F32)
        xr_ref[j] = x[:, :SSM_BLOCK_STATE]
        xi_ref[j] = x[:, SSM_BLOCK_STATE:]

    def scan(j):
        cols = slice(j * SSM_BLOCK_STATE, (j + 1) * SSM_BLOCK_STATE)
        lr = lamr_ref[:, cols]
        li = lami_ref[:, cols]
        s_re = sr_ref[:, cols]
        s_im = si_ref[:, cols]
        for tt in range(tc):
            rows = slice(tt * nb, (tt + 1) * nb)
            s_re, s_im = (lr * s_re - li * s_im + xr_ref[j, rows, :],
                          lr * s_im + li * s_re + xi_ref[j, rows, :])
            xr_ref[j, rows, :] = s_re
            xi_ref[j, rows, :] = s_im
        sr_ref[:, cols] = s_re
        si_ref[:, cols] = s_im

    def project_out(j):
        return (jnp.dot(xr_ref[j].astype(BF16), wcr_ref[j], preferred_element_type=F32)
                - jnp.dot(xi_ref[j].astype(BF16), wci_ref[j], preferred_element_type=F32))

    ys = []
    project_in(0)
    for j in range(SSM_BLOCKS):
        if j + 1 < SSM_BLOCKS:
            project_in(j + 1)
        scan(j)
        ys.append(project_out(j))
    y = jnp.concatenate(ys, axis=-1) + d_ref[...] * u
    y = _gelu(y)
    z = jnp.dot(y.astype(BF16), gluw_ref[...], preferred_element_type=F32) + glub_ref[...]
    out = (y * jax.nn.sigmoid(z)).reshape(tc, nb, SSM_WIDTH)
    y_ref[...] = jnp.swapaxes(out, 0, 1).astype(BF16)

    @pl.when(t == pl.num_programs(0) - 1)
    def _():
        sfr_ref[...] = sr_ref[...]
        sfi_ref[...] = si_ref[...]


def _ssm(u, s0r, s0i, wb, lamr, lami, wcr, wci, d, gluw, glub, *, tc):
    nb, seq, _ = u.shape
    r = tc * nb
    nsteps = seq // tc
    ns = SSM_GROUPS * SSM_STATE
    kern = functools.partial(_ssm_kernel, tc=tc, nb=nb)
    const2 = lambda t: (0, 0)
    const3 = lambda t: (0, 0, 0)
    return pl.pallas_call(
        kern,
        grid=(nsteps,),
        in_specs=[
            pl.BlockSpec((nb, tc, SSM_WIDTH), lambda t: (0, t, 0)),
            pl.BlockSpec((nb, ns), const2),
            pl.BlockSpec((nb, ns), const2),
            pl.BlockSpec((SSM_BLOCKS, SSM_BLOCK_IN, 2 * SSM_BLOCK_STATE), const3),
            pl.BlockSpec((nb, ns), const2),
            pl.BlockSpec((nb, ns), const2),
            pl.BlockSpec((SSM_BLOCKS, SSM_BLOCK_STATE, SSM_BLOCK_IN), const3),
            pl.BlockSpec((SSM_BLOCKS, SSM_BLOCK_STATE, SSM_BLOCK_IN), const3),
            pl.BlockSpec((1, SSM_WIDTH), const2),
            pl.BlockSpec((SSM_WIDTH, SSM_WIDTH), const2),
            pl.BlockSpec((1, SSM_WIDTH), const2),
        ],
        out_specs=[
            pl.BlockSpec((nb, tc, SSM_WIDTH), lambda t: (0, t, 0)),
            pl.BlockSpec((nb, ns), const2),
            pl.BlockSpec((nb, ns), const2),
        ],
        out_shape=(
            jax.ShapeDtypeStruct((nb, seq, SSM_WIDTH), BF16),
            jax.ShapeDtypeStruct((nb, ns), F32),
            jax.ShapeDtypeStruct((nb, ns), F32),
        ),
        scratch_shapes=[
            pltpu.VMEM((SSM_BLOCKS, r, SSM_BLOCK_STATE), F32),
            pltpu.VMEM((SSM_BLOCKS, r, SSM_BLOCK_STATE), F32),
            pltpu.VMEM((nb, ns), F32),
            pltpu.VMEM((nb, ns), F32),
        ],
        compiler_params=pltpu.CompilerParams(
            dimension_semantics=("arbitrary",), vmem_limit_bytes=VMEM_LIMIT),
        name="ssm",
    )(u, s0r, s0i, wb, lamr, lami, wcr, wci, d, gluw, glub)


def _attn_kernel(slopes_ref, *rest, ta, tk, n_tiles, has_prefix, lam_init):
    qt_ref, k_ref, vt_ref = rest[:3]
    rest = rest[3:]
    if has_prefix:
        kh_ref, vht_ref = rest[:2]
        rest = rest[2:]
    (kb_ref, ones_ref, lq1_ref, lk1_ref, lq2_ref, lk2_ref, subg_ref,
     o_ref, q2_ref, sa_ref, sb_ref, pa_ref, pb_ref, acc_ref, m_ref, alpha_ref) = rest

    h = pl.program_id(1)
    g = pl.program_id(2)
    slope = slopes_ref[h]
    nsub = ta // tk
    nq = k_ref.shape[0] // ta
    tile_idx = [g + t * (nq // n_tiles) for t in range(n_tiles)]
    width = 2 * ta
    total = n_tiles * width

    for t in range(n_tiles):
        qt = qt_ref[t]
        first = lax.broadcasted_iota(jnp.int32, qt.shape, 0) < QK_DIM
        zero = jnp.zeros_like(qt)
        q2_ref[:V_DIM, t * width:t * width + ta] = jnp.where(first, qt, zero)
        q2_ref[:V_DIM, t * width + ta:(t + 1) * width] = jnp.where(first, zero, qt)
    slope2 = jnp.full((1, total), slope * LOG2E, F32)
    c_hi = slope2.astype(BF16).astype(F32)
    c_mid = (slope2 - c_hi).astype(BF16).astype(F32)
    c_lo = slope2 - c_hi - c_mid
    bias_row = lax.broadcasted_iota(jnp.int32, (V_DIM, total), 0)
    q2_ref[V_DIM:, :] = jnp.where(
        bias_row == 0, c_hi,
        jnp.where(bias_row == 1, c_mid, jnp.where(bias_row == 2, c_lo, 0.0))).astype(BF16)
    slope = slope * LOG2E

    def key_lhs(k_blk):
        return jnp.concatenate([k_blk, kb_ref[:k_blk.shape[0], :]], axis=1)

    def value_lhs(vt_blk):
        return jnp.concatenate([vt_blk, ones_ref[:, :vt_blk.shape[1]]], axis=0)

    def key_block(kidx):
        return k_ref[pl.ds(pl.multiple_of(kidx * tk, tk), tk), :]

    def softmax_step(m_old, s, shift):
        m_new = jnp.maximum(m_old, jnp.max(s, axis=0, keepdims=True) - shift)
        alpha = jnp.exp2(m_old - m_new)
        p = jnp.exp2(s - (m_new + shift))
        return m_new, alpha, p.astype(BF16)

    sa_ref[...] = jnp.dot(key_lhs(key_block(0)), q2_ref[...], preferred_element_type=F32)
    pb_ref[...] = jnp.zeros(pb_ref.shape, BF16)
    acc_ref[...] = jnp.zeros(acc_ref.shape, F32)
    m_ref[...] = jnp.full((1, total), NEG_INF, F32)
    alpha_ref[...] = jnp.ones((1, total), F32)

    def make_body(col_tiles):
        def body(j, carry):
            n0 = 2 * j
            k_lhs = [key_lhs(key_block(n0 + st + 1)) for st in range(2)]
            v_lhs = [value_lhs(vt_ref[jnp.maximum(n0 - 1, 0)]), value_lhs(vt_ref[n0])]
            s_bufs = (sa_ref, sb_ref)
            p_bufs = (pa_ref, pb_ref)
            for ct in col_tiles:
                cs = slice(ct * COL_TILE, (ct + 1) * COL_TILE)
                q0 = tile_idx[ct * COL_TILE // width] * ta
                shift0 = slope * (q0 - n0 * tk).astype(F32)
                m_c, alpha_c, acc = m_ref[:, cs], alpha_ref[:, cs], acc_ref[:, cs]
                for st in range(2):
                    s_bufs[(st + 1) % 2][:, cs] = jnp.dot(
                        k_lhs[st], q2_ref[:, cs], preferred_element_type=F32)
                    m_c, alpha_n, p = softmax_step(
                        m_c, s_bufs[st % 2][:, cs], shift0 - slope * (st * tk))
                    p_bufs[st % 2][:, cs] = p
                    acc = acc * alpha_c + jnp.dot(
                        v_lhs[st], p_bufs[(st + 1) % 2][:, cs], preferred_element_type=F32)
                    alpha_c = alpha_n
                acc_ref[:, cs] = acc
                m_ref[:, cs] = m_c
                alpha_ref[:, cs] = alpha_c
            return carry
        return body

    if nsub == 2:
        per_tile = width // COL_TILE
        all_ct = list(range(total // COL_TILE))
        for t in range(n_tiles):
            lo = tile_idx[t - 1] if t else 0
            lax.fori_loop(lo, tile_idx[t], make_body(all_ct[t * per_tile:]), 0)
    else:
        assert nsub == 1 and n_tiles == 1 and nq == 1

    dw = min(DIAG_TILE, width)
    n_dt = total // dw
    tiles = [slice(dt * dw, (dt + 1) * dw) for dt in range(n_dt)]
    owner = [dt * dw // width for dt in range(n_dt)]
    q_base = [dt * dw % width for dt in range(n_dt)]
    q_his = [min(q_base[dt] % ta + dw, ta) - 1 for dt in range(n_dt)]
    k_idx = lax.broadcasted_iota(jnp.int32, (tk, dw), 0)
    col = lax.broadcasted_iota(jnp.int32, (tk, dw), 1)
    if has_prefix:
        k_pre = key_lhs(kh_ref[...])[N_JUNK:, :]
        s_pre = [jnp.dot(k_pre, q2_ref[:, cs], preferred_element_type=F32) for cs in tiles]
    v_pend = [value_lhs(vt_ref[jnp.maximum(ti * nsub - 1, 0)]) for ti in tile_idx]
    m_t = [m_ref[:, cs] for cs in tiles]
    acc_t = [acc_ref[:, cs] * alpha_ref[:, cs]
             + jnp.dot(v_pend[owner[dt]], pb_ref[:, cs], preferred_element_type=F32)
             for dt, cs in enumerate(tiles)]
    s_t = {dt: sa_ref[:, cs] for dt, cs in enumerate(tiles)}
    masks = {}
    for c in range(nsub):
        live = [dt for dt in range(n_dt) if q_his[dt] >= c * tk]
        if c + 1 < nsub:
            k_next = [key_lhs(key_block(ti * nsub + c + 1)) for ti in tile_idx]
            s_next = {dt: jnp.dot(k_next[owner[dt]], q2_ref[:, tiles[dt]],
                                  preferred_element_type=F32)
                      for dt in live if q_his[dt] >= (c + 1) * tk}
        p_t = {}
        for dt in live:
            s = s_t[dt]
            lead = q_base[dt] % ta - c * tk
            causal = lead < tk - 1 or dw > ta
            if causal or not has_prefix:
                key = (lead if causal else None, dw > ta)
                if key not in masks:
                    q_idx = (col & (ta - 1)) if dw > ta else col
                    keep = q_idx - k_idx >= -lead if causal else None
                    if not has_prefix:
                        junk = k_idx >= N_JUNK - c * tk
                        keep = junk if keep is None else jnp.logical_and(keep, junk)
                    masks[key] = keep
                s = jnp.where(masks[key], s, NEG_INF)
            m_t[dt], alpha_c, p_t[dt] = softmax_step(m_t[dt], s, slope * (-c * tk))
            acc_t[dt] = acc_t[dt] * alpha_c
        v_c = [value_lhs(vt_ref[ti * nsub + c]) for ti in tile_idx]
        for dt in live:
            acc_t[dt] = acc_t[dt] + jnp.dot(v_c[owner[dt]], p_t[dt],
                                            preferred_element_type=F32)
        if c + 1 < nsub:
            s_t = s_next
    if has_prefix:
        v_pre = value_lhs(vht_ref[...])
        no_weight = jnp.zeros((N_JUNK, dw), BF16)
        for dt in range(n_dt):
            shift = slope * (HEAD_ROWS + tile_idx[owner[dt]] * ta).astype(F32)
            m_t[dt], alpha_c, p = softmax_step(m_t[dt], s_pre[dt], shift)
            p = jnp.concatenate([no_weight, p], axis=0)
            acc_t[dt] = acc_t[dt] * alpha_c + jnp.dot(v_pre, p, preferred_element_type=F32)

    lam = (jnp.exp(jnp.sum(lq1_ref[...] * lk1_ref[...], axis=-1, keepdims=True))
           - jnp.exp(jnp.sum(lq2_ref[...] * lk2_ref[...], axis=-1, keepdims=True))
           + lam_init)
    per = width // dw
    for t in range(n_tiles):
        acc = jnp.concatenate(acc_t[t * per:(t + 1) * per], axis=1)
        l = acc[V_DIM:V_DIM + 1, :]
        acc = acc[:V_DIM, :]
        o = acc[:, :ta] / l[:, :ta] - lam * (acc[:, ta:] / l[:, ta:])
        r = lax.rsqrt(jnp.mean(o * o, axis=0, keepdims=True) + EPS)
        y = ((o * r) * subg_ref[...]) * (1.0 - lam_init)
        o_ref[t] = y.T.astype(BF16)


def _attention(slopes, qt, k, vt, prefix, lq1, lk1, lq2, lk2, subg, *, ta, tk, lam_init):
    b, l, _ = k.shape
    nq = l // ta
    nkv = l // tk
    assert ta & (ta - 1) == 0 and ta % tk == 0
    has_prefix = prefix is not None
    n_tiles = math.gcd(nq, TILES_PER_STEP)
    steps = nq // n_tiles
    kern = functools.partial(_attn_kernel, ta=ta, tk=tk, n_tiles=n_tiles,
                             has_prefix=has_prefix, lam_init=lam_init)
    qt = qt.reshape(b, n_tiles, steps, QK_COLS, ta)
    in_specs = [
        pl.BlockSpec(memory_space=pltpu.SMEM),
        pl.BlockSpec((None, n_tiles, None, V_DIM, ta), lambda bi, h, g: (bi, 0, g, h, 0)),
        pl.BlockSpec((None, l, V_DIM), lambda bi, h, g: (bi, 0, h)),
        pl.BlockSpec((None, nkv, V_DIM, tk), lambda bi, h, g: (bi, 0, h, 0)),
    ]
    args = [slopes, qt, k, vt]
    if has_prefix:
        kh, vht = prefix
        assert tk >= HEAD_ROWS
        in_specs += [
            pl.BlockSpec((None, HEAD_ROWS, V_DIM), lambda bi, h, g: (0, 0, h)),
            pl.BlockSpec((None, None, V_DIM, HEAD_ROWS), lambda bi, h, g: (0, 0, h, 0)),
        ]
        args += [kh, vht]
    const2 = lambda bi, h, g: (0, 0)
    in_specs += [
        pl.BlockSpec((tk, V_DIM), const2),
        pl.BlockSpec((SUM_ROWS, tk), const2),
        pl.BlockSpec((1, QK_DIM), const2), pl.BlockSpec((1, QK_DIM), const2),
        pl.BlockSpec((1, QK_DIM), const2), pl.BlockSpec((1, QK_DIM), const2),
        pl.BlockSpec((V_DIM, 1), const2),
    ]
    assert tk <= 256
    kb = jnp.zeros((tk, V_DIM), F32).at[:, :3].set(jnp.arange(tk, dtype=F32)[:, None])
    ones = jnp.zeros((SUM_ROWS, tk), F32).at[0].set(1.0)
    args += [kb.astype(BF16), ones.astype(BF16), lq1, lk1, lq2, lk2, subg]
    total = n_tiles * 2 * ta
    out = pl.pallas_call(
        kern,
        grid=(b, ATT_HEADS, steps),
        in_specs=in_specs,
        out_specs=pl.BlockSpec((None, n_tiles, ta, V_DIM), lambda bi, h, g: (bi, 0, g, h)),
        out_shape=jax.ShapeDtypeStruct((b, n_tiles, l // n_tiles, ATT_WIDTH), BF16),
        scratch_shapes=[
            pltpu.VMEM((2 * V_DIM, total), BF16),
            pltpu.VMEM((tk, total), F32),
            pltpu.VMEM((tk, total), F32),
            pltpu.VMEM((tk, total), BF16),
            pltpu.VMEM((tk, total), BF16),
            pltpu.VMEM((V_DIM + SUM_ROWS, total), F32),
            pltpu.VMEM((1, total), F32),
            pltpu.VMEM((1, total), F32),
        ],
        compiler_params=pltpu.CompilerParams(
            dimension_semantics=("parallel", "parallel", "arbitrary"),
            vmem_limit_bytes=VMEM_LIMIT),
        name="diff_attention",
    )(*args)
    return out.reshape(b, l, ATT_WIDTH)


def _merge_kernel(x_ref, ys_ref, ya_ref, g1_ref, wg_ref, wso_ref, wao_ref, wo_ref, h1_ref):
    x = x_ref[...]
    hn = _rmsnorm(x, g1_ref[...]).astype(BF16)
    gates = jnp.dot(hn, wg_ref[...], preferred_element_type=F32)
    a = jnp.dot(ys_ref[...], wso_ref[...], preferred_element_type=F32)
    c = jnp.dot(ya_ref[...], wao_ref[...], preferred_element_type=F32)
    mixed = (jax.nn.sigmoid(gates[:, :D_MODEL]) * a
             + jax.nn.sigmoid(gates[:, D_MODEL:]) * c)
    h1_ref[...] = x + jnp.dot(mixed.astype(BF16), wo_ref[...], preferred_element_type=F32)


def _merge(x, ys, ya, g1, wg, wso, wao, wo, *, t):
    b, l, _ = x.shape
    const2 = lambda bi, i: (0, 0)
    row = lambda bi, i: (bi, i, 0)
    return pl.pallas_call(
        _merge_kernel,
        grid=(b, l // t),
        in_specs=[
            pl.BlockSpec((None, t, D_MODEL), row),
            pl.BlockSpec((None, t, SSM_WIDTH), row),
            pl.BlockSpec((None, t, ATT_WIDTH), row),
            pl.BlockSpec((1, D_MODEL), const2),
            pl.BlockSpec((D_MODEL, 2 * D_MODEL), const2, pipeline_mode=pl.Buffered(1)),
            pl.BlockSpec((SSM_WIDTH, D_MODEL), const2, pipeline_mode=pl.Buffered(1)),
            pl.BlockSpec((ATT_WIDTH, D_MODEL), const2, pipeline_mode=pl.Buffered(1)),
            pl.BlockSpec((D_MODEL, D_MODEL), const2, pipeline_mode=pl.Buffered(1)),
        ],
        out_specs=pl.BlockSpec((None, t, D_MODEL), row),
        out_shape=jax.ShapeDtypeStruct((b, l, D_MODEL), F32),
        compiler_params=pltpu.CompilerParams(
            dimension_semantics=("parallel", "parallel"), vmem_limit_bytes=VMEM_LIMIT),
        name="merge",
    )(x, ys, ya, g1, wg, wso, wao, wo)


def _ffn_kernel(h_ref, hprev_ref, hhead_ref, g2_ref, wa_ref, wg_ref, cw_ref, cb_ref, wdn_ref,
                out_ref, hn_ref, gate_ref):
    i = pl.program_id(1)
    t = h_ref.shape[0]
    h = h_ref[...]
    g2 = g2_ref[...]
    halo = jnp.where(i == 0, hhead_ref[...], hprev_ref[...])
    hn_ref[:HALO, :] = _rmsnorm(halo, g2).astype(BF16)
    hn_ref[HALO:, :] = _rmsnorm(h, g2).astype(BF16)
    hn = hn_ref[...]
    for f in range(N_FF_CHUNKS):
        cols = slice(f * FF_CHUNK, (f + 1) * FF_CHUNK)
        a = jnp.dot(hn, wa_ref[:, cols], preferred_element_type=F32)
        gate = jnp.dot(hn[HALO:], wg_ref[:, cols], preferred_element_type=F32)
        c = (a[HALO - 2:HALO - 2 + t] * cw_ref[0:1, cols]
             + a[HALO - 1:HALO - 1 + t] * cw_ref[1:2, cols]
             + a[HALO:] * cw_ref[2:3, cols] + cb_ref[:, cols])
        gate_ref[:, cols] = (_gelu(c) * gate).astype(BF16)
    out_ref[...] = h + jnp.dot(gate_ref[...], wdn_ref[...], preferred_element_type=F32)


def _ffn(h1, h1_head, g2, wa, wg, cw, cb, wdn, *, t):
    b, l, _ = h1.shape
    const2 = lambda bi, i: (0, 0)
    row = lambda bi, i: (bi, i, 0)
    per = t // HALO
    return pl.pallas_call(
        _ffn_kernel,
        grid=(b, l // t),
        in_specs=[
            pl.BlockSpec((None, t, D_MODEL), row),
            pl.BlockSpec((None, HALO, D_MODEL),
                         lambda bi, i: (bi, jnp.maximum(i * per - 1, 0), 0)),
            pl.BlockSpec((None, HALO, D_MODEL),
                         lambda bi, i: (0, HEAD_ROWS // HALO - 1, 0)),
            pl.BlockSpec((1, D_MODEL), const2),
            pl.BlockSpec((D_MODEL, D_FF), const2, pipeline_mode=pl.Buffered(1)),
            pl.BlockSpec((D_MODEL, D_FF), const2, pipeline_mode=pl.Buffered(1)),
            pl.BlockSpec((3, D_FF), const2),
            pl.BlockSpec((1, D_FF), const2),
            pl.BlockSpec((D_FF, D_MODEL), const2, pipeline_mode=pl.Buffered(1)),
        ],
        out_specs=pl.BlockSpec((None, t, D_MODEL), row),
        out_shape=jax.ShapeDtypeStruct((b, l, D_MODEL), F32),
        scratch_shapes=[
            pltpu.VMEM((HALO + t, D_MODEL), BF16),
            pltpu.VMEM((t, D_FF), BF16),
        ],
        compiler_params=pltpu.CompilerParams(
            dimension_semantics=("parallel", "arbitrary"), vmem_limit_bytes=VMEM_LIMIT),
        name="conv_ffn",
    )(h1, h1, h1_head, g2, wa, wg, cw, cb, wdn)


def _pick_tile(l, pref):
    t = min(pref, l)
    while l % t:
        t //= 2
    return t


def _layer(x, head, p, l_idx):
    bsz, seq, _ = x.shape
    lam_init = 0.8 - 0.6 * math.exp(-0.3 * l_idx)
    slopes = 2.0 ** (-8.0 * jnp.arange(1, ATT_HEADS + 1, dtype=F32) / ATT_HEADS)

    g1 = p['norm1_g'][l_idx][None]
    w_in = p['w_in'][l_idx]
    w_mix = w_in[:, :MIX_COLS].astype(BF16)
    w_gate = w_in[:, MIX_COLS:].astype(BF16)
    qg = jnp.tile(p['q_norm_g'][l_idx], QK_COLS // QK_DIM)[None]
    kg = jnp.tile(p['k_norm_g'][l_idx], QK_COLS // QK_DIM)[None]

    lam_re, lam_im, bbt_re, bbt_im = _ssm_prep(
        p['ssm_a_re'][l_idx], p['ssm_a_im'][l_idx], p['ssm_log_dt'][l_idx],
        p['ssm_b_re'][l_idx], p['ssm_b_im'][l_idx])
    wb = jnp.concatenate([_block_diag(bbt_re), _block_diag(bbt_im)], axis=-1).astype(BF16)
    wcr = _block_diag(jnp.swapaxes(p['ssm_c_re'][l_idx], 1, 2)).astype(BF16)
    wci = _block_diag(jnp.swapaxes(p['ssm_c_im'][l_idx], 1, 2)).astype(BF16)
    ns = SSM_GROUPS * SSM_STATE
    lamr = jnp.broadcast_to(lam_re.reshape(1, ns), (bsz, ns))
    lami = jnp.broadcast_to(lam_im.reshape(1, ns), (bsz, ns))
    d = p['ssm_d'][l_idx].reshape(1, SSM_WIDTH)
    gluw = p['ssm_glu_w'][l_idx].astype(BF16)
    glub = p['ssm_glu_b'][l_idx][None]

    lq1, lk1 = p['lam_q1'][l_idx][None], p['lam_k1'][l_idx][None]
    lq2, lk2 = p['lam_q2'][l_idx][None], p['lam_k2'][l_idx][None]
    subg = p['subln_g'][l_idx][:, None]
    wso = p['w_ssm_out'][l_idx].astype(BF16)
    wao = p['w_att_out'][l_idx].astype(BF16)
    wo = p['w_o'][l_idx].astype(BF16)

    t_row = _pick_tile(seq, 1024)
    ta = _pick_tile(seq, 512)
    tk = _pick_tile(seq, 256)
    tc = _pick_tile(seq, 64)

    u_h, qt_h, k_h, vt_h = _inproj(head, g1, w_mix, qg, kg, t=HEAD_ROWS, ta=HEAD_ROWS,
                                    tk=HEAD_ROWS)
    zeros_state = jnp.zeros((bsz, ns), F32)
    u_hb = jnp.broadcast_to(u_h, (bsz,) + u_h.shape[1:])
    ys_hb, s0r, s0i = _ssm(u_hb, zeros_state, zeros_state, wb, lamr, lami, wcr, wci,
                           d, gluw, glub, tc=_pick_tile(HEAD_ROWS, 64))
    ys_h = ys_hb[:1]
    ya_h = _attention(slopes, qt_h, k_h, vt_h, None, lq1, lk1, lq2, lk2, subg,
                      ta=HEAD_ROWS, tk=HEAD_ROWS, lam_init=lam_init)
    h1_head = _merge(head, ys_h, ya_h, g1, w_gate, wso, wao, wo, t=HEAD_ROWS)

    u, qt, k, vt = _inproj(x, g1, w_mix, qg, kg, t=t_row, ta=ta, tk=tk)
    ys, _, _ = _ssm(u, s0r, s0i, wb, lamr, lami, wcr, wci, d, gluw, glub, tc=tc)
    ya = _attention(slopes, qt, k, vt, (k_h, vt_h), lq1, lk1, lq2, lk2, subg,
                    ta=ta, tk=tk, lam_init=lam_init)
    h1 = _merge(x, ys, ya, g1, w_gate, wso, wao, wo, t=t_row)

    w_up = p['w_up'][l_idx]
    out = _ffn(h1, h1_head, p['norm2_g'][l_idx][None], w_up[:, :D_FF].astype(BF16),
               w_up[:, D_FF:].astype(BF16), p['conv_w'][l_idx],
               p['conv_b'][l_idx][None], p['w_down'][l_idx].astype(BF16), t=t_row)
    return out, h1_head


def kernel(x, meta_tokens, norm1_g, w_in, ssm_a_re, ssm_a_im, ssm_log_dt, ssm_b_re, ssm_b_im,
           ssm_c_re, ssm_c_im, ssm_d, ssm_glu_w, ssm_glu_b, q_norm_g, k_norm_g,
           lam_q1, lam_k1, lam_q2, lam_k2, subln_g, w_ssm_out, w_att_out, w_o,
           norm2_g, w_up, conv_w, conv_b, w_down):
    params = dict(norm1_g=norm1_g, w_in=w_in, ssm_a_re=ssm_a_re, ssm_a_im=ssm_a_im,
                  ssm_log_dt=ssm_log_dt, ssm_b_re=ssm_b_re, ssm_b_im=ssm_b_im,
                  ssm_c_re=ssm_c_re, ssm_c_im=ssm_c_im, ssm_d=ssm_d, ssm_glu_w=ssm_glu_w,
                  ssm_glu_b=ssm_glu_b, q_norm_g=q_norm_g, k_norm_g=k_norm_g,
                  lam_q1=lam_q1, lam_k1=lam_k1, lam_q2=lam_q2, lam_k2=lam_k2,
                  subln_g=subln_g, w_ssm_out=w_ssm_out, w_att_out=w_att_out, w_o=w_o,
                  norm2_g=norm2_g, w_up=w_up, conv_w=conv_w, conv_b=conv_b, w_down=w_down)
    depth = norm1_g.shape[0]
    assert depth == 1, "the head tile is only carried through one layer"
    head = jnp.concatenate(
        [jnp.zeros((N_JUNK, D_MODEL), x.dtype), meta_tokens.astype(x.dtype)], axis=0)[None]
    out, _ = _layer(x, head, params, 0)
    return out
```

```python
import functools
import math

import jax
import jax.numpy as jnp
from jax import lax
from jax.experimental import pallas as pl
from jax.experimental.pallas import tpu as pltpu

F32 = jnp.float32
BF16 = jnp.bfloat16

D_MODEL = 1024
N_META = 16
EPS = 1e-6
NEG_INF = -1e30

SSM_WIDTH = 512
SSM_GROUP = 16
SSM_GROUPS = 32
SSM_STATE = 64
SSM_BLOCKS = 4
SSM_BLOCK_IN = SSM_WIDTH // SSM_BLOCKS
SSM_BLOCK_STATE = SSM_GROUPS * SSM_STATE // SSM_BLOCKS

ATT_HEADS = 4
QK_DIM = 64
V_DIM = 128
ATT_WIDTH = 512
QK_COLS = 512
MIX_COLS = SSM_WIDTH + 2 * QK_COLS + ATT_WIDTH

D_FF = 2816
FF_CHUNK = 256
N_FF_CHUNKS = D_FF // FF_CHUNK

HEAD_ROWS = 128
N_JUNK = HEAD_ROWS - N_META
SUM_ROWS = 16
LOG2E = math.log2(math.e)
DIAG_TILE = 256
TILES_PER_STEP = 8
COL_TILE = 512
HALO = 16

VMEM_LIMIT = 56 * 1024 * 1024


def _rmsnorm(x, g):
    r = lax.rsqrt(jnp.mean(x * x, axis=-1, keepdims=True) + EPS)
    return (x * r) * g


def _gelu(x):
    return jax.nn.gelu(x, approximate=True)


def _ssm_prep_kernel(are_ref, aim_ref, ldt_ref, btr_ref, bti_ref,
                     lr_ref, li_ref, bbr_ref, bbi_ref):
    a_re = are_ref[...]
    a_im = aim_ref[...]
    dt = jnp.exp(ldt_ref[...])
    mag = jnp.exp(a_re * dt)
    lb_re = mag * jnp.cos(a_im * dt)
    lb_im = mag * jnp.sin(a_im * dt)
    den = a_re * a_re + a_im * a_im
    n_re = lb_re - 1.0
    f_re = (n_re * a_re + lb_im * a_im) / den
    f_im = (lb_im * a_re - n_re * a_im) / den
    lr_ref[...] = lb_re
    li_ref[...] = lb_im
    bt_re = btr_ref[...]
    bt_im = bti_ref[...]
    bbr_ref[...] = f_re[:, None, :] * bt_re - f_im[:, None, :] * bt_im
    bbi_ref[...] = f_re[:, None, :] * bt_im + f_im[:, None, :] * bt_re


def _ssm_prep(a_re, a_im, log_dt, b_re, b_im):
    g, p, c = b_re.shape
    return pl.pallas_call(
        _ssm_prep_kernel,
        out_shape=(jax.ShapeDtypeStruct((g, p), F32), jax.ShapeDtypeStruct((g, p), F32),
                   jax.ShapeDtypeStruct((g, c, p), F32), jax.ShapeDtypeStruct((g, c, p), F32)),
        name="ssm_prep",
    )(a_re, a_im, log_dt.reshape(g, 1), jnp.swapaxes(b_re, 1, 2), jnp.swapaxes(b_im, 1, 2))


def _block_diag(m):
    gpb = SSM_GROUPS // SSM_BLOCKS
    _, r, c = m.shape
    m4 = m.reshape(SSM_BLOCKS, gpb, r, c)
    eye = jnp.eye(gpb, dtype=m.dtype)
    return jnp.einsum('jgrc,gh->jgrhc', m4, eye).reshape(SSM_BLOCKS, gpb * r, gpb * c)


def _qk_norm(q, g):
    lo = lax.broadcasted_iota(jnp.int32, (1, 128), 1) < QK_DIM
    outs = []
    for c in range(QK_COLS // 128):
        blk = q[:, c * 128:(c + 1) * 128]
        sq = blk * blk
        s_lo = jnp.sum(jnp.where(lo, sq, 0.0), axis=-1, keepdims=True)
        s_hi = jnp.sum(jnp.where(lo, 0.0, sq), axis=-1, keepdims=True)
        r_lo = lax.rsqrt(s_lo / QK_DIM + EPS)
        r_hi = lax.rsqrt(s_hi / QK_DIM + EPS)
        outs.append(blk * jnp.where(lo, r_lo, r_hi))
    return jnp.concatenate(outs, axis=-1) * g


def _inproj_kernel(x_ref, g1_ref, w_ref, qg_ref, kg_ref,
                   u_ref, qt_ref, k_ref, vt_ref, *, ta, tk):
    x = x_ref[...]
    hn = _rmsnorm(x, g1_ref[...]).astype(BF16)

    def proj(c0, width):
        return jnp.dot(hn, w_ref[:, c0:c0 + width], preferred_element_type=F32)

    q = proj(SSM_WIDTH, QK_COLS)
    k = proj(SSM_WIDTH + QK_COLS, QK_COLS)
    qn = _qk_norm(q, qg_ref[...]) * (QK_DIM ** -0.5 * LOG2E)
    for c in range(x.shape[0] // ta):
        qt_ref[c] = qn[c * ta:(c + 1) * ta, :].T.astype(BF16)
    v = proj(SSM_WIDTH + 2 * QK_COLS, ATT_WIDTH)
    k_ref[...] = _qk_norm(k, kg_ref[...]).astype(BF16)
    u_ref[...] = proj(0, SSM_WIDTH)
    for c in range(x.shape[0] // tk):
        vt_ref[c] = v[c * tk:(c + 1) * tk, :].T.astype(BF16)


def _inproj(x, g1, w_mix, qg, kg, *, t, ta, tk):
    b, l, _ = x.shape
    nt = l // t
    kern = functools.partial(_inproj_kernel, ta=ta, tk=tk)
    return pl.pallas_call(
        kern,
        grid=(b, nt),
        in_specs=[
            pl.BlockSpec((None, t, D_MODEL), lambda bi, i: (bi, i, 0)),
            pl.BlockSpec((1, D_MODEL), lambda bi, i: (0, 0)),
            pl.BlockSpec((D_MODEL, MIX_COLS), lambda bi, i: (0, 0)),
            pl.BlockSpec((1, QK_COLS), lambda bi, i: (0, 0)),
            pl.BlockSpec((1, QK_COLS), lambda bi, i: (0, 0)),
        ],
        out_specs=[
            pl.BlockSpec((None, t, SSM_WIDTH), lambda bi, i: (bi, i, 0)),
            pl.BlockSpec((None, t // ta, QK_COLS, ta), lambda bi, i: (bi, i, 0, 0)),
            pl.BlockSpec((None, t, QK_COLS), lambda bi, i: (bi, i, 0)),
            pl.BlockSpec((None, t // tk, ATT_WIDTH, tk), lambda bi, i: (bi, i, 0, 0)),
        ],
        out_shape=(
            jax.ShapeDtypeStruct((b, l, SSM_WIDTH), F32),
            jax.ShapeDtypeStruct((b, l // ta, QK_COLS, ta), BF16),
            jax.ShapeDtypeStruct((b, l, QK_COLS), BF16),
            jax.ShapeDtypeStruct((b, l // tk, ATT_WIDTH, tk), BF16),
        ),
        compiler_params=pltpu.CompilerParams(
            dimension_semantics=("parallel", "parallel"), vmem_limit_bytes=VMEM_LIMIT),
        name="inproj",
    )(x, g1, w_mix, qg, kg)


def _ssm_kernel(u_ref, s0r_ref, s0i_ref, wb_ref, lamr_ref, lami_ref, wcr_ref, wci_ref,
                d_ref, gluw_ref, glub_ref,
                y_ref, sfr_ref, sfi_ref,
                xr_ref, xi_ref, sr_ref, si_ref, *, tc, nb):
    t = pl.program_id(0)

    @pl.when(t == 0)
    def _():
        sr_ref[...] = s0r_ref[...]
        si_ref[...] = s0i_ref[...]

    u = jnp.swapaxes(u_ref[...], 0, 1).reshape(tc * nb, SSM_WIDTH)
    ub = u.astype(BF16)

    def project_in(j):
        x = jnp.dot(ub[:, j * SSM_BLOCK_IN:(j + 1) * SSM_BLOCK_IN], wb_ref[j],
                    preferred_element_type=F32)
        xr_ref[j] = x[:, :SSM_BLOCK_STATE]
        xi_ref[j] = x[:, SSM_BLOCK_STATE:]

    def scan(j):
        cols = slice(j * SSM_BLOCK_STATE, (j + 1) * SSM_BLOCK_STATE)
        lr = lamr_ref[:, cols]
        li = lami_ref[:, cols]
        s_re = sr_ref[:, cols]
        s_im = si_ref[:, cols]
        for tt in range(tc):
            rows = slice(tt * nb, (tt + 1) * nb)
            s_re, s_im = (lr * s_re - li * s_im + xr_ref[j, rows, :],
                          lr * s_im + li * s_re + xi_ref[j, rows, :])
            xr_ref[j, rows, :] = s_re
            xi_ref[j, rows, :] = s_im
        sr_ref[:, cols] = s_re
        si_ref[:, cols] = s_im

    def project_out(j):
        return (jnp.dot(xr_ref[j].astype(BF16), wcr_ref[j], preferred_element_type=F32)
                - jnp.dot(xi_ref[j].astype(BF16), wci_ref[j], preferred_element_type=F32))

    ys = []
    project_in(0)
    for j in range(SSM_BLOCKS):
        if j + 1 < SSM_BLOCKS:
            project_in(j + 1)
        scan(j)
        ys.append(project_out(j))
    y = jnp.concatenate(ys, axis=-1) + d_ref[...] * u
    y = _gelu(y)
    z = jnp.dot(y.astype(BF16), gluw_ref[...], preferred_element_type=F32) + glub_ref[...]
    out = (y * jax.nn.sigmoid(z)).reshape(tc, nb, SSM_WIDTH)
    y_ref[...] = jnp.swapaxes(out, 0, 1).astype(BF16)

    @pl.when(t == pl.num_programs(0) - 1)
    def _():
        sfr_ref[...] = sr_ref[...]
        sfi_ref[...] = si_ref[...]


def _ssm(u, s0r, s0i, wb, lamr, lami, wcr, wci, d, gluw, glub, *, tc):
    nb, seq, _ = u.shape
    r = tc * nb
    nsteps = seq // tc
    ns = SSM_GROUPS * SSM_STATE
    kern = functools.partial(_ssm_kernel, tc=tc, nb=nb)
    const2 = lambda t: (0, 0)
    const3 = lambda t: (0, 0, 0)
    return pl.pallas_call(
        kern,
        grid=(nsteps,),
        in_specs=[
            pl.BlockSpec((nb, tc, SSM_WIDTH), lambda t: (0, t, 0)),
            pl.BlockSpec((nb, ns), const2),
            pl.BlockSpec((nb, ns), const2),
            pl.BlockSpec((SSM_BLOCKS, SSM_BLOCK_IN, 2 * SSM_BLOCK_STATE), const3),
            pl.BlockSpec((nb, ns), const2),
            pl.BlockSpec((nb, ns), const2),
            pl.BlockSpec((SSM_BLOCKS, SSM_BLOCK_STATE, SSM_BLOCK_IN), const3),
            pl.BlockSpec((SSM_BLOCKS, SSM_BLOCK_STATE, SSM_BLOCK_IN), const3),
            pl.BlockSpec((1, SSM_WIDTH), const2),
            pl.BlockSpec((SSM_WIDTH, SSM_WIDTH), const2),
            pl.BlockSpec((1, SSM_WIDTH), const2),
        ],
        out_specs=[
            pl.BlockSpec((nb, tc, SSM_WIDTH), lambda t: (0, t, 0)),
            pl.BlockSpec((nb, ns), const2),
            pl.BlockSpec((nb, ns), const2),
        ],
        out_shape=(
            jax.ShapeDtypeStruct((nb, seq, SSM_WIDTH), BF16),
            jax.ShapeDtypeStruct((nb, ns), F32),
            jax.ShapeDtypeStruct((nb, ns), F32),
        ),
        scratch_shapes=[
            pltpu.VMEM((SSM_BLOCKS, r, SSM_BLOCK_STATE), F32),
            pltpu.VMEM((SSM_BLOCKS, r, SSM_BLOCK_STATE), F32),
            pltpu.VMEM((nb, ns), F32),
            pltpu.VMEM((nb, ns), F32),
        ],
        compiler_params=pltpu.CompilerParams(
            dimension_semantics=("arbitrary",), vmem_limit_bytes=VMEM_LIMIT),
        name="ssm",
    )(u, s0r, s0i, wb, lamr, lami, wcr, wci, d, gluw, glub)


def _attn_kernel(slopes_ref, *rest, ta, tk, n_tiles, has_prefix, lam_init):
    qt_ref, k_ref, vt_ref = rest[:3]
    rest = rest[3:]
    if has_prefix:
        kh_ref, vht_ref = rest[:2]
        rest = rest[2:]
    (kb_ref, ones_ref, lq1_ref, lk1_ref, lq2_ref, lk2_ref, subg_ref,
     o_ref, q2_ref, sa_ref, sb_ref, pa_ref, pb_ref, acc_ref, m_ref, alpha_ref) = rest

    h = pl.program_id(1)
    g = pl.program_id(2) if pl.num_programs(2) > 1 else 0
    slope = slopes_ref[h]
    nsub = ta // tk
    nq = k_ref.shape[0] // ta
    tile_idx = [g + t * (nq // n_tiles) for t in range(n_tiles)]
    width = 2 * ta
    total = n_tiles * width

    for t in range(n_tiles):
        qt = qt_ref[t]
        first = lax.broadcasted_iota(jnp.int32, qt.shape, 0) < QK_DIM
        zero = jnp.zeros_like(qt)
        q2_ref[:V_DIM, t * width:t * width + ta] = jnp.where(first, qt, zero)
        q2_ref[:V_DIM, t * width + ta:(t + 1) * width] = jnp.where(first, zero, qt)
    slope2 = jnp.full((1, total), slope * LOG2E, F32)
    c_hi = slope2.astype(BF16).astype(F32)
    c_mid = (slope2 - c_hi).astype(BF16).astype(F32)
    c_lo = slope2 - c_hi - c_mid
    bias_row = lax.broadcasted_iota(jnp.int32, (V_DIM, total), 0)
    q2_ref[V_DIM:, :] = jnp.where(
        bias_row == 0, c_hi,
        jnp.where(bias_row == 1, c_mid, jnp.where(bias_row == 2, c_lo, 0.0))).astype(BF16)
    slope = slope * LOG2E

    def key_lhs(k_blk):
        return jnp.concatenate([k_blk, kb_ref[:k_blk.shape[0], :]], axis=1)

    def value_lhs(vt_blk):
        return jnp.concatenate([vt_blk, ones_ref[:, :vt_blk.shape[1]]], axis=0)

    def key_block(kidx):
        if isinstance(kidx, int):
            return k_ref[kidx * tk:(kidx + 1) * tk, :]
        return k_ref[pl.ds(pl.multiple_of(kidx * tk, tk), tk), :]

    def clamp0(n):
        return max(n, 0) if isinstance(n, int) else jnp.maximum(n, 0)

    def f32(n):
        return float(n) if isinstance(n, int) else n.astype(F32)

    def softmax_step(m_old, s, shift):
        m_new = jnp.maximum(m_old, jnp.max(s, axis=0, keepdims=True) - shift)
        alpha = jnp.exp2(m_old - m_new)
        p = jnp.exp2(s - (m_new + shift))
        return m_new, alpha, p.astype(BF16)

    sa_ref[...] = jnp.dot(key_lhs(key_block(0)), q2_ref[...], preferred_element_type=F32)
    pb_ref[...] = jnp.zeros(pb_ref.shape, BF16)
    acc_ref[...] = jnp.zeros(acc_ref.shape, F32)
    m_ref[...] = jnp.full((1, total), NEG_INF, F32)
    alpha_ref[...] = jnp.ones((1, total), F32)

    def make_body(col_tiles):
        def body(j, carry):
            n0 = 2 * j
            k_lhs = [key_lhs(key_block(n0 + st + 1)) for st in range(2)]
            v_lhs = [value_lhs(vt_ref[clamp0(n0 - 1)]), value_lhs(vt_ref[n0])]
            s_bufs = (sa_ref, sb_ref)
            p_bufs = (pa_ref, pb_ref)
            for ct in col_tiles:
                cs = slice(ct * COL_TILE, (ct + 1) * COL_TILE)
                q0 = tile_idx[ct * COL_TILE // width] * ta
                shift0 = slope * f32(q0 - n0 * tk)
                m_c, alpha_c, acc = m_ref[:, cs], alpha_ref[:, cs], acc_ref[:, cs]
                for st in range(2):
                    s_bufs[(st + 1) % 2][:, cs] = jnp.dot(
                        k_lhs[st], q2_ref[:, cs], preferred_element_type=F32)
                    m_c, alpha_n, p = softmax_step(
                        m_c, s_bufs[st % 2][:, cs], shift0 - slope * (st * tk))
                    p_bufs[st % 2][:, cs] = p
                    acc = acc * alpha_c + jnp.dot(
                        v_lhs[st], p_bufs[(st + 1) % 2][:, cs], preferred_element_type=F32)
                    alpha_c = alpha_n
                acc_ref[:, cs] = acc
                m_ref[:, cs] = m_c
                alpha_ref[:, cs] = alpha_c
            return carry
        return body

    if nsub == 2:
        per_tile = width // COL_TILE
        all_ct = list(range(total // COL_TILE))
        for t in range(n_tiles):
            lo = tile_idx[t - 1] if t else 0
            body = make_body(all_ct[t * per_tile:])
            if isinstance(lo, int):
                for j in range(lo, tile_idx[t]):
                    body(j, 0)
            else:
                lax.fori_loop(lo, tile_idx[t], body, 0)
    else:
        assert nsub == 1 and n_tiles == 1 and nq == 1

    dw = min(DIAG_TILE, width)
    n_dt = total // dw
    tiles = [slice(dt * dw, (dt + 1) * dw) for dt in range(n_dt)]
    owner = [dt * dw // width for dt in range(n_dt)]
    q_base = [dt * dw % width for dt in range(n_dt)]
    q_his = [min(q_base[dt] % ta + dw, ta) - 1 for dt in range(n_dt)]
    k_idx = lax.broadcasted_iota(jnp.int32, (tk, dw), 0)
    col = lax.broadcasted_iota(jnp.int32, (tk, dw), 1)
    if has_prefix:
        k_pre = key_lhs(kh_ref[...])[N_JUNK:, :]
        s_pre = [jnp.dot(k_pre, q2_ref[:, cs], preferred_element_type=F32) for cs in tiles]
    v_pend = [value_lhs(vt_ref[clamp0(ti * nsub - 1)]) for ti in tile_idx]
    m_t = [m_ref[:, cs] for cs in tiles]
    acc_t = [acc_ref[:, cs] * alpha_ref[:, cs]
             + jnp.dot(v_pend[owner[dt]], pb_ref[:, cs], preferred_element_type=F32)
             for dt, cs in enumerate(tiles)]
    s_t = {dt: sa_ref[:, cs] for dt, cs in enumerate(tiles)}
    masks = {}
    for c in range(nsub):
        live = [dt for dt in range(n_dt) if q_his[dt] >= c * tk]
        if c + 1 < nsub:
            k_next = [key_lhs(key_block(ti * nsub + c + 1)) for ti in tile_idx]
            s_next = {dt: jnp.dot(k_next[owner[dt]], q2_ref[:, tiles[dt]],
                                  preferred_element_type=F32)
                      for dt in live if q_his[dt] >= (c + 1) * tk}
        p_t = {}
        for dt in live:
            s = s_t[dt]
            lead = q_base[dt] % ta - c * tk
            causal = lead < tk - 1 or dw > ta
            if causal or not has_prefix:
                key = (lead if causal else None, dw > ta)
                if key not in masks:
                    q_idx = (col & (ta - 1)) if dw > ta else col
                    keep = q_idx - k_idx >= -lead if causal else None
                    if not has_prefix:
                        junk = k_idx >= N_JUNK - c * tk
                        keep = junk if keep is None else jnp.logical_and(keep, junk)
                    masks[key] = keep
                s = jnp.where(masks[key], s, NEG_INF)
            m_t[dt], alpha_c, p_t[dt] = softmax_step(m_t[dt], s, slope * (-c * tk))
            acc_t[dt] = acc_t[dt] * alpha_c
        v_c = [value_lhs(vt_ref[ti * nsub + c]) for ti in tile_idx]
        for dt in live:
            acc_t[dt] = acc_t[dt] + jnp.dot(v_c[owner[dt]], p_t[dt],
                                            preferred_element_type=F32)
        if c + 1 < nsub:
            s_t = s_next
    if has_prefix:
        v_pre = value_lhs(vht_ref[...])
        no_weight = jnp.zeros((N_JUNK, dw), BF16)
        for dt in range(n_dt):
            shift = slope * f32(HEAD_ROWS + tile_idx[owner[dt]] * ta)
            m_t[dt], alpha_c, p = softmax_step(m_t[dt], s_pre[dt], shift)
            p = jnp.concatenate([no_weight, p], axis=0)
            acc_t[dt] = acc_t[dt] * alpha_c + jnp.dot(v_pre, p, preferred_element_type=F32)

    lam = (jnp.exp(jnp.sum(lq1_ref[...] * lk1_ref[...], axis=-1, keepdims=True))
           - jnp.exp(jnp.sum(lq2_ref[...] * lk2_ref[...], axis=-1, keepdims=True))
           + lam_init)
    per = width // dw
    for t in range(n_tiles):
        acc = jnp.concatenate(acc_t[t * per:(t + 1) * per], axis=1)
        l = acc[V_DIM:V_DIM + 1, :]
        acc = acc[:V_DIM, :]
        o = acc[:, :ta] / l[:, :ta] - lam * (acc[:, ta:] / l[:, ta:])
        r = lax.rsqrt(jnp.mean(o * o, axis=0, keepdims=True) + EPS)
        y = ((o * r) * subg_ref[...]) * (1.0 - lam_init)
        o_ref[t] = y.T.astype(BF16)


def _attention(slopes, qt, k, vt, prefix, lq1, lk1, lq2, lk2, subg, *, ta, tk, lam_init):
    b, l, _ = k.shape
    nq = l // ta
    nkv = l // tk
    assert ta & (ta - 1) == 0 and ta % tk == 0
    has_prefix = prefix is not None
    n_tiles = math.gcd(nq, TILES_PER_STEP)
    steps = nq // n_tiles
    kern = functools.partial(_attn_kernel, ta=ta, tk=tk, n_tiles=n_tiles,
                             has_prefix=has_prefix, lam_init=lam_init)
    qt = qt.reshape(b, n_tiles, steps, QK_COLS, ta)
    in_specs = [
        pl.BlockSpec(memory_space=pltpu.SMEM),
        pl.BlockSpec((None, n_tiles, None, V_DIM, ta), lambda bi, h, g: (bi, 0, g, h, 0)),
        pl.BlockSpec((None, l, V_DIM), lambda bi, h, g: (bi, 0, h)),
        pl.BlockSpec((None, nkv, V_DIM, tk), lambda bi, h, g: (bi, 0, h, 0)),
    ]
    args = [slopes, qt, k, vt]
    if has_prefix:
        kh, vht = prefix
        assert tk >= HEAD_ROWS
        in_specs += [
            pl.BlockSpec((None, HEAD_ROWS, V_DIM), lambda bi, h, g: (0, 0, h)),
            pl.BlockSpec((None, None, V_DIM, HEAD_ROWS), lambda bi, h, g: (0, 0, h, 0)),
        ]
        args += [kh, vht]
    const2 = lambda bi, h, g: (0, 0)
    in_specs += [
        pl.BlockSpec((tk, V_DIM), const2),
        pl.BlockSpec((SUM_ROWS, tk), const2),
        pl.BlockSpec((1, QK_DIM), const2), pl.BlockSpec((1, QK_DIM), const2),
        pl.BlockSpec((1, QK_DIM), const2), pl.BlockSpec((1, QK_DIM), const2),
        pl.BlockSpec((V_DIM, 1), const2),
    ]
    assert tk <= 256
    kb = jnp.zeros((tk, V_DIM), F32).at[:, :3].set(jnp.arange(tk, dtype=F32)[:, None])
    ones = jnp.zeros((SUM_ROWS, tk), F32).at[0].set(1.0)
    args += [kb.astype(BF16), ones.astype(BF16), lq1, lk1, lq2, lk2, subg]
    total = n_tiles * 2 * ta
    out = pl.pallas_call(
        kern,
        grid=(b, ATT_HEADS, steps),
        in_specs=in_specs,
        out_specs=pl.BlockSpec((None, n_tiles, ta, V_DIM), lambda bi, h, g: (bi, 0, g, h)),
        out_shape=jax.ShapeDtypeStruct((b, n_tiles, l // n_tiles, ATT_WIDTH), BF16),
        scratch_shapes=[
            pltpu.VMEM((2 * V_DIM, total), BF16),
            pltpu.VMEM((tk, total), F32),
            pltpu.VMEM((tk, total), F32),
            pltpu.VMEM((tk, total), BF16),
            pltpu.VMEM((tk, total), BF16),
            pltpu.VMEM((V_DIM + SUM_ROWS, total), F32),
            pltpu.VMEM((1, total), F32),
            pltpu.VMEM((1, total), F32),
        ],
        compiler_params=pltpu.CompilerParams(
            dimension_semantics=("parallel", "parallel", "arbitrary"),
            vmem_limit_bytes=VMEM_LIMIT),
        name="diff_attention",
    )(*args)
    return out.reshape(b, l, ATT_WIDTH)


def _merge_kernel(x_ref, ys_ref, ya_ref, g1_ref, wg_ref, wso_ref, wao_ref, wo_ref, h1_ref):
    x = x_ref[...]
    hn = _rmsnorm(x, g1_ref[...]).astype(BF16)
    gates = jnp.dot(hn, wg_ref[...], preferred_element_type=F32)
    a = jnp.dot(ys_ref[...], wso_ref[...], preferred_element_type=F32)
    c = jnp.dot(ya_ref[...], wao_ref[...], preferred_element_type=F32)
    mixed = (jax.nn.sigmoid(gates[:, :D_MODEL]) * a
             + jax.nn.sigmoid(gates[:, D_MODEL:]) * c)
    h1_ref[...] = x + jnp.dot(mixed.astype(BF16), wo_ref[...], preferred_element_type=F32)


def _merge(x, ys, ya, g1, wg, wso, wao, wo, *, t):
    b, l, _ = x.shape
    const2 = lambda bi, i: (0, 0)
    row = lambda bi, i: (bi, i, 0)
    return pl.pallas_call(
        _merge_kernel,
        grid=(b, l // t),
        in_specs=[
            pl.BlockSpec((None, t, D_MODEL), row),
            pl.BlockSpec((None, t, SSM_WIDTH), row),
            pl.BlockSpec((None, t, ATT_WIDTH), row),
            pl.BlockSpec((1, D_MODEL), const2),
            pl.BlockSpec((D_MODEL, 2 * D_MODEL), const2, pipeline_mode=pl.Buffered(1)),
            pl.BlockSpec((SSM_WIDTH, D_MODEL), const2, pipeline_mode=pl.Buffered(1)),
            pl.BlockSpec((ATT_WIDTH, D_MODEL), const2, pipeline_mode=pl.Buffered(1)),
            pl.BlockSpec((D_MODEL, D_MODEL), const2, pipeline_mode=pl.Buffered(1)),
        ],
        out_specs=pl.BlockSpec((None, t, D_MODEL), row),
        out_shape=jax.ShapeDtypeStruct((b, l, D_MODEL), F32),
        compiler_params=pltpu.CompilerParams(
            dimension_semantics=("parallel", "parallel"), vmem_limit_bytes=VMEM_LIMIT),
        name="merge",
    )(x, ys, ya, g1, wg, wso, wao, wo)


def _ffn_kernel(h_ref, hprev_ref, hhead_ref, g2_ref, wa_ref, wg_ref, cw_ref, cb_ref, wdn_ref,
                out_ref, hn_ref, gate_ref):
    i = pl.program_id(1)
    t = h_ref.shape[0]
    h = h_ref[...]
    g2 = g2_ref[...]
    halo = jnp.where(i == 0, hhead_ref[...], hprev_ref[...])
    hn_ref[:HALO, :] = _rmsnorm(halo, g2).astype(BF16)
    hn_ref[HALO:, :] = _rmsnorm(h, g2).astype(BF16)
    hn = hn_ref[...]
    for f in range(N_FF_CHUNKS):
        cols = slice(f * FF_CHUNK, (f + 1) * FF_CHUNK)
        a = jnp.dot(hn, wa_ref[:, cols], preferred_element_type=F32)
        gate = jnp.dot(hn[HALO:], wg_ref[:, cols], preferred_element_type=F32)
        c = (a[HALO - 2:HALO - 2 + t] * cw_ref[0:1, cols]
             + a[HALO - 1:HALO - 1 + t] * cw_ref[1:2, cols]
             + a[HALO:] * cw_ref[2:3, cols] + cb_ref[:, cols])
        gate_ref[:, cols] = (_gelu(c) * gate).astype(BF16)
    out_ref[...] = h + jnp.dot(gate_ref[...], wdn_ref[...], preferred_element_type=F32)


def _ffn(h1, h1_head, g2, wa, wg, cw, cb, wdn, *, t):
    b, l, _ = h1.shape
    const2 = lambda bi, i: (0, 0)
    row = lambda bi, i: (bi, i, 0)
    per = t // HALO
    return pl.pallas_call(
        _ffn_kernel,
        grid=(b, l // t),
        in_specs=[
            pl.BlockSpec((None, t, D_MODEL), row),
            pl.BlockSpec((None, HALO, D_MODEL),
                         lambda bi, i: (bi, jnp.maximum(i * per - 1, 0), 0)),
            pl.BlockSpec((None, HALO, D_MODEL),
                         lambda bi, i: (0, HEAD_ROWS // HALO - 1, 0)),
            pl.BlockSpec((1, D_MODEL), const2),
            pl.BlockSpec((D_MODEL, D_FF), const2, pipeline_mode=pl.Buffered(1)),
            pl.BlockSpec((D_MODEL, D_FF), const2, pipeline_mode=pl.Buffered(1)),
            pl.BlockSpec((3, D_FF), const2),
            pl.BlockSpec((1, D_FF), const2),
            pl.BlockSpec((D_FF, D_MODEL), const2, pipeline_mode=pl.Buffered(1)),
        ],
        out_specs=pl.BlockSpec((None, t, D_MODEL), row),
        out_shape=jax.ShapeDtypeStruct((b, l, D_MODEL), F32),
        scratch_shapes=[
            pltpu.VMEM((HALO + t, D_MODEL), BF16),
            pltpu.VMEM((t, D_FF), BF16),
        ],
        compiler_params=pltpu.CompilerParams(
            dimension_semantics=("parallel", "arbitrary"), vmem_limit_bytes=VMEM_LIMIT),
        name="conv_ffn",
    )(h1, h1, h1_head, g2, wa, wg, cw, cb, wdn)


def _pick_tile(l, pref):
    t = min(pref, l)
    while l % t:
        t //= 2
    return t


def _layer(x, head, p, l_idx):
    bsz, seq, _ = x.shape
    lam_init = 0.8 - 0.6 * math.exp(-0.3 * l_idx)
    slopes = 2.0 ** (-8.0 * jnp.arange(1, ATT_HEADS + 1, dtype=F32) / ATT_HEADS)

    g1 = p['norm1_g'][l_idx][None]
    w_in = p['w_in'][l_idx]
    w_mix = w_in[:, :MIX_COLS].astype(BF16)
    w_gate = w_in[:, MIX_COLS:].astype(BF16)
    qg = jnp.tile(p['q_norm_g'][l_idx], QK_COLS // QK_DIM)[None]
    kg = jnp.tile(p['k_norm_g'][l_idx], QK_COLS // QK_DIM)[None]

    lam_re, lam_im, bbt_re, bbt_im = _ssm_prep(
        p['ssm_a_re'][l_idx], p['ssm_a_im'][l_idx], p['ssm_log_dt'][l_idx],
        p['ssm_b_re'][l_idx], p['ssm_b_im'][l_idx])
    wb = jnp.concatenate([_block_diag(bbt_re), _block_diag(bbt_im)], axis=-1).astype(BF16)
    wcr = _block_diag(jnp.swapaxes(p['ssm_c_re'][l_idx], 1, 2)).astype(BF16)
    wci = _block_diag(jnp.swapaxes(p['ssm_c_im'][l_idx], 1, 2)).astype(BF16)
    ns = SSM_GROUPS * SSM_STATE
    lamr = jnp.broadcast_to(lam_re.reshape(1, ns), (bsz, ns))
    lami = jnp.broadcast_to(lam_im.reshape(1, ns), (bsz, ns))
    d = p['ssm_d'][l_idx].reshape(1, SSM_WIDTH)
    gluw = p['ssm_glu_w'][l_idx].astype(BF16)
    glub = p['ssm_glu_b'][l_idx][None]

    lq1, lk1 = p['lam_q1'][l_idx][None], p['lam_k1'][l_idx][None]
    lq2, lk2 = p['lam_q2'][l_idx][None], p['lam_k2'][l_idx][None]
    subg = p['subln_g'][l_idx][:, None]
    wso = p['w_ssm_out'][l_idx].astype(BF16)
    wao = p['w_att_out'][l_idx].astype(BF16)
    wo = p['w_o'][l_idx].astype(BF16)

    t_row = _pick_tile(seq, 1024)
    ta = _pick_tile(seq, 512)
    tk = _pick_tile(seq, 256)
    tc = _pick_tile(seq, 64)

    u_h, qt_h, k_h, vt_h = _inproj(head, g1, w_mix, qg, kg, t=HEAD_ROWS, ta=HEAD_ROWS,
                                    tk=HEAD_ROWS)
    zeros_state = jnp.zeros((bsz, ns), F32)
    u_hb = jnp.broadcast_to(u_h, (bsz,) + u_h.shape[1:])
    ys_hb, s0r, s0i = _ssm(u_hb, zeros_state, zeros_state, wb, lamr, lami, wcr, wci,
                           d, gluw, glub, tc=_pick_tile(HEAD_ROWS, 64))
    ys_h = ys_hb[:1]
    ya_h = _attention(slopes, qt_h, k_h, vt_h, None, lq1, lk1, lq2, lk2, subg,
                      ta=HEAD_ROWS, tk=HEAD_ROWS, lam_init=lam_init)
    h1_head = _merge(head, ys_h, ya_h, g1, w_gate, wso, wao, wo, t=HEAD_ROWS)

    u, qt, k, vt = _inproj(x, g1, w_mix, qg, kg, t=t_row, ta=ta, tk=tk)
    ys, _, _ = _ssm(u, s0r, s0i, wb, lamr, lami, wcr, wci, d, gluw, glub, tc=tc)
    ya = _attention(slopes, qt, k, vt, (k_h, vt_h), lq1, lk1, lq2, lk2, subg,
                    ta=ta, tk=tk, lam_init=lam_init)
    h1 = _merge(x, ys, ya, g1, w_gate, wso, wao, wo, t=t_row)

    w_up = p['w_up'][l_idx]
    out = _ffn(h1, h1_head, p['norm2_g'][l_idx][None], w_up[:, :D_FF].astype(BF16),
               w_up[:, D_FF:].astype(BF16), p['conv_w'][l_idx],
               p['conv_b'][l_idx][None], p['w_down'][l_idx].astype(BF16), t=t_row)
    return out, h1_head


def kernel(x, meta_tokens, norm1_g, w_in, ssm_a_re, ssm_a_im, ssm_log_dt, ssm_b_re, ssm_b_im,
           ssm_c_re, ssm_c_im, ssm_d, ssm_glu_w, ssm_glu_b, q_norm_g, k_norm_g,
           lam_q1, lam_k1, lam_q2, lam_k2, subln_g, w_ssm_out, w_att_out, w_o,
           norm2_g, w_up, conv_w, conv_b, w_down):
    params = dict(norm1_g=norm1_g, w_in=w_in, ssm_a_re=ssm_a_re, ssm_a_im=ssm_a_im,
                  ssm_log_dt=ssm_log_dt, ssm_b_re=ssm_b_re, ssm_b_im=ssm_b_im,
                  ssm_c_re=ssm_c_re, ssm_c_im=ssm_c_im, ssm_d=ssm_d, ssm_glu_w=ssm_glu_w,
                  ssm_glu_b=ssm_glu_b, q_norm_g=q_norm_g, k_norm_g=k_norm_g,
                  lam_q1=lam_q1, lam_k1=lam_k1, lam_q2=lam_q2, lam_k2=lam_k2,
                  subln_g=subln_g, w_ssm_out=w_ssm_out, w_att_out=w_att_out, w_o=w_o,
                  norm2_g=norm2_g, w_up=w_up, conv_w=conv_w, conv_b=conv_b, w_down=w_down)
    depth = norm1_g.shape[0]
    assert depth == 1, "the head tile is only carried through one layer"
    head = jnp.concatenate(
        [jnp.zeros((N_JUNK, D_MODEL), x.dtype), meta_tokens.astype(x.dtype)], axis=0)[None]
    out, _ = _layer(x, head, params, 0)
    return out
```

```python
import functools
import math

import jax
import jax.numpy as jnp
from jax import lax
from jax.experimental import pallas as pl
from jax.experimental.pallas import tpu as pltpu

F32 = jnp.float32
BF16 = jnp.bfloat16

D_MODEL = 1024
N_META = 16
EPS = 1e-6
NEG_INF = -1e30

SSM_WIDTH = 512
SSM_GROUP = 16
SSM_GROUPS = 32
SSM_STATE = 64
SSM_BLOCKS = 4
SSM_BLOCK_IN = SSM_WIDTH // SSM_BLOCKS
SSM_BLOCK_STATE = SSM_GROUPS * SSM_STATE // SSM_BLOCKS

ATT_HEADS = 4
QK_DIM = 64
V_DIM = 128
ATT_WIDTH = 512
QK_COLS = 512
MIX_COLS = SSM_WIDTH + 2 * QK_COLS + ATT_WIDTH

D_FF = 2816
FF_CHUNK = 256
N_FF_CHUNKS = D_FF // FF_CHUNK

HEAD_ROWS = 128
N_JUNK = HEAD_ROWS - N_META
SUM_ROWS = 16
LOG2E = math.log2(math.e)
DIAG_TILE = 256
TILES_PER_STEP = 8
COL_TILE = 256
HALO = 16

VMEM_LIMIT = 56 * 1024 * 1024


def _rmsnorm(x, g):
    r = lax.rsqrt(jnp.mean(x * x, axis=-1, keepdims=True) + EPS)
    return (x * r) * g


def _gelu(x):
    return jax.nn.gelu(x, approximate=True)


def _ssm_prep_kernel(are_ref, aim_ref, ldt_ref, btr_ref, bti_ref,
                     lr_ref, li_ref, bbr_ref, bbi_ref):
    a_re = are_ref[...]
    a_im = aim_ref[...]
    dt = jnp.exp(ldt_ref[...])
    mag = jnp.exp(a_re * dt)
    lb_re = mag * jnp.cos(a_im * dt)
    lb_im = mag * jnp.sin(a_im * dt)
    den = a_re * a_re + a_im * a_im
    n_re = lb_re - 1.0
    f_re = (n_re * a_re + lb_im * a_im) / den
    f_im = (lb_im * a_re - n_re * a_im) / den
    lr_ref[...] = lb_re
    li_ref[...] = lb_im
    bt_re = btr_ref[...]
    bt_im = bti_ref[...]
    bbr_ref[...] = f_re[:, None, :] * bt_re - f_im[:, None, :] * bt_im
    bbi_ref[...] = f_re[:, None, :] * bt_im + f_im[:, None, :] * bt_re


def _ssm_prep(a_re, a_im, log_dt, b_re, b_im):
    g, p, c = b_re.shape
    return pl.pallas_call(
        _ssm_prep_kernel,
        out_shape=(jax.ShapeDtypeStruct((g, p), F32), jax.ShapeDtypeStruct((g, p), F32),
                   jax.ShapeDtypeStruct((g, c, p), F32), jax.ShapeDtypeStruct((g, c, p), F32)),
        name="ssm_prep",
    )(a_re, a_im, log_dt.reshape(g, 1), jnp.swapaxes(b_re, 1, 2), jnp.swapaxes(b_im, 1, 2))


def _block_diag(m):
    gpb = SSM_GROUPS // SSM_BLOCKS
    _, r, c = m.shape
    m4 = m.reshape(SSM_BLOCKS, gpb, r, c)
    eye = jnp.eye(gpb, dtype=m.dtype)
    return jnp.einsum('jgrc,gh->jgrhc', m4, eye).reshape(SSM_BLOCKS, gpb * r, gpb * c)


def _qk_norm(q, g):
    lo = lax.broadcasted_iota(jnp.int32, (1, 128), 1) < QK_DIM
    outs = []
    for c in range(QK_COLS // 128):
        blk = q[:, c * 128:(c + 1) * 128]
        sq = blk * blk
        s_lo = jnp.sum(jnp.where(lo, sq, 0.0), axis=-1, keepdims=True)
        s_hi = jnp.sum(jnp.where(lo, 0.0, sq), axis=-1, keepdims=True)
        r_lo = lax.rsqrt(s_lo / QK_DIM + EPS)
        r_hi = lax.rsqrt(s_hi / QK_DIM + EPS)
        outs.append(blk * jnp.where(lo, r_lo, r_hi))
    return jnp.concatenate(outs, axis=-1) * g


def _inproj_kernel(x_ref, g1_ref, w_ref, qg_ref, kg_ref,
                   u_ref, qt_ref, k_ref, vt_ref, *, ta, tk):
    x = x_ref[...]
    hn = _rmsnorm(x, g1_ref[...]).astype(BF16)

    def proj(c0, width):
        return jnp.dot(hn, w_ref[:, c0:c0 + width], preferred_element_type=F32)

    q = proj(SSM_WIDTH, QK_COLS)
    k = proj(SSM_WIDTH + QK_COLS, QK_COLS)
    qn = _qk_norm(q, qg_ref[...]) * (QK_DIM ** -0.5 * LOG2E)
    for c in range(x.shape[0] // ta):
        qt_ref[c] = qn[c * ta:(c + 1) * ta, :].T.astype(BF16)
    v = proj(SSM_WIDTH + 2 * QK_COLS, ATT_WIDTH)
    k_ref[...] = _qk_norm(k, kg_ref[...]).astype(BF16)
    u_ref[...] = proj(0, SSM_WIDTH)
    for c in range(x.shape[0] // tk):
        vt_ref[c] = v[c * tk:(c + 1) * tk, :].T.astype(BF16)


def _inproj(x, g1, w_mix, qg, kg, *, t, ta, tk):
    b, l, _ = x.shape
    nt = l // t
    kern = functools.partial(_inproj_kernel, ta=ta, tk=tk)
    return pl.pallas_call(
        kern,
        grid=(b, nt),
        in_specs=[
            pl.BlockSpec((None, t, D_MODEL), lambda bi, i: (bi, i, 0)),
            pl.BlockSpec((1, D_MODEL), lambda bi, i: (0, 0)),
            pl.BlockSpec((D_MODEL, MIX_COLS), lambda bi, i: (0, 0)),
            pl.BlockSpec((1, QK_COLS), lambda bi, i: (0, 0)),
            pl.BlockSpec((1, QK_COLS), lambda bi, i: (0, 0)),
        ],
        out_specs=[
            pl.BlockSpec((None, t, SSM_WIDTH), lambda bi, i: (bi, i, 0)),
            pl.BlockSpec((None, t // ta, QK_COLS, ta), lambda bi, i: (bi, i, 0, 0)),
            pl.BlockSpec((None, t, QK_COLS), lambda bi, i: (bi, i, 0)),
            pl.BlockSpec((None, t // tk, ATT_WIDTH, tk), lambda bi, i: (bi, i, 0, 0)),
        ],
        out_shape=(
            jax.ShapeDtypeStruct((b, l, SSM_WIDTH), F32),
            jax.ShapeDtypeStruct((b, l // ta, QK_COLS, ta), BF16),
            jax.ShapeDtypeStruct((b, l, QK_COLS), BF16),
            jax.ShapeDtypeStruct((b, l // tk, ATT_WIDTH, tk), BF16),
        ),
        compiler_params=pltpu.CompilerParams(
            dimension_semantics=("parallel", "parallel"), vmem_limit_bytes=VMEM_LIMIT),
        name="inproj",
    )(x, g1, w_mix, qg, kg)


def _ssm_kernel(u_ref, s0r_ref, s0i_ref, wb_ref, lamr_ref, lami_ref, wcr_ref, wci_ref,
                d_ref, gluw_ref, glub_ref,
                y_ref, sfr_ref, sfi_ref,
                xr_ref, xi_ref, sr_ref, si_ref, *, tc, nb):
    t = pl.program_id(0)

    @pl.when(t == 0)
    def _():
        sr_ref[...] = s0r_ref[...]
        si_ref[...] = s0i_ref[...]

    u = jnp.swapaxes(u_ref[...], 0, 1).reshape(tc * nb, SSM_WIDTH)
    ub = u.astype(BF16)

    def project_in(j):
        x = jnp.dot(ub[:, j * SSM_BLOCK_IN:(j + 1) * SSM_BLOCK_IN], wb_ref[j],
                    preferred_element_type=F32)
        xr_ref[j] = x[:, :SSM_BLOCK_STATE]
        xi_ref[j] = x[:, SSM_BLOCK_STATE:]

    def scan(j):
        cols = slice(j * SSM_BLOCK_STATE, (j + 1) * SSM_BLOCK_STATE)
        lr = lamr_ref[:, cols]
        li = lami_ref[:, cols]
        s_re = sr_ref[:, cols]
        s_im = si_ref[:, cols]
        for tt in range(tc):
            rows = slice(tt * nb, (tt + 1) * nb)
            s_re, s_im = (lr * s_re - li * s_im + xr_ref[j, rows, :],
                          lr * s_im + li * s_re + xi_ref[j, rows, :])
            xr_ref[j, rows, :] = s_re
            xi_ref[j, rows, :] = s_im
        sr_ref[:, cols] = s_re
        si_ref[:, cols] = s_im

    def project_out(j):
        return (jnp.dot(xr_ref[j].astype(BF16), wcr_ref[j], preferred_element_type=F32)
                - jnp.dot(xi_ref[j].astype(BF16), wci_ref[j], preferred_element_type=F32))

    ys = []
    project_in(0)
    for j in range(SSM_BLOCKS):
        if j + 1 < SSM_BLOCKS:
            project_in(j + 1)
        scan(j)
        ys.append(project_out(j))
    y = jnp.concatenate(ys, axis=-1) + d_ref[...] * u
    y = _gelu(y)
    z = jnp.dot(y.astype(BF16), gluw_ref[...], preferred_element_type=F32) + glub_ref[...]
    out = (y * jax.nn.sigmoid(z)).reshape(tc, nb, SSM_WIDTH)
    y_ref[...] = jnp.swapaxes(out, 0, 1).astype(BF16)

    @pl.when(t == pl.num_programs(0) - 1)
    def _():
        sfr_ref[...] = sr_ref[...]
        sfi_ref[...] = si_ref[...]


def _ssm(u, s0r, s0i, wb, lamr, lami, wcr, wci, d, gluw, glub, *, tc):
    nb, seq, _ = u.shape
    r = tc * nb
    nsteps = seq // tc
    ns = SSM_GROUPS * SSM_STATE
    kern = functools.partial(_ssm_kernel, tc=tc, nb=nb)
    const2 = lambda t: (0, 0)
    const3 = lambda t: (0, 0, 0)
    return pl.pallas_call(
        kern,
        grid=(nsteps,),
        in_specs=[
            pl.BlockSpec((nb, tc, SSM_WIDTH), lambda t: (0, t, 0)),
            pl.BlockSpec((nb, ns), const2),
            pl.BlockSpec((nb, ns), const2),
            pl.BlockSpec((SSM_BLOCKS, SSM_BLOCK_IN, 2 * SSM_BLOCK_STATE), const3),
            pl.BlockSpec((nb, ns), const2),
            pl.BlockSpec((nb, ns), const2),
            pl.BlockSpec((SSM_BLOCKS, SSM_BLOCK_STATE, SSM_BLOCK_IN), const3),
            pl.BlockSpec((SSM_BLOCKS, SSM_BLOCK_STATE, SSM_BLOCK_IN), const3),
            pl.BlockSpec((1, SSM_WIDTH), const2),
            pl.BlockSpec((SSM_WIDTH, SSM_WIDTH), const2),
            pl.BlockSpec((1, SSM_WIDTH), const2),
        ],
        out_specs=[
            pl.BlockSpec((nb, tc, SSM_WIDTH), lambda t: (0, t, 0)),
            pl.BlockSpec((nb, ns), const2),
            pl.BlockSpec((nb, ns), const2),
        ],
        out_shape=(
            jax.ShapeDtypeStruct((nb, seq, SSM_WIDTH), BF16),
            jax.ShapeDtypeStruct((nb, ns), F32),
            jax.ShapeDtypeStruct((nb, ns), F32),
        ),
        scratch_shapes=[
            pltpu.VMEM((SSM_BLOCKS, r, SSM_BLOCK_STATE), F32),
            pltpu.VMEM((SSM_BLOCKS, r, SSM_BLOCK_STATE), F32),
            pltpu.VMEM((nb, ns), F32),
            pltpu.VMEM((nb, ns), F32),
        ],
        compiler_params=pltpu.CompilerParams(
            dimension_semantics=("arbitrary",), vmem_limit_bytes=VMEM_LIMIT),
        name="ssm",
    )(u, s0r, s0i, wb, lamr, lami, wcr, wci, d, gluw, glub)


def _attn_kernel(slopes_ref, *rest, ta, tk, n_tiles, has_prefix, lam_init):
    qt_ref, k_ref, vt_ref = rest[:3]
    rest = rest[3:]
    if has_prefix:
        kh_ref, vht_ref = rest[:2]
        rest = rest[2:]
    (kb_ref, ones_ref, lq1_ref, lk1_ref, lq2_ref, lk2_ref, subg_ref,
     o_ref, q2_ref, sa_ref, sb_ref, pa_ref, pb_ref, acc_ref, m_ref, alpha_ref) = rest

    h = pl.program_id(1)
    g = pl.program_id(2) if pl.num_programs(2) > 1 else 0
    slope = slopes_ref[h]
    nsub = ta // tk
    nq = k_ref.shape[0] // ta
    tile_idx = [g + t * (nq // n_tiles) for t in range(n_tiles)]
    width = 2 * ta
    total = n_tiles * width

    for t in range(n_tiles):
        qt = qt_ref[t]
        first = lax.broadcasted_iota(jnp.int32, qt.shape, 0) < QK_DIM
        zero = jnp.zeros_like(qt)
        q2_ref[:V_DIM, t * width:t * width + ta] = jnp.where(first, qt, zero)
        q2_ref[:V_DIM, t * width + ta:(t + 1) * width] = jnp.where(first, zero, qt)
    slope2 = jnp.full((1, total), slope * LOG2E, F32)
    c_hi = slope2.astype(BF16).astype(F32)
    c_mid = (slope2 - c_hi).astype(BF16).astype(F32)
    c_lo = slope2 - c_hi - c_mid
    bias_row = lax.broadcasted_iota(jnp.int32, (V_DIM, total), 0)
    q2_ref[V_DIM:, :] = jnp.where(
        bias_row == 0, c_hi,
        jnp.where(bias_row == 1, c_mid, jnp.where(bias_row == 2, c_lo, 0.0))).astype(BF16)
    slope = slope * LOG2E

    def key_lhs(k_blk):
        return jnp.concatenate([k_blk, kb_ref[:k_blk.shape[0], :]], axis=1)

    def value_lhs(vt_blk):
        return jnp.concatenate([vt_blk, ones_ref[:, :vt_blk.shape[1]]], axis=0)

    def key_block(kidx):
        if isinstance(kidx, int):
            return k_ref[kidx * tk:(kidx + 1) * tk, :]
        return k_ref[pl.ds(pl.multiple_of(kidx * tk, tk), tk), :]

    def clamp0(n):
        return max(n, 0) if isinstance(n, int) else jnp.maximum(n, 0)

    def f32(n):
        return float(n) if isinstance(n, int) else n.astype(F32)

    def softmax_step(m_old, s, shift):
        m_new = jnp.maximum(m_old, jnp.max(s, axis=0, keepdims=True) - shift)
        alpha = jnp.exp2(m_old - m_new)
        p = jnp.exp2(s - (m_new + shift))
        return m_new, alpha, p.astype(BF16)

    sa_ref[...] = jnp.dot(key_lhs(key_block(0)), q2_ref[...], preferred_element_type=F32)
    pb_ref[...] = jnp.zeros(pb_ref.shape, BF16)
    acc_ref[...] = jnp.zeros(acc_ref.shape, F32)
    m_ref[...] = jnp.full((1, total), NEG_INF, F32)
    alpha_ref[...] = jnp.ones((1, total), F32)

    def make_body(col_tiles):
        def body(j, carry):
            n0 = 2 * j
            k_lhs = [key_lhs(key_block(n0 + st + 1)) for st in range(2)]
            v_lhs = [value_lhs(vt_ref[clamp0(n0 - 1)]), value_lhs(vt_ref[n0])]
            s_bufs = (sa_ref, sb_ref)
            p_bufs = (pa_ref, pb_ref)
            for ct in col_tiles:
                cs = slice(ct * COL_TILE, (ct + 1) * COL_TILE)
                q0 = tile_idx[ct * COL_TILE // width] * ta
                shift0 = slope * f32(q0 - n0 * tk)
                m_c, alpha_c, acc = m_ref[:, cs], alpha_ref[:, cs], acc_ref[:, cs]
                for st in range(2):
                    s_bufs[(st + 1) % 2][:, cs] = jnp.dot(
                        k_lhs[st], q2_ref[:, cs], preferred_element_type=F32)
                    m_c, alpha_n, p = softmax_step(
                        m_c, s_bufs[st % 2][:, cs], shift0 - slope * (st * tk))
                    p_bufs[st % 2][:, cs] = p
                    acc = acc * alpha_c + jnp.dot(
                        v_lhs[st], p_bufs[(st + 1) % 2][:, cs], preferred_element_type=F32)
                    alpha_c = alpha_n
                acc_ref[:, cs] = acc
                m_ref[:, cs] = m_c
                alpha_ref[:, cs] = alpha_c
            return carry
        return body

    if nsub == 2:
        per_tile = width // COL_TILE
        all_ct = list(range(total // COL_TILE))
        for t in range(n_tiles):
            lo = tile_idx[t - 1] if t else 0
            body = make_body(all_ct[t * per_tile:])
            if isinstance(lo, int):
                for j in range(lo, tile_idx[t]):
                    body(j, 0)
            else:
                lax.fori_loop(lo, tile_idx[t], body, 0)
    else:
        assert nsub == 1 and n_tiles == 1 and nq == 1

    dw = min(DIAG_TILE, width)
    n_dt = total // dw
    tiles = [slice(dt * dw, (dt + 1) * dw) for dt in range(n_dt)]
    owner = [dt * dw // width for dt in range(n_dt)]
    q_base = [dt * dw % width for dt in range(n_dt)]
    q_his = [min(q_base[dt] % ta + dw, ta) - 1 for dt in range(n_dt)]
    k_idx = lax.broadcasted_iota(jnp.int32, (tk, dw), 0)
    col = lax.broadcasted_iota(jnp.int32, (tk, dw), 1)
    if has_prefix:
        k_pre = key_lhs(kh_ref[...])[N_JUNK:, :]
        s_pre = [jnp.dot(k_pre, q2_ref[:, cs], preferred_element_type=F32) for cs in tiles]
    v_pend = [value_lhs(vt_ref[clamp0(ti * nsub - 1)]) for ti in tile_idx]
    m_t = [m_ref[:, cs] for cs in tiles]
    acc_t = [acc_ref[:, cs] * alpha_ref[:, cs]
             + jnp.dot(v_pend[owner[dt]], pb_ref[:, cs], preferred_element_type=F32)
             for dt, cs in enumerate(tiles)]
    s_t = {dt: sa_ref[:, cs] for dt, cs in enumerate(tiles)}
    masks = {}
    for c in range(nsub):
        live = [dt for dt in range(n_dt) if q_his[dt] >= c * tk]
        if c + 1 < nsub:
            k_next = [key_lhs(key_block(ti * nsub + c + 1)) for ti in tile_idx]
            s_next = {dt: jnp.dot(k_next[owner[dt]], q2_ref[:, tiles[dt]],
                                  preferred_element_type=F32)
                      for dt in live if q_his[dt] >= (c + 1) * tk}
        p_t = {}
        for dt in live:
            s = s_t[dt]
            lead = q_base[dt] % ta - c * tk
            causal = lead < tk - 1 or dw > ta
            if causal or not has_prefix:
                key = (lead if causal else None, dw > ta)
                if key not in masks:
                    q_idx = (col & (ta - 1)) if dw > ta else col
                    keep = q_idx - k_idx >= -lead if causal else None
                    if not has_prefix:
                        junk = k_idx >= N_JUNK - c * tk
                        keep = junk if keep is None else jnp.logical_and(keep, junk)
                    masks[key] = keep
                s = jnp.where(masks[key], s, NEG_INF)
            m_t[dt], alpha_c, p_t[dt] = softmax_step(m_t[dt], s, slope * (-c * tk))
            acc_t[dt] = acc_t[dt] * alpha_c
        v_c = [value_lhs(vt_ref[ti * nsub + c]) for ti in tile_idx]
        for dt in live:
            acc_t[dt] = acc_t[dt] + jnp.dot(v_c[owner[dt]], p_t[dt],
                                            preferred_element_type=F32)
        if c + 1 < nsub:
            s_t = s_next
    if has_prefix:
        v_pre = value_lhs(vht_ref[...])
        no_weight = jnp.zeros((N_JUNK, dw), BF16)
        for dt in range(n_dt):
            shift = slope * f32(HEAD_ROWS + tile_idx[owner[dt]] * ta)
            m_t[dt], alpha_c, p = softmax_step(m_t[dt], s_pre[dt], shift)
            p = jnp.concatenate([no_weight, p], axis=0)
            acc_t[dt] = acc_t[dt] * alpha_c + jnp.dot(v_pre, p, preferred_element_type=F32)

    lam = (jnp.exp(jnp.sum(lq1_ref[...] * lk1_ref[...], axis=-1, keepdims=True))
           - jnp.exp(jnp.sum(lq2_ref[...] * lk2_ref[...], axis=-1, keepdims=True))
           + lam_init)
    per = width // dw
    for t in range(n_tiles):
        acc = jnp.concatenate(acc_t[t * per:(t + 1) * per], axis=1)
        l = acc[V_DIM:V_DIM + 1, :]
        acc = acc[:V_DIM, :]
        o = acc[:, :ta] / l[:, :ta] - lam * (acc[:, ta:] / l[:, ta:])
        r = lax.rsqrt(jnp.mean(o * o, axis=0, keepdims=True) + EPS)
        y = ((o * r) * subg_ref[...]) * (1.0 - lam_init)
        o_ref[t] = y.T.astype(BF16)


def _attention(slopes, qt, k, vt, prefix, lq1, lk1, lq2, lk2, subg, *, ta, tk, lam_init):
    b, l, _ = k.shape
    nq = l // ta
    nkv = l // tk
    assert ta & (ta - 1) == 0 and ta % tk == 0
    has_prefix = prefix is not None
    n_tiles = math.gcd(nq, TILES_PER_STEP)
    steps = nq // n_tiles
    kern = functools.partial(_attn_kernel, ta=ta, tk=tk, n_tiles=n_tiles,
                             has_prefix=has_prefix, lam_init=lam_init)
    qt = qt.reshape(b, n_tiles, steps, QK_COLS, ta)
    in_specs = [
        pl.BlockSpec(memory_space=pltpu.SMEM),
        pl.BlockSpec((None, n_tiles, None, V_DIM, ta), lambda bi, h, g: (bi, 0, g, h, 0)),
        pl.BlockSpec((None, l, V_DIM), lambda bi, h, g: (bi, 0, h)),
        pl.BlockSpec((None, nkv, V_DIM, tk), lambda bi, h, g: (bi, 0, h, 0)),
    ]
    args = [slopes, qt, k, vt]
    if has_prefix:
        kh, vht = prefix
        assert tk >= HEAD_ROWS
        in_specs += [
            pl.BlockSpec((None, HEAD_ROWS, V_DIM), lambda bi, h, g: (0, 0, h)),
            pl.BlockSpec((None, None, V_DIM, HEAD_ROWS), lambda bi, h, g: (0, 0, h, 0)),
        ]
        args += [kh, vht]
    const2 = lambda bi, h, g: (0, 0)
    in_specs += [
        pl.BlockSpec((tk, V_DIM), const2),
        pl.BlockSpec((SUM_ROWS, tk), const2),
        pl.BlockSpec((1, QK_DIM), const2), pl.BlockSpec((1, QK_DIM), const2),
        pl.BlockSpec((1, QK_DIM), const2), pl.BlockSpec((1, QK_DIM), const2),
        pl.BlockSpec((V_DIM, 1), const2),
    ]
    assert tk <= 256
    kb = jnp.zeros((tk, V_DIM), F32).at[:, :3].set(jnp.arange(tk, dtype=F32)[:, None])
    ones = jnp.zeros((SUM_ROWS, tk), F32).at[0].set(1.0)
    args += [kb.astype(BF16), ones.astype(BF16), lq1, lk1, lq2, lk2, subg]
    total = n_tiles * 2 * ta
    out = pl.pallas_call(
        kern,
        grid=(b, ATT_HEADS, steps),
        in_specs=in_specs,
        out_specs=pl.BlockSpec((None, n_tiles, ta, V_DIM), lambda bi, h, g: (bi, 0, g, h)),
        out_shape=jax.ShapeDtypeStruct((b, n_tiles, l // n_tiles, ATT_WIDTH), BF16),
        scratch_shapes=[
            pltpu.VMEM((2 * V_DIM, total), BF16),
            pltpu.VMEM((tk, total), F32),
            pltpu.VMEM((tk, total), F32),
            pltpu.VMEM((tk, total), BF16),
            pltpu.VMEM((tk, total), BF16),
            pltpu.VMEM((V_DIM + SUM_ROWS, total), F32),
            pltpu.VMEM((1, total), F32),
            pltpu.VMEM((1, total), F32),
        ],
        compiler_params=pltpu.CompilerParams(
            dimension_semantics=("parallel", "parallel", "arbitrary"),
            vmem_limit_bytes=VMEM_LIMIT),
        name="diff_attention",
    )(*args)
    return out.reshape(b, l, ATT_WIDTH)


def _merge_kernel(x_ref, ys_ref, ya_ref, g1_ref, wg_ref, wso_ref, wao_ref, wo_ref, h1_ref):
    x = x_ref[...]
    hn = _rmsnorm(x, g1_ref[...]).astype(BF16)
    gates = jnp.dot(hn, wg_ref[...], preferred_element_type=F32)
    a = jnp.dot(ys_ref[...], wso_ref[...], preferred_element_type=F32)
    c = jnp.dot(ya_ref[...], wao_ref[...], preferred_element_type=F32)
    mixed = (jax.nn.sigmoid(gates[:, :D_MODEL]) * a
             + jax.nn.sigmoid(gates[:, D_MODEL:]) * c)
    h1_ref[...] = x + jnp.dot(mixed.astype(BF16), wo_ref[...], preferred_element_type=F32)


def _merge(x, ys, ya, g1, wg, wso, wao, wo, *, t):
    b, l, _ = x.shape
    const2 = lambda bi, i: (0, 0)
    row = lambda bi, i: (bi, i, 0)
    return pl.pallas_call(
        _merge_kernel,
        grid=(b, l // t),
        in_specs=[
            pl.BlockSpec((None, t, D_MODEL), row),
            pl.BlockSpec((None, t, SSM_WIDTH), row),
            pl.BlockSpec((None, t, ATT_WIDTH), row),
            pl.BlockSpec((1, D_MODEL), const2),
            pl.BlockSpec((D_MODEL, 2 * D_MODEL), const2, pipeline_mode=pl.Buffered(1)),
            pl.BlockSpec((SSM_WIDTH, D_MODEL), const2, pipeline_mode=pl.Buffered(1)),
            pl.BlockSpec((ATT_WIDTH, D_MODEL), const2, pipeline_mode=pl.Buffered(1)),
            pl.BlockSpec((D_MODEL, D_MODEL), const2, pipeline_mode=pl.Buffered(1)),
        ],
        out_specs=pl.BlockSpec((None, t, D_MODEL), row),
        out_shape=jax.ShapeDtypeStruct((b, l, D_MODEL), F32),
        compiler_params=pltpu.CompilerParams(
            dimension_semantics=("parallel", "parallel"), vmem_limit_bytes=VMEM_LIMIT),
        name="merge",
    )(x, ys, ya, g1, wg, wso, wao, wo)


def _ffn_kernel(h_ref, hprev_ref, hhead_ref, g2_ref, wa_ref, wg_ref, cw_ref, cb_ref, wdn_ref,
                out_ref, hn_ref, gate_ref):
    i = pl.program_id(1)
    t = h_ref.shape[0]
    h = h_ref[...]
    g2 = g2_ref[...]
    halo = jnp.where(i == 0, hhead_ref[...], hprev_ref[...])
    hn_ref[:HALO, :] = _rmsnorm(halo, g2).astype(BF16)
    hn_ref[HALO:, :] = _rmsnorm(h, g2).astype(BF16)
    hn = hn_ref[...]
    for f in range(N_FF_CHUNKS):
        cols = slice(f * FF_CHUNK, (f + 1) * FF_CHUNK)
        a = jnp.dot(hn, wa_ref[:, cols], preferred_element_type=F32)
        gate = jnp.dot(hn[HALO:], wg_ref[:, cols], preferred_element_type=F32)
        c = (a[HALO - 2:HALO - 2 + t] * cw_ref[0:1, cols]
             + a[HALO - 1:HALO - 1 + t] * cw_ref[1:2, cols]
             + a[HALO:] * cw_ref[2:3, cols] + cb_ref[:, cols])
        gate_ref[:, cols] = (_gelu(c) * gate).astype(BF16)
    out_ref[...] = h + jnp.dot(gate_ref[...], wdn_ref[...], preferred_element_type=F32)


def _ffn(h1, h1_head, g2, wa, wg, cw, cb, wdn, *, t):
    b, l, _ = h1.shape
    const2 = lambda bi, i: (0, 0)
    row = lambda bi, i: (bi, i, 0)
    per = t // HALO
    return pl.pallas_call(
        _ffn_kernel,
        grid=(b, l // t),
        in_specs=[
            pl.BlockSpec((None, t, D_MODEL), row),
            pl.BlockSpec((None, HALO, D_MODEL),
                         lambda bi, i: (bi, jnp.maximum(i * per - 1, 0), 0)),
            pl.BlockSpec((None, HALO, D_MODEL),
                         lambda bi, i: (0, HEAD_ROWS // HALO - 1, 0)),
            pl.BlockSpec((1, D_MODEL), const2),
            pl.BlockSpec((D_MODEL, D_FF), const2, pipeline_mode=pl.Buffered(1)),
            pl.BlockSpec((D_MODEL, D_FF), const2, pipeline_mode=pl.Buffered(1)),
            pl.BlockSpec((3, D_FF), const2),
            pl.BlockSpec((1, D_FF), const2),
            pl.BlockSpec((D_FF, D_MODEL), const2, pipeline_mode=pl.Buffered(1)),
        ],
        out_specs=pl.BlockSpec((None, t, D_MODEL), row),
        out_shape=jax.ShapeDtypeStruct((b, l, D_MODEL), F32),
        scratch_shapes=[
            pltpu.VMEM((HALO + t, D_MODEL), BF16),
            pltpu.VMEM((t, D_FF), BF16),
        ],
        compiler_params=pltpu.CompilerParams(
            dimension_semantics=("parallel", "arbitrary"), vmem_limit_bytes=VMEM_LIMIT),
        name="conv_ffn",
    )(h1, h1, h1_head, g2, wa, wg, cw, cb, wdn)


def _pick_tile(l, pref):
    t = min(pref, l)
    while l % t:
        t //= 2
    return t


def _layer(x, head, p, l_idx):
    bsz, seq, _ = x.shape
    lam_init = 0.8 - 0.6 * math.exp(-0.3 * l_idx)
    slopes = 2.0 ** (-8.0 * jnp.arange(1, ATT_HEADS + 1, dtype=F32) / ATT_HEADS)

    g1 = p['norm1_g'][l_idx][None]
    w_in = p['w_in'][l_idx]
    w_mix = w_in[:, :MIX_COLS].astype(BF16)
    w_gate = w_in[:, MIX_COLS:].astype(BF16)
    qg = jnp.tile(p['q_norm_g'][l_idx], QK_COLS // QK_DIM)[None]
    kg = jnp.tile(p['k_norm_g'][l_idx], QK_COLS // QK_DIM)[None]

    lam_re, lam_im, bbt_re, bbt_im = _ssm_prep(
        p['ssm_a_re'][l_idx], p['ssm_a_im'][l_idx], p['ssm_log_dt'][l_idx],
        p['ssm_b_re'][l_idx], p['ssm_b_im'][l_idx])
    wb = jnp.concatenate([_block_diag(bbt_re), _block_diag(bbt_im)], axis=-1).astype(BF16)
    wcr = _block_diag(jnp.swapaxes(p['ssm_c_re'][l_idx], 1, 2)).astype(BF16)
    wci = _block_diag(jnp.swapaxes(p['ssm_c_im'][l_idx], 1, 2)).astype(BF16)
    ns = SSM_GROUPS * SSM_STATE
    lamr = jnp.broadcast_to(lam_re.reshape(1, ns), (bsz, ns))
    lami = jnp.broadcast_to(lam_im.reshape(1, ns), (bsz, ns))
    d = p['ssm_d'][l_idx].reshape(1, SSM_WIDTH)
    gluw = p['ssm_glu_w'][l_idx].astype(BF16)
    glub = p['ssm_glu_b'][l_idx][None]

    lq1, lk1 = p['lam_q1'][l_idx][None], p['lam_k1'][l_idx][None]
    lq2, lk2 = p['lam_q2'][l_idx][None], p['lam_k2'][l_idx][None]
    subg = p['subln_g'][l_idx][:, None]
    wso = p['w_ssm_out'][l_idx].astype(BF16)
    wao = p['w_att_out'][l_idx].astype(BF16)
    wo = p['w_o'][l_idx].astype(BF16)

    t_row = _pick_tile(seq, 1024)
    ta = _pick_tile(seq, 512)
    tk = _pick_tile(seq, 256)
    tc = _pick_tile(seq, 64)

    u_h, qt_h, k_h, vt_h = _inproj(head, g1, w_mix, qg, kg, t=HEAD_ROWS, ta=HEAD_ROWS,
                                    tk=HEAD_ROWS)
    zeros_state = jnp.zeros((bsz, ns), F32)
    u_hb = jnp.broadcast_to(u_h, (bsz,) + u_h.shape[1:])
    ys_hb, s0r, s0i = _ssm(u_hb, zeros_state, zeros_state, wb, lamr, lami, wcr, wci,
                           d, gluw, glub, tc=_pick_tile(HEAD_ROWS, 64))
    ys_h = ys_hb[:1]
    ya_h = _attention(slopes, qt_h, k_h, vt_h, None, lq1, lk1, lq2, lk2, subg,
                      ta=HEAD_ROWS, tk=HEAD_ROWS, lam_init=lam_init)
    h1_head = _merge(head, ys_h, ya_h, g1, w_gate, wso, wao, wo, t=HEAD_ROWS)

    u, qt, k, vt = _inproj(x, g1, w_mix, qg, kg, t=t_row, ta=ta, tk=tk)
    ys, _, _ = _ssm(u, s0r, s0i, wb, lamr, lami, wcr, wci, d, gluw, glub, tc=tc)
    ya = _attention(slopes, qt, k, vt, (k_h, vt_h), lq1, lk1, lq2, lk2, subg,
                    ta=ta, tk=tk, lam_init=lam_init)
    h1 = _merge(x, ys, ya, g1, w_gate, wso, wao, wo, t=t_row)

    w_up = p['w_up'][l_idx]
    out = _ffn(h1, h1_head, p['norm2_g'][l_idx][None], w_up[:, :D_FF].astype(BF16),
               w_up[:, D_FF:].astype(BF16), p['conv_w'][l_idx],
               p['conv_b'][l_idx][None], p['w_down'][l_idx].astype(BF16), t=t_row)
    return out, h1_head


def kernel(x, meta_tokens, norm1_g, w_in, ssm_a_re, ssm_a_im, ssm_log_dt, ssm_b_re, ssm_b_im,
           ssm_c_re, ssm_c_im, ssm_d, ssm_glu_w, ssm_glu_b, q_norm_g, k_norm_g,
           lam_q1, lam_k1, lam_q2, lam_k2, subln_g, w_ssm_out, w_att_out, w_o,
           norm2_g, w_up, conv_w, conv_b, w_down):
    params = dict(norm1_g=norm1_g, w_in=w_in, ssm_a_re=ssm_a_re, ssm_a_im=ssm_a_im,
                  ssm_log_dt=ssm_log_dt, ssm_b_re=ssm_b_re, ssm_b_im=ssm_b_im,
                  ssm_c_re=ssm_c_re, ssm_c_im=ssm_c_im, ssm_d=ssm_d, ssm_glu_w=ssm_glu_w,
                  ssm_glu_b=ssm_glu_b, q_norm_g=q_norm_g, k_norm_g=k_norm_g,
                  lam_q1=lam_q1, lam_k1=lam_k1, lam_q2=lam_q2, lam_k2=lam_k2,
                  subln_g=subln_g, w_ssm_out=w_ssm_out, w_att_out=w_att_out, w_o=w_o,
                  norm2_g=norm2_g, w_up=w_up, conv_w=conv_w, conv_b=conv_b, w_down=w_down)
    depth = norm1_g.shape[0]
    assert depth == 1, "the head tile is only carried through one layer"
    head = jnp.concatenate(
        [jnp.zeros((N_JUNK, D_MODEL), x.dtype), meta_tokens.astype(x.dtype)], axis=0)[None]
    out, _ = _layer(x, head, params, 0)
    return out
```

```python
import functools
import math

import jax
import jax.numpy as jnp
from jax import lax
from jax.experimental import pallas as pl
from jax.experimental.pallas import tpu as pltpu

F32 = jnp.float32
BF16 = jnp.bfloat16

D_MODEL = 1024
N_META = 16
EPS = 1e-6
NEG_INF = -1e30

SSM_WIDTH = 512
SSM_GROUP = 16
SSM_GROUPS = 32
SSM_STATE = 64
SSM_BLOCKS = 4
SSM_BLOCK_IN = SSM_WIDTH // SSM_BLOCKS
SSM_BLOCK_STATE = SSM_GROUPS * SSM_STATE // SSM_BLOCKS

ATT_HEADS = 4
QK_DIM = 64
V_DIM = 128
ATT_WIDTH = 512
QK_COLS = 512
MIX_COLS = SSM_WIDTH + 2 * QK_COLS + ATT_WIDTH

D_FF = 2816
FF_CHUNK = 256
N_FF_CHUNKS = D_FF // FF_CHUNK

HEAD_ROWS = 128
N_JUNK = HEAD_ROWS - N_META
SUM_ROWS = 16
LOG2E = math.log2(math.e)
DIAG_TILE = 256
TILES_PER_STEP = 8
COL_TILE = 256
HALO = 16

VMEM_LIMIT = 56 * 1024 * 1024


def _rmsnorm(x, g):
    r = lax.rsqrt(jnp.mean(x * x, axis=-1, keepdims=True) + EPS)
    return (x * r) * g


def _gelu(x):
    return jax.nn.gelu(x, approximate=True)


def _ssm_prep_kernel(are_ref, aim_ref, ldt_ref, btr_ref, bti_ref,
                     lr_ref, li_ref, bbr_ref, bbi_ref):
    a_re = are_ref[...]
    a_im = aim_ref[...]
    dt = jnp.exp(ldt_ref[...])
    mag = jnp.exp(a_re * dt)
    lb_re = mag * jnp.cos(a_im * dt)
    lb_im = mag * jnp.sin(a_im * dt)
    den = a_re * a_re + a_im * a_im
    n_re = lb_re - 1.0
    f_re = (n_re * a_re + lb_im * a_im) / den
    f_im = (lb_im * a_re - n_re * a_im) / den
    lr_ref[...] = lb_re
    li_ref[...] = lb_im
    bt_re = btr_ref[...]
    bt_im = bti_ref[...]
    bbr_ref[...] = f_re[:, None, :] * bt_re - f_im[:, None, :] * bt_im
    bbi_ref[...] = f_re[:, None, :] * bt_im + f_im[:, None, :] * bt_re


def _ssm_prep(a_re, a_im, log_dt, b_re, b_im):
    g, p, c = b_re.shape
    return pl.pallas_call(
        _ssm_prep_kernel,
        out_shape=(jax.ShapeDtypeStruct((g, p), F32), jax.ShapeDtypeStruct((g, p), F32),
                   jax.ShapeDtypeStruct((g, c, p), F32), jax.ShapeDtypeStruct((g, c, p), F32)),
        name="ssm_prep",
    )(a_re, a_im, log_dt.reshape(g, 1), jnp.swapaxes(b_re, 1, 2), jnp.swapaxes(b_im, 1, 2))


def _block_diag(m):
    gpb = SSM_GROUPS // SSM_BLOCKS
    _, r, c = m.shape
    m4 = m.reshape(SSM_BLOCKS, gpb, r, c)
    eye = jnp.eye(gpb, dtype=m.dtype)
    return jnp.einsum('jgrc,gh->jgrhc', m4, eye).reshape(SSM_BLOCKS, gpb * r, gpb * c)


def _qk_norm(q, g):
    lo = lax.broadcasted_iota(jnp.int32, (1, 128), 1) < QK_DIM
    outs = []
    for c in range(QK_COLS // 128):
        blk = q[:, c * 128:(c + 1) * 128]
        sq = blk * blk
        s_lo = jnp.sum(jnp.where(lo, sq, 0.0), axis=-1, keepdims=True)
        s_hi = jnp.sum(jnp.where(lo, 0.0, sq), axis=-1, keepdims=True)
        r_lo = lax.rsqrt(s_lo / QK_DIM + EPS)
        r_hi = lax.rsqrt(s_hi / QK_DIM + EPS)
        outs.append(blk * jnp.where(lo, r_lo, r_hi))
    return jnp.concatenate(outs, axis=-1) * g


def _inproj_kernel(x_ref, g1_ref, w_ref, qg_ref, kg_ref,
                   u_ref, qt_ref, k_ref, vt_ref, *, ta, tk):
    x = x_ref[...]
    hn = _rmsnorm(x, g1_ref[...]).astype(BF16)

    def proj(c0, width):
        return jnp.dot(hn, w_ref[:, c0:c0 + width], preferred_element_type=F32)

    q = proj(SSM_WIDTH, QK_COLS)
    k = proj(SSM_WIDTH + QK_COLS, QK_COLS)
    qn = _qk_norm(q, qg_ref[...]) * (QK_DIM ** -0.5 * LOG2E)
    for c in range(x.shape[0] // ta):
        qt_ref[c] = qn[c * ta:(c + 1) * ta, :].T.astype(BF16)
    v = proj(SSM_WIDTH + 2 * QK_COLS, ATT_WIDTH)
    k_ref[...] = _qk_norm(k, kg_ref[...]).astype(BF16)
    u_ref[...] = proj(0, SSM_WIDTH)
    for c in range(x.shape[0] // tk):
        vt_ref[c] = v[c * tk:(c + 1) * tk, :].T.astype(BF16)


def _inproj(x, g1, w_mix, qg, kg, *, t, ta, tk):
    b, l, _ = x.shape
    nt = l // t
    kern = functools.partial(_inproj_kernel, ta=ta, tk=tk)
    return pl.pallas_call(
        kern,
        grid=(b, nt),
        in_specs=[
            pl.BlockSpec((None, t, D_MODEL), lambda bi, i: (bi, i, 0)),
            pl.BlockSpec((1, D_MODEL), lambda bi, i: (0, 0)),
            pl.BlockSpec((D_MODEL, MIX_COLS), lambda bi, i: (0, 0)),
            pl.BlockSpec((1, QK_COLS), lambda bi, i: (0, 0)),
            pl.BlockSpec((1, QK_COLS), lambda bi, i: (0, 0)),
        ],
        out_specs=[
            pl.BlockSpec((None, t, SSM_WIDTH), lambda bi, i: (bi, i, 0)),
            pl.BlockSpec((None, t // ta, QK_COLS, ta), lambda bi, i: (bi, i, 0, 0)),
            pl.BlockSpec((None, t, QK_COLS), lambda bi, i: (bi, i, 0)),
            pl.BlockSpec((None, t // tk, ATT_WIDTH, tk), lambda bi, i: (bi, i, 0, 0)),
        ],
        out_shape=(
            jax.ShapeDtypeStruct((b, l, SSM_WIDTH), F32),
            jax.ShapeDtypeStruct((b, l // ta, QK_COLS, ta), BF16),
            jax.ShapeDtypeStruct((b, l, QK_COLS), BF16),
            jax.ShapeDtypeStruct((b, l // tk, ATT_WIDTH, tk), BF16),
        ),
        compiler_params=pltpu.CompilerParams(
            dimension_semantics=("parallel", "parallel"), vmem_limit_bytes=VMEM_LIMIT),
        name="inproj",
    )(x, g1, w_mix, qg, kg)


def _ssm_kernel(u_ref, s0r_ref, s0i_ref, wb_ref, lamr_ref, lami_ref, wcr_ref, wci_ref,
                d_ref, gluw_ref, glub_ref,
                y_ref, sfr_ref, sfi_ref,
                xr_ref, xi_ref, sr_ref, si_ref, *, tc, nb):
    t = pl.program_id(0)

    @pl.when(t == 0)
    def _():
        sr_ref[...] = s0r_ref[...]
        si_ref[...] = s0i_ref[...]

    u = jnp.swapaxes(u_ref[...], 0, 1).reshape(tc * nb, SSM_WIDTH)
    ub = u.astype(BF16)

    def project_in(j):
        x = jnp.dot(ub[:, j * SSM_BLOCK_IN:(j + 1) * SSM_BLOCK_IN], wb_ref[j],
                    preferred_element_type=F32)
        xr_ref[j] = x[:, :SSM_BLOCK_STATE]
        xi_ref[j] = x[:, SSM_BLOCK_STATE:]

    def scan(j):
        cols = slice(j * SSM_BLOCK_STATE, (j + 1) * SSM_BLOCK_STATE)
        lr = lamr_ref[:, cols]
        li = lami_ref[:, cols]
        s_re = sr_ref[:, cols]
        s_im = si_ref[:, cols]
        for tt in range(tc):
            rows = slice(tt * nb, (tt + 1) * nb)
            s_re, s_im = (lr * s_re - li * s_im + xr_ref[j, rows, :],
                          lr * s_im + li * s_re + xi_ref[j, rows, :])
            xr_ref[j, rows, :] = s_re
            xi_ref[j, rows, :] = s_im
        sr_ref[:, cols] = s_re
        si_ref[:, cols] = s_im

    def project_out(j):
        return (jnp.dot(xr_ref[j].astype(BF16), wcr_ref[j], preferred_element_type=F32)
                - jnp.dot(xi_ref[j].astype(BF16), wci_ref[j], preferred_element_type=F32))

    ys = []
    project_in(0)
    for j in range(SSM_BLOCKS):
        if j + 1 < SSM_BLOCKS:
            project_in(j + 1)
        scan(j)
        ys.append(project_out(j))
    y = jnp.concatenate(ys, axis=-1) + d_ref[...] * u
    y = _gelu(y)
    z = jnp.dot(y.astype(BF16), gluw_ref[...], preferred_element_type=F32) + glub_ref[...]
    out = (y * jax.nn.sigmoid(z)).reshape(tc, nb, SSM_WIDTH)
    y_ref[...] = jnp.swapaxes(out, 0, 1).astype(BF16)

    @pl.when(t == pl.num_programs(0) - 1)
    def _():
        sfr_ref[...] = sr_ref[...]
        sfi_ref[...] = si_ref[...]


def _ssm(u, s0r, s0i, wb, lamr, lami, wcr, wci, d, gluw, glub, *, tc):
    nb, seq, _ = u.shape
    r = tc * nb
    nsteps = seq // tc
    ns = SSM_GROUPS * SSM_STATE
    kern = functools.partial(_ssm_kernel, tc=tc, nb=nb)
    const2 = lambda t: (0, 0)
    const3 = lambda t: (0, 0, 0)
    return pl.pallas_call(
        kern,
        grid=(nsteps,),
        in_specs=[
            pl.BlockSpec((nb, tc, SSM_WIDTH), lambda t: (0, t, 0)),
            pl.BlockSpec((nb, ns), const2),
            pl.BlockSpec((nb, ns), const2),
            pl.BlockSpec((SSM_BLOCKS, SSM_BLOCK_IN, 2 * SSM_BLOCK_STATE), const3),
            pl.BlockSpec((nb, ns), const2),
            pl.BlockSpec((nb, ns), const2),
            pl.BlockSpec((SSM_BLOCKS, SSM_BLOCK_STATE, SSM_BLOCK_IN), const3),
            pl.BlockSpec((SSM_BLOCKS, SSM_BLOCK_STATE, SSM_BLOCK_IN), const3),
            pl.BlockSpec((1, SSM_WIDTH), const2),
            pl.BlockSpec((SSM_WIDTH, SSM_WIDTH), const2),
            pl.BlockSpec((1, SSM_WIDTH), const2),
        ],
        out_specs=[
            pl.BlockSpec((nb, tc, SSM_WIDTH), lambda t: (0, t, 0)),
            pl.BlockSpec((nb, ns), const2),
            pl.BlockSpec((nb, ns), const2),
        ],
        out_shape=(
            jax.ShapeDtypeStruct((nb, seq, SSM_WIDTH), BF16),
            jax.ShapeDtypeStruct((nb, ns), F32),
            jax.ShapeDtypeStruct((nb, ns), F32),
        ),
        scratch_shapes=[
            pltpu.VMEM((SSM_BLOCKS, r, SSM_BLOCK_STATE), F32),
            pltpu.VMEM((SSM_BLOCKS, r, SSM_BLOCK_STATE), F32),
            pltpu.VMEM((nb, ns), F32),
            pltpu.VMEM((nb, ns), F32),
        ],
        compiler_params=pltpu.CompilerParams(
            dimension_semantics=("arbitrary",), vmem_limit_bytes=VMEM_LIMIT),
        name="ssm",
    )(u, s0r, s0i, wb, lamr, lami, wcr, wci, d, gluw, glub)


def _attn_kernel(slopes_ref, *rest, ta, tk, n_tiles, has_prefix, lam_init):
    qt_ref, k_ref, vt_ref = rest[:3]
    rest = rest[3:]
    if has_prefix:
        kh_ref, vht_ref = rest[:2]
        rest = rest[2:]
    (kb_ref, ones_ref, lq1_ref, lk1_ref, lq2_ref, lk2_ref, subg_ref,
     o_ref, q2_ref, sa_ref, sb_ref, pa_ref, pb_ref, acc_ref, m_ref, alpha_ref) = rest

    h = pl.program_id(1)
    g = pl.program_id(2) if pl.num_programs(2) > 1 else 0
    slope = slopes_ref[h]
    nsub = ta // tk
    nq = k_ref.shape[0] // ta
    tile_idx = [g + t * (nq // n_tiles) for t in range(n_tiles)]
    width = 2 * ta
    total = n_tiles * width

    slope2 = jnp.full((1, width), slope * LOG2E, F32)
    c_hi = slope2.astype(BF16).astype(F32)
    c_mid = (slope2 - c_hi).astype(BF16).astype(F32)
    c_lo = slope2 - c_hi - c_mid
    bias_row = lax.broadcasted_iota(jnp.int32, (V_DIM, width), 0)
    bias_rows = jnp.where(
        bias_row == 0, c_hi,
        jnp.where(bias_row == 1, c_mid, jnp.where(bias_row == 2, c_lo, 0.0))).astype(BF16)
    k0 = jnp.concatenate([k_ref[:tk, :], kb_ref[...]], axis=1)
    for t in range(n_tiles):
        qt = qt_ref[t]
        first = lax.broadcasted_iota(jnp.int32, qt.shape, 0) < QK_DIM
        zero = jnp.zeros_like(qt)
        q2_ref[:V_DIM, t * width:t * width + ta] = jnp.where(first, qt, zero)
        q2_ref[:V_DIM, t * width + ta:(t + 1) * width] = jnp.where(first, zero, qt)
        q2_ref[V_DIM:, t * width:(t + 1) * width] = bias_rows
        sa_ref[:, t * width:(t + 1) * width] = jnp.dot(
            k0, q2_ref[:, t * width:(t + 1) * width], preferred_element_type=F32)
    slope = slope * LOG2E

    def key_lhs(k_blk):
        return jnp.concatenate([k_blk, kb_ref[:k_blk.shape[0], :]], axis=1)

    def value_lhs(vt_blk):
        return jnp.concatenate([vt_blk, ones_ref[:, :vt_blk.shape[1]]], axis=0)

    def key_block(kidx):
        if isinstance(kidx, int):
            return k_ref[kidx * tk:(kidx + 1) * tk, :]
        return k_ref[pl.ds(pl.multiple_of(kidx * tk, tk), tk), :]

    def clamp0(n):
        return max(n, 0) if isinstance(n, int) else jnp.maximum(n, 0)

    def f32(n):
        return float(n) if isinstance(n, int) else n.astype(F32)

    def softmax_step(m_old, s, shift):
        m_new = jnp.maximum(m_old, jnp.max(s, axis=0, keepdims=True) - shift)
        alpha = jnp.exp2(m_old - m_new)
        p = jnp.exp2(s - (m_new + shift))
        return m_new, alpha, p.astype(BF16)

    pb_ref[...] = jnp.zeros(pb_ref.shape, BF16)
    acc_ref[...] = jnp.zeros(acc_ref.shape, F32)
    m_ref[...] = jnp.full((1, total), NEG_INF, F32)
    alpha_ref[...] = jnp.ones((1, total), F32)

    def make_body(col_tiles):
        def body(j, carry):
            n0 = 2 * j
            k_lhs = [key_lhs(key_block(n0 + st + 1)) for st in range(2)]
            v_lhs = [value_lhs(vt_ref[clamp0(n0 - 1)]), value_lhs(vt_ref[n0])]
            s_bufs = (sa_ref, sb_ref)
            p_bufs = (pa_ref, pb_ref)
            for ct in col_tiles:
                cs = slice(ct * COL_TILE, (ct + 1) * COL_TILE)
                q0 = tile_idx[ct * COL_TILE // width] * ta
                shift0 = slope * f32(q0 - n0 * tk)
                m_c, alpha_c, acc = m_ref[:, cs], alpha_ref[:, cs], acc_ref[:, cs]
                for st in range(2):
                    s_bufs[(st + 1) % 2][:, cs] = jnp.dot(
                        k_lhs[st], q2_ref[:, cs], preferred_element_type=F32)
                    m_c, alpha_n, p = softmax_step(
                        m_c, s_bufs[st % 2][:, cs], shift0 - slope * (st * tk))
                    p_bufs[st % 2][:, cs] = p
                    acc = acc * alpha_c + jnp.dot(
                        v_lhs[st], p_bufs[(st + 1) % 2][:, cs], preferred_element_type=F32)
                    alpha_c = alpha_n
                acc_ref[:, cs] = acc
                m_ref[:, cs] = m_c
                alpha_ref[:, cs] = alpha_c
            return carry
        return body

    if nsub == 2:
        per_tile = width // COL_TILE
        all_ct = list(range(total // COL_TILE))
        for t in range(n_tiles):
            lo = tile_idx[t - 1] if t else 0
            body = make_body(all_ct[t * per_tile:])
            if isinstance(lo, int):
                for j in range(lo, tile_idx[t]):
                    body(j, 0)
            else:
                lax.fori_loop(lo, tile_idx[t], body, 0)
    else:
        assert nsub == 1 and n_tiles == 1 and nq == 1

    dw = min(DIAG_TILE, width)
    n_dt = total // dw
    tiles = [slice(dt * dw, (dt + 1) * dw) for dt in range(n_dt)]
    owner = [dt * dw // width for dt in range(n_dt)]
    q_base = [dt * dw % width for dt in range(n_dt)]
    q_his = [min(q_base[dt] % ta + dw, ta) - 1 for dt in range(n_dt)]
    k_idx = lax.broadcasted_iota(jnp.int32, (tk, dw), 0)
    col = lax.broadcasted_iota(jnp.int32, (tk, dw), 1)
    if has_prefix:
        k_pre = key_lhs(kh_ref[...])[N_JUNK:, :]
        s_pre = [jnp.dot(k_pre, q2_ref[:, cs], preferred_element_type=F32) for cs in tiles]
    v_pend = [value_lhs(vt_ref[clamp0(ti * nsub - 1)]) for ti in tile_idx]
    m_t = [m_ref[:, cs] for cs in tiles]
    acc_t = [acc_ref[:, cs] * alpha_ref[:, cs]
             + jnp.dot(v_pend[owner[dt]], pb_ref[:, cs], preferred_element_type=F32)
             for dt, cs in enumerate(tiles)]
    s_t = {dt: sa_ref[:, cs] for dt, cs in enumerate(tiles)}
    masks = {}
    for c in range(nsub):
        live = [dt for dt in range(n_dt) if q_his[dt] >= c * tk]
        if c + 1 < nsub:
            k_next = [key_lhs(key_block(ti * nsub + c + 1)) for ti in tile_idx]
            s_next = {dt: jnp.dot(k_next[owner[dt]], q2_ref[:, tiles[dt]],
                                  preferred_element_type=F32)
                      for dt in live if q_his[dt] >= (c + 1) * tk}
        p_t = {}
        for dt in live:
            s = s_t[dt]
            lead = q_base[dt] % ta - c * tk
            causal = lead < tk - 1 or dw > ta
            if causal or not has_prefix:
                key = (lead if causal else None, dw > ta)
                if key not in masks:
                    q_idx = (col & (ta - 1)) if dw > ta else col
                    keep = q_idx - k_idx >= -lead if causal else None
                    if not has_prefix:
                        junk = k_idx >= N_JUNK - c * tk
                        keep = junk if keep is None else jnp.logical_and(keep, junk)
                    masks[key] = keep
                s = jnp.where(masks[key], s, NEG_INF)
            m_t[dt], alpha_c, p_t[dt] = softmax_step(m_t[dt], s, slope * (-c * tk))
            acc_t[dt] = acc_t[dt] * alpha_c
        v_c = [value_lhs(vt_ref[ti * nsub + c]) for ti in tile_idx]
        for dt in live:
            acc_t[dt] = acc_t[dt] + jnp.dot(v_c[owner[dt]], p_t[dt],
                                            preferred_element_type=F32)
        if c + 1 < nsub:
            s_t = s_next
    if has_prefix:
        v_pre = value_lhs(vht_ref[...])
        no_weight = jnp.zeros((N_JUNK, dw), BF16)
        for dt in range(n_dt):
            shift = slope * f32(HEAD_ROWS + tile_idx[owner[dt]] * ta)
            m_t[dt], alpha_c, p = softmax_step(m_t[dt], s_pre[dt], shift)
            p = jnp.concatenate([no_weight, p], axis=0)
            acc_t[dt] = acc_t[dt] * alpha_c + jnp.dot(v_pre, p, preferred_element_type=F32)

    lam = (jnp.exp(jnp.sum(lq1_ref[...] * lk1_ref[...], axis=-1, keepdims=True))
           - jnp.exp(jnp.sum(lq2_ref[...] * lk2_ref[...], axis=-1, keepdims=True))
           + lam_init)
    per = width // dw
    for t in range(n_tiles):
        acc = jnp.concatenate(acc_t[t * per:(t + 1) * per], axis=1)
        l = acc[V_DIM:V_DIM + 1, :]
        acc = acc[:V_DIM, :]
        o = acc[:, :ta] / l[:, :ta] - lam * (acc[:, ta:] / l[:, ta:])
        r = lax.rsqrt(jnp.mean(o * o, axis=0, keepdims=True) + EPS)
        y = ((o * r) * subg_ref[...]) * (1.0 - lam_init)
        o_ref[t] = y.T.astype(BF16)


def _attention(slopes, qt, k, vt, prefix, lq1, lk1, lq2, lk2, subg, *, ta, tk, lam_init):
    b, l, _ = k.shape
    nq = l // ta
    nkv = l // tk
    assert ta & (ta - 1) == 0 and ta % tk == 0
    has_prefix = prefix is not None
    n_tiles = math.gcd(nq, TILES_PER_STEP)
    steps = nq // n_tiles
    kern = functools.partial(_attn_kernel, ta=ta, tk=tk, n_tiles=n_tiles,
                             has_prefix=has_prefix, lam_init=lam_init)
    qt = qt.reshape(b, n_tiles, steps, QK_COLS, ta)
    in_specs = [
        pl.BlockSpec(memory_space=pltpu.SMEM),
        pl.BlockSpec((None, n_tiles, None, V_DIM, ta), lambda bi, h, g: (bi, 0, g, h, 0)),
        pl.BlockSpec((None, l, V_DIM), lambda bi, h, g: (bi, 0, h)),
        pl.BlockSpec((None, nkv, V_DIM, tk), lambda bi, h, g: (bi, 0, h, 0)),
    ]
    args = [slopes, qt, k, vt]
    if has_prefix:
        kh, vht = prefix
        assert tk >= HEAD_ROWS
        in_specs += [
            pl.BlockSpec((None, HEAD_ROWS, V_DIM), lambda bi, h, g: (0, 0, h)),
            pl.BlockSpec((None, None, V_DIM, HEAD_ROWS), lambda bi, h, g: (0, 0, h, 0)),
        ]
        args += [kh, vht]
    const2 = lambda bi, h, g: (0, 0)
    in_specs += [
        pl.BlockSpec((tk, V_DIM), const2),
        pl.BlockSpec((SUM_ROWS, tk), const2),
        pl.BlockSpec((1, QK_DIM), const2), pl.BlockSpec((1, QK_DIM), const2),
        pl.BlockSpec((1, QK_DIM), const2), pl.BlockSpec((1, QK_DIM), const2),
        pl.BlockSpec((V_DIM, 1), const2),
    ]
    assert tk <= 256
    kb = jnp.zeros((tk, V_DIM), F32).at[:, :3].set(jnp.arange(tk, dtype=F32)[:, None])
    ones = jnp.zeros((SUM_ROWS, tk), F32).at[0].set(1.0)
    args += [kb.astype(BF16), ones.astype(BF16), lq1, lk1, lq2, lk2, subg]
    total = n_tiles * 2 * ta
    out = pl.pallas_call(
        kern,
        grid=(b, ATT_HEADS, steps),
        in_specs=in_specs,
        out_specs=pl.BlockSpec((None, n_tiles, ta, V_DIM), lambda bi, h, g: (bi, 0, g, h)),
        out_shape=jax.ShapeDtypeStruct((b, n_tiles, l // n_tiles, ATT_WIDTH), BF16),
        scratch_shapes=[
            pltpu.VMEM((2 * V_DIM, total), BF16),
            pltpu.VMEM((tk, total), F32),
            pltpu.VMEM((tk, total), F32),
            pltpu.VMEM((tk, total), BF16),
            pltpu.VMEM((tk, total), BF16),
            pltpu.VMEM((V_DIM + SUM_ROWS, total), F32),
            pltpu.VMEM((1, total), F32),
            pltpu.VMEM((1, total), F32),
        ],
        compiler_params=pltpu.CompilerParams(
            dimension_semantics=("parallel", "parallel", "arbitrary"),
            vmem_limit_bytes=VMEM_LIMIT),
        name="diff_attention",
    )(*args)
    return out.reshape(b, l, ATT_WIDTH)


def _merge_kernel(x_ref, ys_ref, ya_ref, g1_ref, wg_ref, wso_ref, wao_ref, wo_ref, h1_ref):
    x = x_ref[...]
    hn = _rmsnorm(x, g1_ref[...]).astype(BF16)
    gates = jnp.dot(hn, wg_ref[...], preferred_element_type=F32)
    a = jnp.dot(ys_ref[...], wso_ref[...], preferred_element_type=F32)
    c = jnp.dot(ya_ref[...], wao_ref[...], preferred_element_type=F32)
    mixed = (jax.nn.sigmoid(gates[:, :D_MODEL]) * a
             + jax.nn.sigmoid(gates[:, D_MODEL:]) * c)
    h1_ref[...] = x + jnp.dot(mixed.astype(BF16), wo_ref[...], preferred_element_type=F32)


def _merge(x, ys, ya, g1, wg, wso, wao, wo, *, t):
    b, l, _ = x.shape
    const2 = lambda bi, i: (0, 0)
    row = lambda bi, i: (bi, i, 0)
    return pl.pallas_call(
        _merge_kernel,
        grid=(b, l // t),
        in_specs=[
            pl.BlockSpec((None, t, D_MODEL), row),
            pl.BlockSpec((None, t, SSM_WIDTH), row),
            pl.BlockSpec((None, t, ATT_WIDTH), row),
            pl.BlockSpec((1, D_MODEL), const2),
            pl.BlockSpec((D_MODEL, 2 * D_MODEL), const2, pipeline_mode=pl.Buffered(1)),
            pl.BlockSpec((SSM_WIDTH, D_MODEL), const2, pipeline_mode=pl.Buffered(1)),
            pl.BlockSpec((ATT_WIDTH, D_MODEL), const2, pipeline_mode=pl.Buffered(1)),
            pl.BlockSpec((D_MODEL, D_MODEL), const2, pipeline_mode=pl.Buffered(1)),
        ],
        out_specs=pl.BlockSpec((None, t, D_MODEL), row),
        out_shape=jax.ShapeDtypeStruct((b, l, D_MODEL), F32),
        compiler_params=pltpu.CompilerParams(
            dimension_semantics=("parallel", "parallel"), vmem_limit_bytes=VMEM_LIMIT),
        name="merge",
    )(x, ys, ya, g1, wg, wso, wao, wo)


def _ffn_kernel(h_ref, hprev_ref, hhead_ref, g2_ref, wa_ref, wg_ref, cw_ref, cb_ref, wdn_ref,
                out_ref, hn_ref, gate_ref):
    i = pl.program_id(1)
    t = h_ref.shape[0]
    h = h_ref[...]
    g2 = g2_ref[...]
    halo = jnp.where(i == 0, hhead_ref[...], hprev_ref[...])
    hn_ref[:HALO, :] = _rmsnorm(halo, g2).astype(BF16)
    hn_ref[HALO:, :] = _rmsnorm(h, g2).astype(BF16)
    hn = hn_ref[...]
    for f in range(N_FF_CHUNKS):
        cols = slice(f * FF_CHUNK, (f + 1) * FF_CHUNK)
        a = jnp.dot(hn, wa_ref[:, cols], preferred_element_type=F32)
        gate = jnp.dot(hn[HALO:], wg_ref[:, cols], preferred_element_type=F32)
        c = (a[HALO - 2:HALO - 2 + t] * cw_ref[0:1, cols]
             + a[HALO - 1:HALO - 1 + t] * cw_ref[1:2, cols]
             + a[HALO:] * cw_ref[2:3, cols] + cb_ref[:, cols])
        gate_ref[:, cols] = (_gelu(c) * gate).astype(BF16)
    out_ref[...] = h + jnp.dot(gate_ref[...], wdn_ref[...], preferred_element_type=F32)


def _ffn(h1, h1_head, g2, wa, wg, cw, cb, wdn, *, t):
    b, l, _ = h1.shape
    const2 = lambda bi, i: (0, 0)
    row = lambda bi, i: (bi, i, 0)
    per = t // HALO
    return pl.pallas_call(
        _ffn_kernel,
        grid=(b, l // t),
        in_specs=[
            pl.BlockSpec((None, t, D_MODEL), row),
            pl.BlockSpec((None, HALO, D_MODEL),
                         lambda bi, i: (bi, jnp.maximum(i * per - 1, 0), 0)),
            pl.BlockSpec((None, HALO, D_MODEL),
                         lambda bi, i: (0, HEAD_ROWS // HALO - 1, 0)),
            pl.BlockSpec((1, D_MODEL), const2),
            pl.BlockSpec((D_MODEL, D_FF), const2, pipeline_mode=pl.Buffered(1)),
            pl.BlockSpec((D_MODEL, D_FF), const2, pipeline_mode=pl.Buffered(1)),
            pl.BlockSpec((3, D_FF), const2),
            pl.BlockSpec((1, D_FF), const2),
            pl.BlockSpec((D_FF, D_MODEL), const2, pipeline_mode=pl.Buffered(1)),
        ],
        out_specs=pl.BlockSpec((None, t, D_MODEL), row),
        out_shape=jax.ShapeDtypeStruct((b, l, D_MODEL), F32),
        scratch_shapes=[
            pltpu.VMEM((HALO + t, D_MODEL), BF16),
            pltpu.VMEM((t, D_FF), BF16),
        ],
        compiler_params=pltpu.CompilerParams(
            dimension_semantics=("parallel", "arbitrary"), vmem_limit_bytes=VMEM_LIMIT),
        name="conv_ffn",
    )(h1, h1, h1_head, g2, wa, wg, cw, cb, wdn)


def _pick_tile(l, pref):
    t = min(pref, l)
    while l % t:
        t //= 2
    return t


def _layer(x, head, p, l_idx):
    bsz, seq, _ = x.shape
    lam_init = 0.8 - 0.6 * math.exp(-0.3 * l_idx)
    slopes = 2.0 ** (-8.0 * jnp.arange(1, ATT_HEADS + 1, dtype=F32) / ATT_HEADS)

    g1 = p['norm1_g'][l_idx][None]
    w_in = p['w_in'][l_idx]
    w_mix = w_in[:, :MIX_COLS].astype(BF16)
    w_gate = w_in[:, MIX_COLS:].astype(BF16)
    qg = jnp.tile(p['q_norm_g'][l_idx], QK_COLS // QK_DIM)[None]
    kg = jnp.tile(p['k_norm_g'][l_idx], QK_COLS // QK_DIM)[None]

    lam_re, lam_im, bbt_re, bbt_im = _ssm_prep(
        p['ssm_a_re'][l_idx], p['ssm_a_im'][l_idx], p['ssm_log_dt'][l_idx],
        p['ssm_b_re'][l_idx], p['ssm_b_im'][l_idx])
    wb = jnp.concatenate([_block_diag(bbt_re), _block_diag(bbt_im)], axis=-1).astype(BF16)
    wcr = _block_diag(jnp.swapaxes(p['ssm_c_re'][l_idx], 1, 2)).astype(BF16)
    wci = _block_diag(jnp.swapaxes(p['ssm_c_im'][l_idx], 1, 2)).astype(BF16)
    ns = SSM_GROUPS * SSM_STATE
    lamr = jnp.broadcast_to(lam_re.reshape(1, ns), (bsz, ns))
    lami = jnp.broadcast_to(lam_im.reshape(1, ns), (bsz, ns))
    d = p['ssm_d'][l_idx].reshape(1, SSM_WIDTH)
    gluw = p['ssm_glu_w'][l_idx].astype(BF16)
    glub = p['ssm_glu_b'][l_idx][None]

    lq1, lk1 = p['lam_q1'][l_idx][None], p['lam_k1'][l_idx][None]
    lq2, lk2 = p['lam_q2'][l_idx][None], p['lam_k2'][l_idx][None]
    subg = p['subln_g'][l_idx][:, None]
    wso = p['w_ssm_out'][l_idx].astype(BF16)
    wao = p['w_att_out'][l_idx].astype(BF16)
    wo = p['w_o'][l_idx].astype(BF16)

    t_row = _pick_tile(seq, 1024)
    ta = _pick_tile(seq, 512)
    tk = _pick_tile(seq, 256)
    tc = _pick_tile(seq, 64)

    u_h, qt_h, k_h, vt_h = _inproj(head, g1, w_mix, qg, kg, t=HEAD_ROWS, ta=HEAD_ROWS,
                                    tk=HEAD_ROWS)
    zeros_state = jnp.zeros((bsz, ns), F32)
    u_hb = jnp.broadcast_to(u_h, (bsz,) + u_h.shape[1:])
    ys_hb, s0r, s0i = _ssm(u_hb, zeros_state, zeros_state, wb, lamr, lami, wcr, wci,
                           d, gluw, glub, tc=_pick_tile(HEAD_ROWS, 64))
    ys_h = ys_hb[:1]
    ya_h = _attention(slopes, qt_h, k_h, vt_h, None, lq1, lk1, lq2, lk2, subg,
                      ta=HEAD_ROWS, tk=HEAD_ROWS, lam_init=lam_init)
    h1_head = _merge(head, ys_h, ya_h, g1, w_gate, wso, wao, wo, t=HEAD_ROWS)

    u, qt, k, vt = _inproj(x, g1, w_mix, qg, kg, t=t_row, ta=ta, tk=tk)
    ys, _, _ = _ssm(u, s0r, s0i, wb, lamr, lami, wcr, wci, d, gluw, glub, tc=tc)
    ya = _attention(slopes, qt, k, vt, (k_h, vt_h), lq1, lk1, lq2, lk2, subg,
                    ta=ta, tk=tk, lam_init=lam_init)
    h1 = _merge(x, ys, ya, g1, w_gate, wso, wao, wo, t=t_row)

    w_up = p['w_up'][l_idx]
    out = _ffn(h1, h1_head, p['norm2_g'][l_idx][None], w_up[:, :D_FF].astype(BF16),
               w_up[:, D_FF:].astype(BF16), p['conv_w'][l_idx],
               p['conv_b'][l_idx][None], p['w_down'][l_idx].astype(BF16), t=t_row)
    return out, h1_head


def kernel(x, meta_tokens, norm1_g, w_in, ssm_a_re, ssm_a_im, ssm_log_dt, ssm_b_re, ssm_b_im,
           ssm_c_re, ssm_c_im, ssm_d, ssm_glu_w, ssm_glu_b, q_norm_g, k_norm_g,
           lam_q1, lam_k1, lam_q2, lam_k2, subln_g, w_ssm_out, w_att_out, w_o,
           norm2_g, w_up, conv_w, conv_b, w_down):
    params = dict(norm1_g=norm1_g, w_in=w_in, ssm_a_re=ssm_a_re, ssm_a_im=ssm_a_im,
                  ssm_log_dt=ssm_log_dt, ssm_b_re=ssm_b_re, ssm_b_im=ssm_b_im,
                  ssm_c_re=ssm_c_re, ssm_c_im=ssm_c_im, ssm_d=ssm_d, ssm_glu_w=ssm_glu_w,
                  ssm_glu_b=ssm_glu_b, q_norm_g=q_norm_g, k_norm_g=k_norm_g,
                  lam_q1=lam_q1, lam_k1=lam_k1, lam_q2=lam_q2, lam_k2=lam_k2,
                  subln_g=subln_g, w_ssm_out=w_ssm_out, w_att_out=w_att_out, w_o=w_o,
                  norm2_g=norm2_g, w_up=w_up, conv_w=conv_w, conv_b=conv_b, w_down=w_down)
    depth = norm1_g.shape[0]
    assert depth == 1, "the head tile is only carried through one layer"
    head = jnp.concatenate(
        [jnp.zeros((N_JUNK, D_MODEL), x.dtype), meta_tokens.astype(x.dtype)], axis=0)[None]
    out, _ = _layer(x, head, params, 0)
    return out
```
